```python
import jax, jax.numpy as jnp
from jax import lax
import numpy as np

D_MODEL = 4096
BATCH = 8
SEQ = 4096
DEPTH = 1

D_RWKV = D_MODEL // 2
RWKV_HEAD = 64
RWKV_HEADS = D_RWKV // RWKV_HEAD
DECAY_RANK = 96
ICLR_RANK = 96
GATE_RANK = 256
D_CONV = D_MODEL // 2
CONV_WIDTH = 31
MEM_LEN = 256
XATTN_HEADS = 4
XATTN_HEAD_DIM = D_MODEL // XATTN_HEADS
D_FF = 4 * D_MODEL
DEEPNORM_ALPHA = float((2 * DEPTH) ** 0.25)
DEEPNORM_BETA = float((8 * DEPTH) ** -0.25)
LN_EPS = 1e-5
GN_EPS = 64e-5

N_RWKV_COLS = 3 * D_RWKV + DECAY_RANK + ICLR_RANK + GATE_RANK
N_CONV_COLS = 2 * D_CONV
N_GATE_COLS = 2 * D_MODEL
D_IN = N_RWKV_COLS + N_CONV_COLS + N_GATE_COLS
RWKV_SPLITS = [D_RWKV, 2 * D_RWKV, 3 * D_RWKV, 3 * D_RWKV + DECAY_RANK, 3 * D_RWKV + DECAY_RANK + ICLR_RANK]

kernel_name = "rwkv7_conformer_gated_hybrid_deepnorm"


def layer_norm(x, g, b, eps=LN_EPS):
    xf = x.astype(jnp.float32)
    mu = jnp.mean(xf, -1, keepdims=True)
    var = jnp.mean(jnp.square(xf - mu), -1, keepdims=True)
    return ((xf - mu) * lax.rsqrt(var + eps) * g.astype(jnp.float32) + b.astype(jnp.float32)).astype(x.dtype)


def token_shift(z, mu):
    prev = jnp.pad(z, ((0, 0), (1, 0), (0, 0)))[:, :-1]
    return z + (prev - z) * mu


def rwkv7_scan(r, decay, k, v, a, b):
    B, S, H, N = r.shape

    def step(state, inp):
        r_t, w_t, k_t, v_t, a_t, b_t = inp
        sa = jnp.einsum('bhvk,bhk->bhv', state, a_t)
        state = (state * w_t[:, :, None, :] + sa[..., None] * b_t[:, :, None, :]
                 + v_t[..., None] * k_t[:, :, None, :])
        return state, jnp.einsum('bhvk,bhk->bhv', state, r_t)

    s0 = jnp.zeros((B, H, N, N), jnp.float32)
    xs = tuple(jnp.swapaxes(t, 0, 1) for t in (r, decay, k, v, a, b))
    _, out = lax.scan(step, s0, xs)
    return jnp.swapaxes(out, 0, 1)


def rwkv7_time_mix(z, shift_mix, w0, w_up, a0, a_up, g_up, k_k, k_a, r_k, gn_g, gn_b):
    B, S, _ = z.shape
    f32 = jnp.float32
    z = token_shift(z, shift_mix)
    r, k, v, dw, da, dg = jnp.split(z, RWKV_SPLITS, axis=-1)
    w = -jax.nn.softplus(-(w0 + jnp.tanh(dw) @ w_up)) - 0.5
    decay = jnp.exp(-jnp.exp(w.astype(f32)))
    a = jax.nn.sigmoid(a0 + da @ a_up)
    g = jax.nn.sigmoid(dg) @ g_up
    heads = lambda t: t.reshape(B, S, RWKV_HEADS, RWKV_HEAD).astype(f32)
    kk = heads(k * k_k)
    kk = kk / jnp.maximum(jnp.sqrt(jnp.sum(kk * kk, -1, keepdims=True)), 1e-12)
    k = k * (1.0 + (a - 1.0) * k_a)
    rh, kh, vh, ah = heads(r), heads(k), heads(v), heads(a)
    o = rwkv7_scan(rh, heads(decay), kh, vh, -kk, kk * ah)
    mu = jnp.mean(o, -1, keepdims=True)
    var = jnp.mean(jnp.square(o - mu), -1, keepdims=True)
    o = ((o - mu) * lax.rsqrt(var + GN_EPS)).reshape(B, S, D_RWKV) * gn_g.astype(f32) + gn_b.astype(f32)
    bonus = jnp.sum(rh * kh * r_k.astype(f32), -1, keepdims=True) * vh
    o = o + bonus.reshape(B, S, D_RWKV)
    return (o * g.astype(f32)).astype(z.dtype)


def conformer_conv(z, conv_w, conv_b, ln_g, ln_b):
    u = z[..., :D_CONV] * jax.nn.sigmoid(z[..., D_CONV:])
    u = lax.conv_general_dilated(u, conv_w, window_strides=(1,), padding=[(CONV_WIDTH - 1, 0)],
                                 dimension_numbers=('NWC', 'WIO', 'NWC'),
                                 feature_group_count=D_CONV) + conv_b
    return jax.nn.silu(layer_norm(u, ln_g, ln_b))


def memory_cross_attention(h, mem_n, wq, wk, wv, wo):
    B, S, _ = h.shape
    M = mem_n.shape[1]
    q = (h @ wq).reshape(B, S, XATTN_HEADS, XATTN_HEAD_DIM)
    k = (mem_n @ wk).reshape(B, M, XATTN_HEADS, XATTN_HEAD_DIM)
    v = (mem_n @ wv).reshape(B, M, XATTN_HEADS, XATTN_HEAD_DIM)
    s = jnp.einsum('bshd,bmhd->bhsm', q, k).astype(jnp.float32) * (XATTN_HEAD_DIM ** -0.5)
    p = jax.nn.softmax(s, axis=-1).astype(v.dtype)
    o = jnp.einsum('bhsm,bmhd->bshd', p, v).reshape(B, S, D_MODEL)
    return o @ wo


def _fwd_setup_inputs(seed: int = 0) -> dict:
    key = jax.random.key(seed)
    ks = iter(jax.random.split(key, 48))
    L = DEPTH
    nrm = lambda shape, scale: jax.random.normal(next(ks), shape, jnp.float32) * scale
    gain = lambda shape: 1.0 + nrm(shape, 0.02)
    beta = DEEPNORM_BETA
    lin = jnp.linspace(0.0, 1.0, D_RWKV, dtype=jnp.float32)
    w0 = -7.0 + 5.0 * lin ** 0.85 + 0.5
    return {
        "x": nrm((BATCH, SEQ, D_MODEL), 1.0),
        "mem": nrm((BATCH, MEM_LEN, D_MODEL), 1.0),
        "w_in": nrm((L, D_MODEL, D_IN), D_MODEL ** -0.5),
        "rwkv_shift_mix": jax.random.uniform(next(ks), (L, N_RWKV_COLS), jnp.float32),
        "rwkv_w0": w0[None, :] + nrm((L, D_RWKV), 0.05),
        "rwkv_w_up": nrm((L, DECAY_RANK, D_RWKV), 0.1 * DECAY_RANK ** -0.5),
        "rwkv_a0": nrm((L, D_RWKV), 0.1),
        "rwkv_a_up": nrm((L, ICLR_RANK, D_RWKV), 0.1 * ICLR_RANK ** -0.5),
        "rwkv_g_up": nrm((L, GATE_RANK, D_RWKV), GATE_RANK ** -0.5),
        "rwkv_k_k": 0.85 + nrm((L, D_RWKV), 0.05),
        "rwkv_k_a": 1.0 + nrm((L, D_RWKV), 0.05),
        "rwkv_r_k": nrm((L, RWKV_HEADS, RWKV_HEAD), 0.1),
        "rwkv_gn_g": gain((L, D_RWKV)),
        "rwkv_gn_b": nrm((L, D_RWKV), 0.02),
        "conv_w": nrm((L, CONV_WIDTH, 1, D_CONV), CONV_WIDTH ** -0.5),
        "conv_b": nrm((L, D_CONV), 0.02),
        "conv_ln_g": gain((L, D_CONV)),
        "conv_ln_b": nrm((L, D_CONV), 0.02),
        "proj_rwkv": nrm((L, D_RWKV, D_MODEL), beta * D_RWKV ** -0.5),
        "proj_conv": nrm((L, D_CONV, D_MODEL), beta * D_CONV ** -0.5),
        "w_out": nrm((L, D_MODEL, D_MODEL), beta * D_MODEL ** -0.5),
        "ln1_g": gain((L, D_MODEL)),
        "ln1_b": nrm((L, D_MODEL), 0.02),
        "ln_mem_g": gain((D_MODEL,)),
        "ln_mem_b": nrm((D_MODEL,), 0.02),
        "xattn_wq": nrm((L, D_MODEL, D_MODEL), D_MODEL ** -0.5),
        "xattn_wk": nrm((L, D_MODEL, D_MODEL), D_MODEL ** -0.5),
        "xattn_wv": nrm((L, D_MODEL, D_MODEL), beta * D_MODEL ** -0.5),
        "xattn_wo": nrm((L, D_MODEL, D_MODEL), beta * D_MODEL ** -0.5),
        "ln2_g": gain((L, D_MODEL)),
        "ln2_b": nrm((L, D_MODEL), 0.02),
        "mlp_w1": nrm((L, D_MODEL, D_FF), beta * D_MODEL ** -0.5),
        "mlp_w2": nrm((L, D_FF, D_MODEL), beta * D_FF ** -0.5),
        "ln3_g": gain((L, D_MODEL)),
        "ln3_b": nrm((L, D_MODEL), 0.02),
    }


def _fwd_reference(x, mem, w_in, rwkv_shift_mix, rwkv_w0, rwkv_w_up, rwkv_a0, rwkv_a_up, rwkv_g_up,
              rwkv_k_k, rwkv_k_a, rwkv_r_k, rwkv_gn_g, rwkv_gn_b, conv_w, conv_b, conv_ln_g,
              conv_ln_b, proj_rwkv, proj_conv, w_out, ln1_g, ln1_b, ln_mem_g, ln_mem_b,
              xattn_wq, xattn_wk, xattn_wv, xattn_wo, ln2_g, ln2_b, mlp_w1, mlp_w2, ln3_g, ln3_b):
    alpha = DEEPNORM_ALPHA
    mem_n = layer_norm(mem, ln_mem_g, ln_mem_b)
    h = x
    for l in range(DEPTH):
        z = h @ w_in[l]
        z_rwkv = z[..., :N_RWKV_COLS]
        z_conv = z[..., N_RWKV_COLS:N_RWKV_COLS + N_CONV_COLS]
        z_gate = z[..., N_RWKV_COLS + N_CONV_COLS:]
        o_r = rwkv7_time_mix(z_rwkv, rwkv_shift_mix[l], rwkv_w0[l], rwkv_w_up[l], rwkv_a0[l],
                             rwkv_a_up[l], rwkv_g_up[l], rwkv_k_k[l], rwkv_k_a[l], rwkv_r_k[l],
                             rwkv_gn_g[l], rwkv_gn_b[l])
        o_c = conformer_conv(z_conv, conv_w[l], conv_b[l], conv_ln_g[l], conv_ln_b[l])
        gate_r = jax.nn.sigmoid(z_gate[..., :D_MODEL])
        gate_c = jax.nn.sigmoid(z_gate[..., D_MODEL:])
        merged = gate_r * (o_r @ proj_rwkv[l]) + gate_c * (o_c @ proj_conv[l])
        h = layer_norm(alpha * h + merged @ w_out[l], ln1_g[l], ln1_b[l])
        ca = memory_cross_attention(h, mem_n, xattn_wq[l], xattn_wk[l], xattn_wv[l], xattn_wo[l])
        h = layer_norm(alpha * h + ca, ln2_g[l], ln2_b[l])
        ff = jnp.square(jax.nn.relu(h @ mlp_w1[l])) @ mlp_w2[l]
        h = layer_norm(alpha * h + ff, ln3_g[l], ln3_b[l])
    return h


import jax as _jax
import jax.numpy as _jnp

TWIN_FORMAT = 'train_step'
FWD_PARAMS = ['x', 'mem', 'w_in', 'rwkv_shift_mix', 'rwkv_w0', 'rwkv_w_up', 'rwkv_a0', 'rwkv_a_up', 'rwkv_g_up', 'rwkv_k_k', 'rwkv_k_a', 'rwkv_r_k', 'rwkv_gn_g', 'rwkv_gn_b', 'conv_w', 'conv_b', 'conv_ln_g', 'conv_ln_b', 'proj_rwkv', 'proj_conv', 'w_out', 'ln1_g', 'ln1_b', 'ln_mem_g', 'ln_mem_b', 'xattn_wq', 'xattn_wk', 'xattn_wv', 'xattn_wo', 'ln2_g', 'ln2_b', 'mlp_w1', 'mlp_w2', 'ln3_g', 'ln3_b']
TWIN_WEIGHTS = ['w_in', 'rwkv_shift_mix', 'rwkv_w0', 'rwkv_w_up', 'rwkv_a0', 'rwkv_a_up', 'rwkv_g_up', 'rwkv_k_k', 'rwkv_k_a', 'rwkv_r_k', 'rwkv_gn_g', 'rwkv_gn_b', 'conv_w', 'conv_b', 'conv_ln_g', 'conv_ln_b', 'proj_rwkv', 'proj_conv', 'w_out', 'ln1_g', 'ln1_b', 'ln_mem_g', 'ln_mem_b', 'xattn_wq', 'xattn_wk', 'xattn_wv', 'xattn_wo', 'ln2_g', 'ln2_b', 'mlp_w1', 'mlp_w2', 'ln3_g', 'ln3_b']
TWIN_DIFF_INPUT = 'x'
TWIN_INPUTS = ['x', 'mem', 'w_in', 'rwkv_shift_mix', 'rwkv_w0', 'rwkv_w_up', 'rwkv_a0', 'rwkv_a_up', 'rwkv_g_up', 'rwkv_k_k', 'rwkv_k_a', 'rwkv_r_k', 'rwkv_gn_g', 'rwkv_gn_b', 'conv_w', 'conv_b', 'conv_ln_g', 'conv_ln_b', 'proj_rwkv', 'proj_conv', 'w_out', 'ln1_g', 'ln1_b', 'ln_mem_g', 'ln_mem_b', 'xattn_wq', 'xattn_wk', 'xattn_wv', 'xattn_wo', 'ln2_g', 'ln2_b', 'mlp_w1', 'mlp_w2', 'ln3_g', 'ln3_b', 'loss_target', 'm_w_in', 'm_rwkv_shift_mix', 'm_rwkv_w0', 'm_rwkv_w_up', 'm_rwkv_a0', 'm_rwkv_a_up', 'm_rwkv_g_up', 'm_rwkv_k_k', 'm_rwkv_k_a', 'm_rwkv_r_k', 'm_rwkv_gn_g', 'm_rwkv_gn_b', 'm_conv_w', 'm_conv_b', 'm_conv_ln_g', 'm_conv_ln_b', 'm_proj_rwkv', 'm_proj_conv', 'm_w_out', 'm_ln1_g', 'm_ln1_b', 'm_ln_mem_g', 'm_ln_mem_b', 'm_xattn_wq', 'm_xattn_wk', 'm_xattn_wv', 'm_xattn_wo', 'm_ln2_g', 'm_ln2_b', 'm_mlp_w1', 'm_mlp_w2', 'm_ln3_g', 'm_ln3_b', 'v_w_in', 'v_rwkv_shift_mix', 'v_rwkv_w0', 'v_rwkv_w_up', 'v_rwkv_a0', 'v_rwkv_a_up', 'v_rwkv_g_up', 'v_rwkv_k_k', 'v_rwkv_k_a', 'v_rwkv_r_k', 'v_rwkv_gn_g', 'v_rwkv_gn_b', 'v_conv_w', 'v_conv_b', 'v_conv_ln_g', 'v_conv_ln_b', 'v_proj_rwkv', 'v_proj_conv', 'v_w_out', 'v_ln1_g', 'v_ln1_b', 'v_ln_mem_g', 'v_ln_mem_b', 'v_xattn_wq', 'v_xattn_wk', 'v_xattn_wv', 'v_xattn_wo', 'v_ln2_g', 'v_ln2_b', 'v_mlp_w1', 'v_mlp_w2', 'v_ln3_g', 'v_ln3_b']
TWIN_OUTPUTS = ['loss', 'grad_x', 'grad_w_in', 'grad_rwkv_shift_mix', 'grad_rwkv_w0', 'grad_rwkv_w_up', 'grad_rwkv_a0', 'grad_rwkv_a_up', 'grad_rwkv_g_up', 'grad_rwkv_k_k', 'grad_rwkv_k_a', 'grad_rwkv_r_k', 'grad_rwkv_gn_g', 'grad_rwkv_gn_b', 'grad_conv_w', 'grad_conv_b', 'grad_conv_ln_g', 'grad_conv_ln_b', 'grad_proj_rwkv', 'grad_proj_conv', 'grad_w_out', 'grad_ln1_g', 'grad_ln1_b', 'grad_ln_mem_g', 'grad_ln_mem_b', 'grad_xattn_wq', 'grad_xattn_wk', 'grad_xattn_wv', 'grad_xattn_wo', 'grad_ln2_g', 'grad_ln2_b', 'grad_mlp_w1', 'grad_mlp_w2', 'grad_ln3_g', 'grad_ln3_b', 'delta_w_in', 'delta_rwkv_shift_mix', 'delta_rwkv_w0', 'delta_rwkv_w_up', 'delta_rwkv_a0', 'delta_rwkv_a_up', 'delta_rwkv_g_up', 'delta_rwkv_k_k', 'delta_rwkv_k_a', 'delta_rwkv_r_k', 'delta_rwkv_gn_g', 'delta_rwkv_gn_b', 'delta_conv_w', 'delta_conv_b', 'delta_conv_ln_g', 'delta_conv_ln_b', 'delta_proj_rwkv', 'delta_proj_conv', 'delta_w_out', 'delta_ln1_g', 'delta_ln1_b', 'delta_ln_mem_g', 'delta_ln_mem_b', 'delta_xattn_wq', 'delta_xattn_wk', 'delta_xattn_wv', 'delta_xattn_wo', 'delta_ln2_g', 'delta_ln2_b', 'delta_mlp_w1', 'delta_mlp_w2', 'delta_ln3_g', 'delta_ln3_b', 'new_m_w_in', 'new_m_rwkv_shift_mix', 'new_m_rwkv_w0', 'new_m_rwkv_w_up', 'new_m_rwkv_a0', 'new_m_rwkv_a_up', 'new_m_rwkv_g_up', 'new_m_rwkv_k_k', 'new_m_rwkv_k_a', 'new_m_rwkv_r_k', 'new_m_rwkv_gn_g', 'new_m_rwkv_gn_b', 'new_m_conv_w', 'new_m_conv_b', 'new_m_conv_ln_g', 'new_m_conv_ln_b', 'new_m_proj_rwkv', 'new_m_proj_conv', 'new_m_w_out', 'new_m_ln1_g', 'new_m_ln1_b', 'new_m_ln_mem_g', 'new_m_ln_mem_b', 'new_m_xattn_wq', 'new_m_xattn_wk', 'new_m_xattn_wv', 'new_m_xattn_wo', 'new_m_ln2_g', 'new_m_ln2_b', 'new_m_mlp_w1', 'new_m_mlp_w2', 'new_m_ln3_g', 'new_m_ln3_b', 'new_v_w_in', 'new_v_rwkv_shift_mix', 'new_v_rwkv_w0', 'new_v_rwkv_w_up', 'new_v_rwkv_a0', 'new_v_rwkv_a_up', 'new_v_rwkv_g_up', 'new_v_rwkv_k_k', 'new_v_rwkv_k_a', 'new_v_rwkv_r_k', 'new_v_rwkv_gn_g', 'new_v_rwkv_gn_b', 'new_v_conv_w', 'new_v_conv_b', 'new_v_conv_ln_g', 'new_v_conv_ln_b', 'new_v_proj_rwkv', 'new_v_proj_conv', 'new_v_w_out', 'new_v_ln1_g', 'new_v_ln1_b', 'new_v_ln_mem_g', 'new_v_ln_mem_b', 'new_v_xattn_wq', 'new_v_xattn_wk', 'new_v_xattn_wv', 'new_v_xattn_wo', 'new_v_ln2_g', 'new_v_ln2_b', 'new_v_mlp_w1', 'new_v_mlp_w2', 'new_v_ln3_g', 'new_v_ln3_b']
TWIN_LEAF_KINDS = {'loss': 'loss', 'grad_x': 'grad_x', 'grad_w_in': 'grad_w', 'grad_rwkv_shift_mix': 'grad_w', 'grad_rwkv_w0': 'grad_w', 'grad_rwkv_w_up': 'grad_w', 'grad_rwkv_a0': 'grad_w', 'grad_rwkv_a_up': 'grad_w', 'grad_rwkv_g_up': 'grad_w', 'grad_rwkv_k_k': 'grad_w', 'grad_rwkv_k_a': 'grad_w', 'grad_rwkv_r_k': 'grad_w', 'grad_rwkv_gn_g': 'grad_w', 'grad_rwkv_gn_b': 'grad_w', 'grad_conv_w': 'grad_w', 'grad_conv_b': 'grad_w', 'grad_conv_ln_g': 'grad_w', 'grad_conv_ln_b': 'grad_w', 'grad_proj_rwkv': 'grad_w', 'grad_proj_conv': 'grad_w', 'grad_w_out': 'grad_w', 'grad_ln1_g': 'grad_w', 'grad_ln1_b': 'grad_w', 'grad_ln_mem_g': 'grad_w', 'grad_ln_mem_b': 'grad_w', 'grad_xattn_wq': 'grad_w', 'grad_xattn_wk': 'grad_w', 'grad_xattn_wv': 'grad_w', 'grad_xattn_wo': 'grad_w', 'grad_ln2_g': 'grad_w', 'grad_ln2_b': 'grad_w', 'grad_mlp_w1': 'grad_w', 'grad_mlp_w2': 'grad_w', 'grad_ln3_g': 'grad_w', 'grad_ln3_b': 'grad_w', 'delta_w_in': 'delta_w', 'delta_rwkv_shift_mix': 'delta_w', 'delta_rwkv_w0': 'delta_w', 'delta_rwkv_w_up': 'delta_w', 'delta_rwkv_a0': 'delta_w', 'delta_rwkv_a_up': 'delta_w', 'delta_rwkv_g_up': 'delta_w', 'delta_rwkv_k_k': 'delta_w', 'delta_rwkv_k_a': 'delta_w', 'delta_rwkv_r_k': 'delta_w', 'delta_rwkv_gn_g': 'delta_w', 'delta_rwkv_gn_b': 'delta_w', 'delta_conv_w': 'delta_w', 'delta_conv_b': 'delta_w', 'delta_conv_ln_g': 'delta_w', 'delta_conv_ln_b': 'delta_w', 'delta_proj_rwkv': 'delta_w', 'delta_proj_conv': 'delta_w', 'delta_w_out': 'delta_w', 'delta_ln1_g': 'delta_w', 'delta_ln1_b': 'delta_w', 'delta_ln_mem_g': 'delta_w', 'delta_ln_mem_b': 'delta_w', 'delta_xattn_wq': 'delta_w', 'delta_xattn_wk': 'delta_w', 'delta_xattn_wv': 'delta_w', 'delta_xattn_wo': 'delta_w', 'delta_ln2_g': 'delta_w', 'delta_ln2_b': 'delta_w', 'delta_mlp_w1': 'delta_w', 'delta_mlp_w2': 'delta_w', 'delta_ln3_g': 'delta_w', 'delta_ln3_b': 'delta_w', 'new_m_w_in': 'new_m', 'new_m_rwkv_shift_mix': 'new_m', 'new_m_rwkv_w0': 'new_m', 'new_m_rwkv_w_up': 'new_m', 'new_m_rwkv_a0': 'new_m', 'new_m_rwkv_a_up': 'new_m', 'new_m_rwkv_g_up': 'new_m', 'new_m_rwkv_k_k': 'new_m', 'new_m_rwkv_k_a': 'new_m', 'new_m_rwkv_r_k': 'new_m', 'new_m_rwkv_gn_g': 'new_m', 'new_m_rwkv_gn_b': 'new_m', 'new_m_conv_w': 'new_m', 'new_m_conv_b': 'new_m', 'new_m_conv_ln_g': 'new_m', 'new_m_conv_ln_b': 'new_m', 'new_m_proj_rwkv': 'new_m', 'new_m_proj_conv': 'new_m', 'new_m_w_out': 'new_m', 'new_m_ln1_g': 'new_m', 'new_m_ln1_b': 'new_m', 'new_m_ln_mem_g': 'new_m', 'new_m_ln_mem_b': 'new_m', 'new_m_xattn_wq': 'new_m', 'new_m_xattn_wk': 'new_m', 'new_m_xattn_wv': 'new_m', 'new_m_xattn_wo': 'new_m', 'new_m_ln2_g': 'new_m', 'new_m_ln2_b': 'new_m', 'new_m_mlp_w1': 'new_m', 'new_m_mlp_w2': 'new_m', 'new_m_ln3_g': 'new_m', 'new_m_ln3_b': 'new_m', 'new_v_w_in': 'new_v', 'new_v_rwkv_shift_mix': 'new_v', 'new_v_rwkv_w0': 'new_v', 'new_v_rwkv_w_up': 'new_v', 'new_v_rwkv_a0': 'new_v', 'new_v_rwkv_a_up': 'new_v', 'new_v_rwkv_g_up': 'new_v', 'new_v_rwkv_k_k': 'new_v', 'new_v_rwkv_k_a': 'new_v', 'new_v_rwkv_r_k': 'new_v', 'new_v_rwkv_gn_g': 'new_v', 'new_v_rwkv_gn_b': 'new_v', 'new_v_conv_w': 'new_v', 'new_v_conv_b': 'new_v', 'new_v_conv_ln_g': 'new_v', 'new_v_conv_ln_b': 'new_v', 'new_v_proj_rwkv': 'new_v', 'new_v_proj_conv': 'new_v', 'new_v_w_out': 'new_v', 'new_v_ln1_g': 'new_v', 'new_v_ln1_b': 'new_v', 'new_v_ln_mem_g': 'new_v', 'new_v_ln_mem_b': 'new_v', 'new_v_xattn_wq': 'new_v', 'new_v_xattn_wk': 'new_v', 'new_v_xattn_wv': 'new_v', 'new_v_xattn_wo': 'new_v', 'new_v_ln2_g': 'new_v', 'new_v_ln2_b': 'new_v', 'new_v_mlp_w1': 'new_v', 'new_v_mlp_w2': 'new_v', 'new_v_ln3_g': 'new_v', 'new_v_ln3_b': 'new_v'}


def _forward(args):
    return _fwd_reference(*[args[k] for k in FWD_PARAMS])


def _output_shape():
    out = _jax.eval_shape(lambda: _forward(_fwd_setup_inputs(0)))
    return out.shape, out.dtype

N_MICROBATCH = 1
ADAM_LR = 0.001
ADAM_B1 = 0.9
ADAM_B2 = 0.999
ADAM_EPS = 1e-08
ADAM_WD = 0.01
ADAM_STEP = 10
PER_EXAMPLE_BATCH_AXIS = {'x': 0, 'mem': 0, 'loss_target': 0}
SHARED_INPUTS = []
_WEIGHT_DTYPES = {'w_in': _jnp.float32, 'rwkv_shift_mix': _jnp.float32, 'rwkv_w0': _jnp.float32, 'rwkv_w_up': _jnp.float32, 'rwkv_a0': _jnp.float32, 'rwkv_a_up': _jnp.float32, 'rwkv_g_up': _jnp.float32, 'rwkv_k_k': _jnp.float32, 'rwkv_k_a': _jnp.float32, 'rwkv_r_k': _jnp.float32, 'rwkv_gn_g': _jnp.float32, 'rwkv_gn_b': _jnp.float32, 'conv_w': _jnp.float32, 'conv_b': _jnp.float32, 'conv_ln_g': _jnp.float32, 'conv_ln_b': _jnp.float32, 'proj_rwkv': _jnp.float32, 'proj_conv': _jnp.float32, 'w_out': _jnp.float32, 'ln1_g': _jnp.float32, 'ln1_b': _jnp.float32, 'ln_mem_g': _jnp.float32, 'ln_mem_b': _jnp.float32, 'xattn_wq': _jnp.float32, 'xattn_wk': _jnp.float32, 'xattn_wv': _jnp.float32, 'xattn_wo': _jnp.float32, 'ln2_g': _jnp.float32, 'ln2_b': _jnp.float32, 'mlp_w1': _jnp.float32, 'mlp_w2': _jnp.float32, 'ln3_g': _jnp.float32, 'ln3_b': _jnp.float32}
MOMENT_SCALE = {'w_in': 4.471046e-03, 'rwkv_shift_mix': 1.035819e-02, 'rwkv_w0': 2.701415e-03, 'rwkv_w_up': 3.772131e-04, 'rwkv_a0': 2.895672e-03, 'rwkv_a_up': 2.669294e-03, 'rwkv_g_up': 5.583138e-03, 'rwkv_k_k': 6.216567e-03, 'rwkv_k_a': 6.481514e-03, 'rwkv_r_k': 1.161408e-02, 'rwkv_gn_g': 5.416317e-03, 'rwkv_gn_b': 1.019246e-02, 'conv_w': 6.337206e-03, 'conv_b': 1.869033e-02, 'conv_ln_g': 9.158752e-03, 'conv_ln_b': 1.161756e-02, 'proj_rwkv': 6.462045e-03, 'proj_conv': 8.299834e-03, 'w_out': 1.057258e-02, 'ln1_g': 2.455235e-01, 'ln1_b': 1.096702e-01, 'ln_mem_g': 2.086018e-03, 'ln_mem_b': 3.073309e-02, 'xattn_wq': 1.340969e-03, 'xattn_wk': 1.340074e-03, 'xattn_wv': 2.783740e-03, 'xattn_wo': 2.815572e-03, 'ln2_g': 2.454656e-01, 'ln2_b': 1.101174e-01, 'mlp_w1': 9.095979e-03, 'mlp_w2': 2.016760e-02, 'ln3_g': 7.992017e+00, 'ln3_b': 6.944493e-01}


def _to_microbatches(a, axis):
    t = _jnp.moveaxis(a, axis, 0)
    t = t.reshape((N_MICROBATCH, t.shape[0] // N_MICROBATCH) + t.shape[1:])
    return _jnp.moveaxis(t, 1, axis + 1)


def setup_inputs(seed: int = 0) -> dict:
    inp = _fwd_setup_inputs(seed)
    key = _jax.random.fold_in(_jax.random.key(seed), 7919)
    shape, _ = _output_shape()
    out = dict(inp)
    out["loss_target"] = _jax.random.normal(_jax.random.fold_in(key, 0), shape, _jnp.float32)
    for i, name in enumerate(TWIN_WEIGHTS):
        w = inp[name].astype(_jnp.float32)
        if MOMENT_SCALE is None:
            s = _jnp.sqrt(_jnp.mean(_jnp.square(w)) + 1e-30)
        else:
            s = MOMENT_SCALE[name]
        km, kv = _jax.random.split(_jax.random.fold_in(key, i + 1))
        out[name] = w
        out["m_" + name] = s * _jax.random.normal(km, w.shape, _jnp.float32)
        out["v_" + name] = (s * s) * _jax.random.uniform(kv, w.shape, _jnp.float32, 0.5, 1.5)
    if N_MICROBATCH > 1:
        for name, axis in PER_EXAMPLE_BATCH_AXIS.items():
            out[name] = _to_microbatches(out[name], axis)
    return {'x': out['x'], 'mem': out['mem'], 'w_in': out['w_in'], 'rwkv_shift_mix': out['rwkv_shift_mix'], 'rwkv_w0': out['rwkv_w0'], 'rwkv_w_up': out['rwkv_w_up'], 'rwkv_a0': out['rwkv_a0'], 'rwkv_a_up': out['rwkv_a_up'], 'rwkv_g_up': out['rwkv_g_up'], 'rwkv_k_k': out['rwkv_k_k'], 'rwkv_k_a': out['rwkv_k_a'], 'rwkv_r_k': out['rwkv_r_k'], 'rwkv_gn_g': out['rwkv_gn_g'], 'rwkv_gn_b': out['rwkv_gn_b'], 'conv_w': out['conv_w'], 'conv_b': out['conv_b'], 'conv_ln_g': out['conv_ln_g'], 'conv_ln_b': out['conv_ln_b'], 'proj_rwkv': out['proj_rwkv'], 'proj_conv': out['proj_conv'], 'w_out': out['w_out'], 'ln1_g': out['ln1_g'], 'ln1_b': out['ln1_b'], 'ln_mem_g': out['ln_mem_g'], 'ln_mem_b': out['ln_mem_b'], 'xattn_wq': out['xattn_wq'], 'xattn_wk': out['xattn_wk'], 'xattn_wv': out['xattn_wv'], 'xattn_wo': out['xattn_wo'], 'ln2_g': out['ln2_g'], 'ln2_b': out['ln2_b'], 'mlp_w1': out['mlp_w1'], 'mlp_w2': out['mlp_w2'], 'ln3_g': out['ln3_g'], 'ln3_b': out['ln3_b'], 'loss_target': out['loss_target'], 'm_w_in': out['m_w_in'], 'm_rwkv_shift_mix': out['m_rwkv_shift_mix'], 'm_rwkv_w0': out['m_rwkv_w0'], 'm_rwkv_w_up': out['m_rwkv_w_up'], 'm_rwkv_a0': out['m_rwkv_a0'], 'm_rwkv_a_up': out['m_rwkv_a_up'], 'm_rwkv_g_up': out['m_rwkv_g_up'], 'm_rwkv_k_k': out['m_rwkv_k_k'], 'm_rwkv_k_a': out['m_rwkv_k_a'], 'm_rwkv_r_k': out['m_rwkv_r_k'], 'm_rwkv_gn_g': out['m_rwkv_gn_g'], 'm_rwkv_gn_b': out['m_rwkv_gn_b'], 'm_conv_w': out['m_conv_w'], 'm_conv_b': out['m_conv_b'], 'm_conv_ln_g': out['m_conv_ln_g'], 'm_conv_ln_b': out['m_conv_ln_b'], 'm_proj_rwkv': out['m_proj_rwkv'], 'm_proj_conv': out['m_proj_conv'], 'm_w_out': out['m_w_out'], 'm_ln1_g': out['m_ln1_g'], 'm_ln1_b': out['m_ln1_b'], 'm_ln_mem_g': out['m_ln_mem_g'], 'm_ln_mem_b': out['m_ln_mem_b'], 'm_xattn_wq': out['m_xattn_wq'], 'm_xattn_wk': out['m_xattn_wk'], 'm_xattn_wv': out['m_xattn_wv'], 'm_xattn_wo': out['m_xattn_wo'], 'm_ln2_g': out['m_ln2_g'], 'm_ln2_b': out['m_ln2_b'], 'm_mlp_w1': out['m_mlp_w1'], 'm_mlp_w2': out['m_mlp_w2'], 'm_ln3_g': out['m_ln3_g'], 'm_ln3_b': out['m_ln3_b'], 'v_w_in': out['v_w_in'], 'v_rwkv_shift_mix': out['v_rwkv_shift_mix'], 'v_rwkv_w0': out['v_rwkv_w0'], 'v_rwkv_w_up': out['v_rwkv_w_up'], 'v_rwkv_a0': out['v_rwkv_a0'], 'v_rwkv_a_up': out['v_rwkv_a_up'], 'v_rwkv_g_up': out['v_rwkv_g_up'], 'v_rwkv_k_k': out['v_rwkv_k_k'], 'v_rwkv_k_a': out['v_rwkv_k_a'], 'v_rwkv_r_k': out['v_rwkv_r_k'], 'v_rwkv_gn_g': out['v_rwkv_gn_g'], 'v_rwkv_gn_b': out['v_rwkv_gn_b'], 'v_conv_w': out['v_conv_w'], 'v_conv_b': out['v_conv_b'], 'v_conv_ln_g': out['v_conv_ln_g'], 'v_conv_ln_b': out['v_conv_ln_b'], 'v_proj_rwkv': out['v_proj_rwkv'], 'v_proj_conv': out['v_proj_conv'], 'v_w_out': out['v_w_out'], 'v_ln1_g': out['v_ln1_g'], 'v_ln1_b': out['v_ln1_b'], 'v_ln_mem_g': out['v_ln_mem_g'], 'v_ln_mem_b': out['v_ln_mem_b'], 'v_xattn_wq': out['v_xattn_wq'], 'v_xattn_wk': out['v_xattn_wk'], 'v_xattn_wv': out['v_xattn_wv'], 'v_xattn_wo': out['v_xattn_wo'], 'v_ln2_g': out['v_ln2_g'], 'v_ln2_b': out['v_ln2_b'], 'v_mlp_w1': out['v_mlp_w1'], 'v_mlp_w2': out['v_mlp_w2'], 'v_ln3_g': out['v_ln3_g'], 'v_ln3_b': out['v_ln3_b']}


def _loss(weights, diff, rest, loss_target):
    with _jax.named_scope("forward"):
        args = {**rest, TWIN_DIFF_INPUT: diff, **{k: w.astype(_WEIGHT_DTYPES[k]) for k, w in weights.items()}}
        y = _forward(args)
    with _jax.named_scope("loss_head"):
        err = _jnp.square(y.astype(_jnp.float32) - loss_target)
        return 0.5 * _jnp.sum(_jnp.mean(err, axis=-1)) if err.ndim else 0.5 * err


def _adamw(w, g, m, v):
    m = ADAM_B1 * m + (1.0 - ADAM_B1) * g
    v = ADAM_B2 * v + (1.0 - ADAM_B2) * _jnp.square(g)
    m_hat = m / (1.0 - ADAM_B1 ** ADAM_STEP)
    v_hat = v / (1.0 - ADAM_B2 ** ADAM_STEP)
    delta = -ADAM_LR * (m_hat / (_jnp.sqrt(v_hat) + ADAM_EPS) + ADAM_WD * w)
    return delta, m, v


def reference(x, mem, w_in, rwkv_shift_mix, rwkv_w0, rwkv_w_up, rwkv_a0, rwkv_a_up, rwkv_g_up, rwkv_k_k, rwkv_k_a, rwkv_r_k, rwkv_gn_g, rwkv_gn_b, conv_w, conv_b, conv_ln_g, conv_ln_b, proj_rwkv, proj_conv, w_out, ln1_g, ln1_b, ln_mem_g, ln_mem_b, xattn_wq, xattn_wk, xattn_wv, xattn_wo, ln2_g, ln2_b, mlp_w1, mlp_w2, ln3_g, ln3_b, loss_target, m_w_in, m_rwkv_shift_mix, m_rwkv_w0, m_rwkv_w_up, m_rwkv_a0, m_rwkv_a_up, m_rwkv_g_up, m_rwkv_k_k, m_rwkv_k_a, m_rwkv_r_k, m_rwkv_gn_g, m_rwkv_gn_b, m_conv_w, m_conv_b, m_conv_ln_g, m_conv_ln_b, m_proj_rwkv, m_proj_conv, m_w_out, m_ln1_g, m_ln1_b, m_ln_mem_g, m_ln_mem_b, m_xattn_wq, m_xattn_wk, m_xattn_wv, m_xattn_wo, m_ln2_g, m_ln2_b, m_mlp_w1, m_mlp_w2, m_ln3_g, m_ln3_b, v_w_in, v_rwkv_shift_mix, v_rwkv_w0, v_rwkv_w_up, v_rwkv_a0, v_rwkv_a_up, v_rwkv_g_up, v_rwkv_k_k, v_rwkv_k_a, v_rwkv_r_k, v_rwkv_gn_g, v_rwkv_gn_b, v_conv_w, v_conv_b, v_conv_ln_g, v_conv_ln_b, v_proj_rwkv, v_proj_conv, v_w_out, v_ln1_g, v_ln1_b, v_ln_mem_g, v_ln_mem_b, v_xattn_wq, v_xattn_wk, v_xattn_wv, v_xattn_wo, v_ln2_g, v_ln2_b, v_mlp_w1, v_mlp_w2, v_ln3_g, v_ln3_b):
    given = dict(x=x, mem=mem, w_in=w_in, rwkv_shift_mix=rwkv_shift_mix, rwkv_w0=rwkv_w0, rwkv_w_up=rwkv_w_up, rwkv_a0=rwkv_a0, rwkv_a_up=rwkv_a_up, rwkv_g_up=rwkv_g_up, rwkv_k_k=rwkv_k_k, rwkv_k_a=rwkv_k_a, rwkv_r_k=rwkv_r_k, rwkv_gn_g=rwkv_gn_g, rwkv_gn_b=rwkv_gn_b, conv_w=conv_w, conv_b=conv_b, conv_ln_g=conv_ln_g, conv_ln_b=conv_ln_b, proj_rwkv=proj_rwkv, proj_conv=proj_conv, w_out=w_out, ln1_g=ln1_g, ln1_b=ln1_b, ln_mem_g=ln_mem_g, ln_mem_b=ln_mem_b, xattn_wq=xattn_wq, xattn_wk=xattn_wk, xattn_wv=xattn_wv, xattn_wo=xattn_wo, ln2_g=ln2_g, ln2_b=ln2_b, mlp_w1=mlp_w1, mlp_w2=mlp_w2, ln3_g=ln3_g, ln3_b=ln3_b, loss_target=loss_target, m_w_in=m_w_in, m_rwkv_shift_mix=m_rwkv_shift_mix, m_rwkv_w0=m_rwkv_w0, m_rwkv_w_up=m_rwkv_w_up, m_rwkv_a0=m_rwkv_a0, m_rwkv_a_up=m_rwkv_a_up, m_rwkv_g_up=m_rwkv_g_up, m_rwkv_k_k=m_rwkv_k_k, m_rwkv_k_a=m_rwkv_k_a, m_rwkv_r_k=m_rwkv_r_k, m_rwkv_gn_g=m_rwkv_gn_g, m_rwkv_gn_b=m_rwkv_gn_b, m_conv_w=m_conv_w, m_conv_b=m_conv_b, m_conv_ln_g=m_conv_ln_g, m_conv_ln_b=m_conv_ln_b, m_proj_rwkv=m_proj_rwkv, m_proj_conv=m_proj_conv, m_w_out=m_w_out, m_ln1_g=m_ln1_g, m_ln1_b=m_ln1_b, m_ln_mem_g=m_ln_mem_g, m_ln_mem_b=m_ln_mem_b, m_xattn_wq=m_xattn_wq, m_xattn_wk=m_xattn_wk, m_xattn_wv=m_xattn_wv, m_xattn_wo=m_xattn_wo, m_ln2_g=m_ln2_g, m_ln2_b=m_ln2_b, m_mlp_w1=m_mlp_w1, m_mlp_w2=m_mlp_w2, m_ln3_g=m_ln3_g, m_ln3_b=m_ln3_b, v_w_in=v_w_in, v_rwkv_shift_mix=v_rwkv_shift_mix, v_rwkv_w0=v_rwkv_w0, v_rwkv_w_up=v_rwkv_w_up, v_rwkv_a0=v_rwkv_a0, v_rwkv_a_up=v_rwkv_a_up, v_rwkv_g_up=v_rwkv_g_up, v_rwkv_k_k=v_rwkv_k_k, v_rwkv_k_a=v_rwkv_k_a, v_rwkv_r_k=v_rwkv_r_k, v_rwkv_gn_g=v_rwkv_gn_g, v_rwkv_gn_b=v_rwkv_gn_b, v_conv_w=v_conv_w, v_conv_b=v_conv_b, v_conv_ln_g=v_conv_ln_g, v_conv_ln_b=v_conv_ln_b, v_proj_rwkv=v_proj_rwkv, v_proj_conv=v_proj_conv, v_w_out=v_w_out, v_ln1_g=v_ln1_g, v_ln1_b=v_ln1_b, v_ln_mem_g=v_ln_mem_g, v_ln_mem_b=v_ln_mem_b, v_xattn_wq=v_xattn_wq, v_xattn_wk=v_xattn_wk, v_xattn_wv=v_xattn_wv, v_xattn_wo=v_xattn_wo, v_ln2_g=v_ln2_g, v_ln2_b=v_ln2_b, v_mlp_w1=v_mlp_w1, v_mlp_w2=v_mlp_w2, v_ln3_g=v_ln3_g, v_ln3_b=v_ln3_b)
    weights = {n: given[n] for n in TWIN_WEIGHTS}
    shared = {n: given[n] for n in SHARED_INPUTS}
    per_example = {n: given[n] for n in ['x', 'mem']}
    grad_fn = _jax.value_and_grad(_loss, argnums=(0, 1))

    def one_microbatch(ex, loss_target):
        ex = dict(ex)
        diff = ex.pop(TWIN_DIFF_INPUT)
        return grad_fn(weights, diff, {**shared, **ex}, loss_target)

    if N_MICROBATCH == 1:
        loss, (grad_w, grad_x) = one_microbatch(per_example, given["loss_target"])
    else:
        def body(carry, xs):
            loss_sum, grad_sum = carry
            l_k, (gw_k, gx_k) = one_microbatch(xs[0], xs[1])
            with _jax.named_scope("update"):
                return (loss_sum + l_k, _jax.tree.map(_jnp.add, grad_sum, gw_k)), gx_k

        init = (_jnp.zeros((), _jnp.float32), _jax.tree.map(_jnp.zeros_like, weights))
        (loss, grad_w), grad_x = _jax.lax.scan(body, init, (per_example, given["loss_target"]))
    with _jax.named_scope("update"):
        delta_w, new_m, new_v = {}, {}, {}
        for n in TWIN_WEIGHTS:
            delta_w[n], new_m[n], new_v[n] = _adamw(weights[n], grad_w[n], given["m_" + n], given["v_" + n])
    return (loss, grad_x, *[grad_w[n] for n in TWIN_WEIGHTS], *[delta_w[n] for n in TWIN_WEIGHTS],
            *[new_m[n] for n in TWIN_WEIGHTS], *[new_v[n] for n in TWIN_WEIGHTS])
```

```python
import functools

import jax
import jax.numpy as jnp
from jax import lax
from jax.experimental import pallas as pl
from jax.experimental.pallas import tpu as pltpu

F32, BF16 = jnp.float32, jnp.bfloat16
ALPHA = 2.0 ** 0.25
LN_EPS = 1e-5
GN_EPS = 64e-5
HEAD = 64
LANES = 128
PAIR = 2 * HEAD
XATTN_HEADS = 4
CONV_WIDTH = 31
CONV_HALO = 32
N_DECAY, N_ICLR, N_GATE = 96, 96, 256
N_LORA_PAD = 512
VMEM_LIMIT = 56 * 1024 * 1024
ADAM_LR, ADAM_B1, ADAM_B2, ADAM_EPS, ADAM_WD, ADAM_STEP = 0.001, 0.9, 0.999, 1e-8, 0.01, 10
MESH = pl.DeviceIdType.MESH
HI = lax.Precision.HIGHEST


def _params(**kw):
    return pltpu.CompilerParams(vmem_limit_bytes=VMEM_LIMIT, **kw)


def _tile(n, pref, unit=LANES):
    if n <= pref:
        return n
    t = pref
    while n % t:
        t -= unit
    return t


def mm(a, b, *, name, ta=False, tb=False, out_dtype=F32, acc=None):
    m, k = (a.shape[1], a.shape[0]) if ta else a.shape
    n = b.shape[0] if tb else b.shape[1]
    tm, tn, tk = _tile(m, 1024), _tile(n, 1024), _tile(k, 512)
    nk = k // tk
    a_spec = pl.BlockSpec((tk, tm), lambda i, j, kk: (kk, i)) if ta else pl.BlockSpec((tm, tk), lambda i, j, kk: (i, kk))
    b_spec = pl.BlockSpec((tn, tk), lambda i, j, kk: (j, kk)) if tb else pl.BlockSpec((tk, tn), lambda i, j, kk: (kk, j))
    o_spec = pl.BlockSpec((tm, tn), lambda i, j, kk: (i, j))
    dims = (((0 if ta else 1,), (1 if tb else 0,)), ((), ()))

    def body(*refs):
        if acc is None:
            a_ref, b_ref, o_ref, acc_ref = refs
        else:
            a_ref, b_ref, c_ref, o_ref, acc_ref = refs
        kk = pl.program_id(2)

        @pl.when(kk == 0)
        def _():
            acc_ref[...] = jnp.zeros_like(acc_ref) if acc is None else c_ref[...].astype(F32)

        acc_ref[...] += lax.dot_general(a_ref[...].astype(BF16), b_ref[...].astype(BF16), dims,
                                        preferred_element_type=F32)

        @pl.when(kk == nk - 1)
        def _():
            o_ref[...] = acc_ref[...].astype(o_ref.dtype)

    ins = [a, b] + ([] if acc is None else [acc])
    in_specs = [a_spec, b_spec] + ([] if acc is None else [o_spec])
    return pl.pallas_call(
        body, name=name, grid=(m // tm, n // tn, nk),
        in_specs=in_specs, out_specs=o_spec,
        out_shape=jax.ShapeDtypeStruct((m, n), out_dtype),
        scratch_shapes=[pltpu.VMEM((tm, tn), F32)],
        input_output_aliases={} if acc is None else {2: 0},
        compiler_params=_params(dimension_semantics=("arbitrary", "arbitrary", "arbitrary")),
    )(*ins)


def rowcall(fn, rows, consts, out_rows, out_accs, *, tm, name, scratch=()):
    rows = [r if isinstance(r, tuple) else (r, None) for r in rows]
    s = rows[0][0].shape[0]
    n = s // tm
    in_specs, ins = [], []
    for arr, halo in rows:
        w = arr.shape[1]
        in_specs.append(pl.BlockSpec((tm, w), lambda i: (i, 0)))
        ins.append(arr)
        if halo is not None:
            kind, h = halo
            per = tm // h
            if kind == "prev":
                in_specs.append(pl.BlockSpec((h, w), lambda i, per=per: (jnp.maximum(i * per - 1, 0), 0)))
            else:
                in_specs.append(pl.BlockSpec((h, w), lambda i, per=per, last=s // h - 1: (jnp.minimum((i + 1) * per, last), 0)))
            ins.append(arr)
    for cst in consts:
        in_specs.append(pl.BlockSpec(cst.shape, lambda i, nd=cst.ndim: (0,) * nd))
        ins.append(cst)
    out_specs = [pl.BlockSpec((tm, w), lambda i: (i, 0)) for w, _ in out_rows]
    out_specs += [pl.BlockSpec(shp, lambda i, nd=len(shp): (0,) * nd) for shp, _ in out_accs]
    out_shape = [jax.ShapeDtypeStruct((s, w), dt) for w, dt in out_rows]
    out_shape += [jax.ShapeDtypeStruct(shp, dt) for shp, dt in out_accs]
    n_in, n_or, n_oa = len(ins), len(out_rows), len(out_accs)

    def body(*refs):
        i = pl.program_id(0)
        it = iter(refs[:n_in])
        row_vals = []
        for _, halo in rows:
            cur = next(it)[...]
            row_vals.append(cur if halo is None else (cur, next(it)[...]))
        const_vals = [r[...] for r in it]
        o_refs = refs[n_in:n_in + n_or]
        a_refs = refs[n_in + n_or:n_in + n_or + n_oa]
        outs, parts = fn(i, n, row_vals, const_vals, refs[n_in + n_or + n_oa:])
        for o_ref, val in zip(o_refs, outs, strict=True):
            o_ref[...] = val.astype(o_ref.dtype)
        for a_ref, part in zip(a_refs, parts, strict=True):
            part = jnp.broadcast_to(part, a_ref.shape).astype(a_ref.dtype)

            @pl.when(i == 0)
            def _(a_ref=a_ref, part=part):
                a_ref[...] = part

            @pl.when(i > 0)
            def _(a_ref=a_ref, part=part):
                a_ref[...] += part

    res = pl.pallas_call(
        body, name=name, grid=(n,), in_specs=in_specs, out_specs=out_specs, out_shape=out_shape,
        scratch_shapes=list(scratch),
        compiler_params=_params(dimension_semantics=("arbitrary",)),
    )(*ins)
    return res[:n_or], res[n_or:]


def stage_fwd(f, rows, params, consts, outs, *, tm, name):
    n_p = len(params)

    def fn(i, n, rv, cv, sc):
        return f([r.astype(F32) for r in rv], cv[:n_p], cv[n_p:]), []

    return rowcall(fn, rows, list(params) + list(consts), outs, [], tm=tm, name=name)[0]


def stage_bwd(f, rows, params, consts, cts, row_grads, *, tm, name):
    n_r, n_p = len(rows), len(params)

    def fn(i, n, rv, cv, sc):
        r = [x.astype(F32) for x in rv[:n_r]]
        ct = [x.astype(F32) for x in rv[n_r:]]
        _, vjp = jax.vjp(lambda r_, p_: f(r_, p_, cv[n_p:]), r, list(cv[:n_p]))
        d_r, d_p = vjp(ct)
        return [d_r[k] for k, _ in row_grads], d_p

    return rowcall(fn, list(rows) + list(cts), list(params) + list(consts),
                   [(rows[k].shape[1], dt) for k, dt in row_grads],
                   [(p.shape, F32) for p in params], tm=tm, name=name)


def _ln(x, g, b, eps=LN_EPS):
    xc = x - jnp.mean(x, -1, keepdims=True)
    var = jnp.mean(xc * xc, -1, keepdims=True)
    return xc * lax.rsqrt(var + eps) * g + b


def _sigmoid(x):
    return 1.0 / (1.0 + jnp.exp(-x))


def _softplus(x):
    return jnp.maximum(x, 0.0) + jnp.log(1.0 + jnp.exp(-jnp.abs(x)))


def _bdot(a, b):
    return jnp.dot(a.astype(BF16), b.astype(BF16), preferred_element_type=F32)


def _head_sum(x, e, et):
    return jnp.dot(jnp.dot(x, e, precision=HI, preferred_element_type=F32), et, precision=HI,
                   preferred_element_type=F32)


def f_rwkv_pre(rows, params, consts):
    zk, zl = rows
    w0, w_up, a0, a_up, g_up, k_k, k_a = params
    e, et = consts
    w = -_softplus(-(w0 + _bdot(jnp.tanh(zl[:, 0:LANES]), w_up))) - 0.5
    log_decay = -jnp.exp(w)
    a = _sigmoid(a0 + _bdot(zl[:, 0:2 * LANES], a_up))
    g = _bdot(_sigmoid(zl[:, LANES:N_LORA_PAD]), g_up)
    kk = zk * k_k
    kk = kk / jnp.maximum(jnp.sqrt(_head_sum(kk * kk, e, et)), 1e-12)
    return [zk * (1.0 + (a - 1.0) * k_a), log_decay, -kk, kk * a, g]


def f_rwkv_post(rows, params, consts):
    o, r, k, v, g = rows
    r_k, gn_g, gn_b = params
    e, et = consts
    oc = o - _head_sum(o, e, et) * (1.0 / HEAD)
    var = _head_sum(oc * oc, e, et) * (1.0 / HEAD)
    on = oc * lax.rsqrt(var + GN_EPS) * gn_g + gn_b
    return [(on + _head_sum(r * k * r_k, e, et) * v) * g]


def f_glu(rows, params, consts):
    return [rows[0] * _sigmoid(rows[1])]


def f_convln(rows, params, consts):
    x = _ln(rows[0], params[0], params[1])
    return [x * _sigmoid(x)]


def f_merge(rows, params, consts):
    gr, gc, pr, pc = rows
    return [_sigmoid(gr) * pr + _sigmoid(gc) * pc]


def f_resln(rows, params, consts):
    return [_ln(ALPHA * rows[0] + rows[1], params[0], params[1])]


def f_ln(rows, params, consts):
    return [_ln(rows[0], params[0], params[1])]


def f_attn(rows, params, consts):
    q, (k, v) = rows[0], params
    dh = q.shape[1] // XATTN_HEADS
    outs = []
    for h in range(XATTN_HEADS):
        sl = slice(h * dh, (h + 1) * dh)
        s = lax.dot_general(q[:, sl].astype(BF16), k[:, sl].astype(BF16), (((1,), (1,)), ((), ())),
                            preferred_element_type=F32) * dh ** -0.5
        p = jnp.exp(s - jnp.max(s, -1, keepdims=True))
        p = p / jnp.sum(p, -1, keepdims=True)
        outs.append(_bdot(p, v[:, sl]))
    return [jnp.concatenate(outs, axis=-1)]


def f_relu2(rows, params, consts):
    return [jnp.square(jnp.maximum(rows[0], 0.0))]


def loss_bwd(h, y, tgt, g, b, *, tm, name):
    def fn(i, n, rv, cv, sc):
        def loss(h_, y_, g_, b_):
            err = _ln(ALPHA * h_ + y_, g_, b_) - rv[2]
            return 0.5 * jnp.sum(jnp.mean(err * err, -1, keepdims=True))
        val, vjp = jax.vjp(loss, rv[0], rv[1], cv[0], cv[1])
        dh, dy, dg, db = vjp(jnp.ones((), F32))
        return [dh, dy], [val.reshape(1, 1), dg, db]

    w = h.shape[1]
    (dh, dy), (val, dg, db) = rowcall(fn, [h, y, tgt], [g, b], [(w, F32), (w, F32)],
                                      [((8, LANES), F32), (g.shape, F32), (b.shape, F32)], tm=tm, name=name)
    return val, dh, dy, dg, db


def _shift_down(cur, halo, first):
    rolled = pltpu.roll(cur, 1, 0)
    row0 = jnp.where(first, 0.0, halo[halo.shape[0] - 1:, :])
    return jnp.where(lax.broadcasted_iota(jnp.int32, cur.shape, 0) == 0, row0, rolled)


def _shift_up(cur, halo, last):
    rolled = pltpu.roll(cur, cur.shape[0] - 1, 0)
    rown = jnp.where(last, 0.0, halo[0:1, :])
    return jnp.where(lax.broadcasted_iota(jnp.int32, cur.shape, 0) == cur.shape[0] - 1, rown, rolled)


def tokenshift_fwd(zs, mus, *, tm, name):
    def fn(i, n, rv, cv, sc):
        return [z + (_shift_down(z, halo, i == 0) - z) * mu for (z, halo), mu in zip(rv, cv)], []

    return rowcall(fn, [(z, ("prev", 8)) for z in zs], mus, [(z.shape[1], F32) for z in zs], [], tm=tm, name=name)[0]


def tokenshift_bwd(dzs, zs, mus, *, tm, name):
    nz = len(zs)

    def fn(i, n, rv, cv, sc):
        outs, parts = [], []
        for (dz, dnext), (z, zprev), mu in zip(rv[:nz], rv[nz:], cv):
            g = dz * mu
            outs.append(dz - g + _shift_up(g, dnext * mu, i == n - 1))
            parts.append(jnp.sum(dz * (_shift_down(z, zprev, i == 0) - z), 0, keepdims=True))
        return outs, parts

    return rowcall(fn, [(d, ("next", 8)) for d in dzs] + [(z, ("prev", 8)) for z in zs], mus,
                   [(z.shape[1], F32) for z in zs], [(mu.shape, F32) for mu in mus], tm=tm, name=name)


def conv_fwd(u, w, b, *, tm, name):
    c = u.shape[1]

    def fn(i, n, rv, cv, sc):
        (cur, halo), (ext,) = rv[0], sc
        ext[0:CONV_HALO, :] = jnp.where(i == 0, 0.0, halo)
        ext[CONV_HALO:, :] = cur
        wv = cv[0]
        acc = jnp.broadcast_to(cv[1], cur.shape)
        for j in range(CONV_WIDTH):
            acc = acc + wv[j:j + 1, :] * ext[pl.ds(CONV_HALO - CONV_WIDTH + 1 + j, tm), :]
        return [acc], []

    return rowcall(fn, [(u, ("prev", CONV_HALO))], [w, b], [(c, F32)], [], tm=tm, name=name,
                   scratch=[pltpu.VMEM((tm + CONV_HALO, c), F32)])[0][0]


def conv_bwd(dc, u, w, *, tm, name):
    c = u.shape[1]

    def fn(i, n, rv, cv, sc):
        (dcur, dnext), (ucur, uprev) = rv
        dext, uext, dw_ref = sc
        dext[0:tm, :] = dcur
        dext[tm:, :] = jnp.where(i == n - 1, 0.0, dnext)
        uext[0:CONV_HALO, :] = jnp.where(i == 0, 0.0, uprev)
        uext[CONV_HALO:, :] = ucur
        wv = cv[0]
        du = jnp.zeros_like(dcur)
        dw_ref[...] = jnp.zeros_like(dw_ref)
        for j in range(CONV_WIDTH):
            du = du + wv[j:j + 1, :] * dext[pl.ds(CONV_WIDTH - 1 - j, tm), :]
            dw_ref[j:j + 1, :] = jnp.sum(dcur * uext[pl.ds(CONV_HALO - CONV_WIDTH + 1 + j, tm), :], 0, keepdims=True)
        return [du], [dw_ref[...], jnp.sum(dcur, 0, keepdims=True)]

    (du,), (dw, db) = rowcall(
        fn, [(dc, ("next", CONV_HALO)), (u, ("prev", CONV_HALO))], [w], [(c, F32)],
        [((CONV_HALO, c), F32), ((1, c), F32)], tm=tm, name=name,
        scratch=[pltpu.VMEM((tm + CONV_HALO, c), F32), pltpu.VMEM((tm + CONV_HALO, c), F32),
                 pltpu.VMEM((CONV_HALO, c), F32)])
    return du, dw, db


SCAN_TB = 128


def _seg_cols(x, left):
    return (jnp.sum(jnp.where(left, x, 0.0), 1, keepdims=True), jnp.sum(jnp.where(left, 0.0, x), 1, keepdims=True))


def _seg_bcast(x, left):
    c0, c1 = _seg_cols(x, left)
    return jnp.where(left, c0, c1)


def _col_of(xt, onehot, left):
    col = jnp.sum(jnp.where(onehot, xt, 0.0), 1, keepdims=True)
    return jnp.where(left, col[0:HEAD], col[HEAD:PAIR])


def scan_fwd(r, lw, k, v, a, b, *, pg, name):
    s, c = r.shape
    tb, lw_ = SCAN_TB, PAIR * pg
    ng, nt = c // lw_, s // tb
    blk = pl.BlockSpec((tb, lw_), lambda g, t: (t, g))

    def body(r_ref, lw_ref, k_ref, v_ref, a_ref, b_ref, o_ref, ck_ref, s_ref, vt_ref, ot_ref):
        @pl.when(pl.program_id(1) == 0)
        def _():
            s_ref[...] = jnp.zeros_like(s_ref)

        ck_ref[0] = s_ref[...]
        left = lax.broadcasted_iota(jnp.int32, (HEAD, PAIR), 1) < HEAD
        lane_t = lax.broadcasted_iota(jnp.int32, (PAIR, tb), 1)
        for p in range(pg):
            vt_ref[p] = v_ref[:, p * PAIR:(p + 1) * PAIR].T
        ot_ref[...] = jnp.zeros_like(ot_ref)

        def step8(t8, carry):
            base = pl.multiple_of(t8 * 8, 8)
            rows = pl.ds(base, 8)
            r8, k8, a8, b8 = r_ref[rows, :], k_ref[rows, :], a_ref[rows, :], b_ref[rows, :]
            w8 = jnp.exp(lw_ref[rows, :])
            for j in range(8):
                onehot = lane_t == base + j
                for p in range(pg):
                    ls = slice(p * PAIR, (p + 1) * PAIR)
                    row = lambda x8: x8[j:j + 1, ls]
                    st = s_ref[p * HEAD:(p + 1) * HEAD, :]
                    vcol = _col_of(vt_ref[p], onehot, left)
                    sa = _seg_bcast(st * row(a8), left)
                    st = st * row(w8) + sa * row(b8) + vcol * row(k8)
                    s_ref[p * HEAD:(p + 1) * HEAD, :] = st
                    o0, o1 = _seg_cols(st * row(r8), left)
                    ot_ref[p, 0:HEAD, :] = jnp.where(onehot[0:HEAD], o0, ot_ref[p, 0:HEAD, :])
                    ot_ref[p, HEAD:PAIR, :] = jnp.where(onehot[0:HEAD], o1, ot_ref[p, HEAD:PAIR, :])
            return carry

        lax.fori_loop(0, tb // 8, step8, 0)
        for p in range(pg):
            o_ref[:, p * PAIR:(p + 1) * PAIR] = ot_ref[p].T

    return pl.pallas_call(
        body, name=name, grid=(ng, nt), in_specs=[blk] * 6,
        out_specs=[blk, pl.BlockSpec((1, pg * HEAD, PAIR), lambda g, t: (t, g, 0))],
        out_shape=[jax.ShapeDtypeStruct((s, c), F32), jax.ShapeDtypeStruct((nt, c // 2, PAIR), F32)],
        scratch_shapes=[pltpu.VMEM((pg * HEAD, PAIR), F32), pltpu.VMEM((pg, PAIR, tb), F32),
                        pltpu.VMEM((pg, PAIR, tb), F32)],
        compiler_params=_params(dimension_semantics=("arbitrary", "arbitrary")),
    )(r, lw, k, v, a, b)


def scan_bwd(r, lw, k, v, a, b, ck, do, dr_in, dk_in, dv_in, *, pg, name):
    s, c = r.shape
    tb, lw_ = SCAN_TB, PAIR * pg
    ng, nt = c // lw_, s // tb
    blk = pl.BlockSpec((tb, lw_), lambda g, t: (nt - 1 - t, g))
    ck_spec = pl.BlockSpec((1, pg * HEAD, PAIR), lambda g, t: (nt - 1 - t, g, 0))

    def body(r_ref, lw_ref, k_ref, v_ref, a_ref, b_ref, ck_ref, do_ref, dri_ref, dki_ref, dvi_ref,
             dr_ref, dlw_ref, dk_ref, dv_ref, da_ref, db_ref,
             s_ref, ds_ref, vt_ref, dot_ref, dvt_ref, sall_ref, saall_ref):
        @pl.when(pl.program_id(1) == 0)
        def _():
            ds_ref[...] = jnp.zeros_like(ds_ref)

        left = lax.broadcasted_iota(jnp.int32, (HEAD, PAIR), 1) < HEAD
        lane_t = lax.broadcasted_iota(jnp.int32, (PAIR, tb), 1)
        sub8 = lax.broadcasted_iota(jnp.int32, (8, PAIR), 0)
        for p in range(pg):
            vt_ref[p] = v_ref[:, p * PAIR:(p + 1) * PAIR].T
            dot_ref[p] = do_ref[:, p * PAIR:(p + 1) * PAIR].T
        dvt_ref[...] = jnp.zeros_like(dvt_ref)
        s_ref[...] = ck_ref[0]

        def fwd8(t8, carry):
            base = pl.multiple_of(t8 * 8, 8)
            rows = pl.ds(base, 8)
            k8, a8, b8 = k_ref[rows, :], a_ref[rows, :], b_ref[rows, :]
            w8 = jnp.exp(lw_ref[rows, :])
            for j in range(8):
                onehot = lane_t == base + j
                for p in range(pg):
                    ls = slice(p * PAIR, (p + 1) * PAIR)
                    hs = slice(p * HEAD, (p + 1) * HEAD)
                    row = lambda x8: x8[j:j + 1, ls]
                    st = s_ref[hs, :]
                    sa = _seg_bcast(st * row(a8), left)
                    sall_ref[base + j, hs, :] = st
                    saall_ref[base + j, hs, :] = sa
                    s_ref[hs, :] = st * row(w8) + sa * row(b8) + _col_of(vt_ref[p], onehot, left) * row(k8)
            return carry

        lax.fori_loop(0, tb // 8, fwd8, 0)

        def bwd8(i8, carry):
            base = pl.multiple_of((tb // 8 - 1 - i8) * 8, 8)
            rows = pl.ds(base, 8)
            r8, k8, a8, b8 = r_ref[rows, :], k_ref[rows, :], a_ref[rows, :], b_ref[rows, :]
            w8 = jnp.exp(lw_ref[rows, :])
            acc = [[jnp.zeros((8, PAIR), F32) for _ in range(5)] for _ in range(pg)]
            for j in reversed(range(8)):
                onehot = lane_t == base + j
                for p in range(pg):
                    ls = slice(p * PAIR, (p + 1) * PAIR)
                    hs = slice(p * HEAD, (p + 1) * HEAD)
                    row = lambda x8: x8[j:j + 1, ls]
                    sp, sa = sall_ref[base + j, hs, :], saall_ref[base + j, hs, :]
                    vcol = _col_of(vt_ref[p], onehot, left)
                    docol = _col_of(dot_ref[p], onehot, left)
                    st = sp * row(w8) + sa * row(b8) + vcol * row(k8)
                    ds = ds_ref[hs, :] + docol * row(r8)
                    dsa = _seg_bcast(ds * row(b8), left)
                    dv0, dv1 = _seg_cols(ds * row(k8), left)
                    dvt_ref[p, 0:HEAD, :] = jnp.where(onehot[0:HEAD], dv0, dvt_ref[p, 0:HEAD, :])
                    dvt_ref[p, HEAD:PAIR, :] = jnp.where(onehot[0:HEAD], dv1, dvt_ref[p, HEAD:PAIR, :])
                    new = [jnp.sum(st * docol, 0, keepdims=True),
                           jnp.sum(ds * sp, 0, keepdims=True),
                           jnp.sum(ds * vcol, 0, keepdims=True),
                           jnp.sum(sp * dsa, 0, keepdims=True),
                           jnp.sum(ds * sa, 0, keepdims=True)]
                    acc[p] = [jnp.where(sub8 == j, n_, a_) for n_, a_ in zip(new, acc[p])]
                    ds_ref[hs, :] = ds * row(w8) + dsa * row(a8)
            for p in range(pg):
                ls = slice(p * PAIR, (p + 1) * PAIR)
                dr_ref[rows, ls] = acc[p][0] + dri_ref[rows, ls]
                dlw_ref[rows, ls] = acc[p][1] * w8[:, ls]
                dk_ref[rows, ls] = acc[p][2] + dki_ref[rows, ls]
                da_ref[rows, ls] = acc[p][3]
                db_ref[rows, ls] = acc[p][4]
            return carry

        lax.fori_loop(0, tb // 8, bwd8, 0)
        for p in range(pg):
            ls = slice(p * PAIR, (p + 1) * PAIR)
            dv_ref[:, ls] = dvt_ref[p].T + dvi_ref[:, ls]

    return pl.pallas_call(
        body, name=name, grid=(ng, nt), in_specs=[blk] * 6 + [ck_spec] + [blk] * 4, out_specs=[blk] * 6,
        out_shape=[jax.ShapeDtypeStruct((s, c), F32)] * 6,
        scratch_shapes=[pltpu.VMEM((pg * HEAD, PAIR), F32), pltpu.VMEM((pg * HEAD, PAIR), F32),
                        pltpu.VMEM((pg, PAIR, tb), F32), pltpu.VMEM((pg, PAIR, tb), F32),
                        pltpu.VMEM((pg, PAIR, tb), F32),
                        pltpu.VMEM((tb, pg * HEAD, PAIR), F32), pltpu.VMEM((tb, pg * HEAD, PAIR), F32)],
        compiler_params=_params(dimension_semantics=("arbitrary", "arbitrary")),
    )(r, lw, k, v, a, b, ck, do, dr_in, dk_in, dv_in)


def _place():
    x, y, c = lax.axis_index("x"), lax.axis_index("y"), lax.axis_index("c")
    return x, y, c, [(1 - x, y), (x, 1 - y), (1 - x, 1 - y)]


_ANY = pl.BlockSpec(memory_space=pl.ANY)


def allgather_shards(sh, *, name):
    r, w = sh.shape
    rh = r // 2

    def body(sh_ref, out_ref, send_sems, recv_sems, local_sem):
        x, y, c, chips = _place()
        sib = (x, y, 1 - c)

        def half(jj, cc):
            return out_ref.at[jj, pl.ds(cc * rh, rh), :]

        def copy(kk, src, dst, to):
            return pltpu.make_async_remote_copy(src_ref=src, dst_ref=dst, send_sem=send_sems.at[kk],
                                                recv_sem=recv_sems.at[kk], device_id=to, device_id_type=MESH)

        mine = pltpu.make_async_copy(sh_ref, out_ref.at[2 * x + y], local_sem)
        mine.start()
        first = [copy(kk, sh_ref.at[pl.ds(c * rh, rh), :], half(2 * x + y, c), (cx, cy, c))
                 for kk, (cx, cy) in enumerate(chips)]
        for cp in first:
            cp.start()
        passed = [copy(3 + kk, half(2 * cx + cy, c), half(2 * cx + cy, c), sib) for kk, (cx, cy) in enumerate(chips)]
        for kk, (cx, cy) in enumerate(chips):
            copy(kk, half(2 * cx + cy, c), half(2 * cx + cy, c), sib).wait_recv()
            passed[kk].start()
        for kk, (cx, cy) in enumerate(chips):
            copy(3 + kk, half(2 * cx + cy, 1 - c), half(2 * cx + cy, 1 - c), sib).wait_recv()
        for cp in first + passed:
            cp.wait_send()
        mine.wait()

    return pl.pallas_call(
        body, name=name, in_specs=[_ANY], out_specs=_ANY,
        out_shape=jax.ShapeDtypeStruct((4, r, w), sh.dtype),
        scratch_shapes=[pltpu.SemaphoreType.DMA((6,)), pltpu.SemaphoreType.DMA((6,)), pltpu.SemaphoreType.DMA],
    )(sh)


def sibling_swap_halves(g, *, name):
    _, r, w = g.shape
    rh = r // 2

    def body(g_ref, out_ref, send_sem, recv_sem):
        x, y, c, _ = _place()
        cp = pltpu.make_async_remote_copy(src_ref=g_ref.at[:, pl.ds((1 - c) * rh, rh), :], dst_ref=out_ref,
                                          send_sem=send_sem, recv_sem=recv_sem, device_id=(x, y, 1 - c),
                                          device_id_type=MESH)
        cp.start()
        cp.wait()

    return pl.pallas_call(
        body, name=name, in_specs=[_ANY], out_specs=_ANY,
        out_shape=jax.ShapeDtypeStruct((4, rh, w), g.dtype),
        scratch_shapes=[pltpu.SemaphoreType.DMA, pltpu.SemaphoreType.DMA],
    )(g)


def pair_sum(g, got, *, tr, name):
    _, rh, w = got.shape
    nb = rh // tr
    c_arr = lax.axis_index("c").astype(jnp.int32).reshape(1)

    def body(c_ref, g_ref, got_ref, o_ref):
        o_ref[...] = (g_ref[...].astype(F32) + got_ref[...].astype(F32)).astype(o_ref.dtype)

    return pl.pallas_call(
        body, name=name,
        grid_spec=pltpu.PrefetchScalarGridSpec(
            num_scalar_prefetch=1, grid=(4, nb),
            in_specs=[pl.BlockSpec((1, tr, w), lambda j, i, c_ref: (j, c_ref[0] * nb + i, 0)),
                      pl.BlockSpec((1, tr, w), lambda j, i, c_ref: (j, i, 0))],
            out_specs=pl.BlockSpec((1, tr, w), lambda j, i, c_ref: (j, i, 0))),
        out_shape=jax.ShapeDtypeStruct((4, rh, w), got.dtype),
        compiler_params=_params(dimension_semantics=("arbitrary", "arbitrary")),
    )(c_arr, g, got)


def scatter_to_chips(ps, *, name):
    _, rh, w = ps.shape

    def body(ps_ref, out_ref, send_sems, recv_sems):
        x, y, c, chips = _place()
        cps = [pltpu.make_async_remote_copy(src_ref=ps_ref.at[2 * cx + cy], dst_ref=out_ref.at[kk],
                                            send_sem=send_sems.at[kk], recv_sem=recv_sems.at[kk],
                                            device_id=(cx, cy, c), device_id_type=MESH)
               for kk, (cx, cy) in enumerate(chips)]
        for cp in cps:
            cp.start()
        for cp in cps:
            cp.wait()

    return pl.pallas_call(
        body, name=name, in_specs=[_ANY], out_specs=_ANY,
        out_shape=jax.ShapeDtypeStruct((3, rh, w), ps.dtype),
        scratch_shapes=[pltpu.SemaphoreType.DMA((3,)), pltpu.SemaphoreType.DMA((3,))],
    )(ps)


def chip_sum(ps, got, *, tr, name):
    _, rh, w = ps.shape
    j_arr = (2 * lax.axis_index("x") + lax.axis_index("y")).astype(jnp.int32).reshape(1)

    def body(j_ref, ps_ref, got_ref, o_ref):
        acc = ps_ref[0].astype(F32)
        for kk in range(3):
            acc = acc + got_ref[kk].astype(F32)
        o_ref[...] = acc

    return pl.pallas_call(
        body, name=name,
        grid_spec=pltpu.PrefetchScalarGridSpec(
            num_scalar_prefetch=1, grid=(rh // tr,),
            in_specs=[pl.BlockSpec((1, tr, w), lambda i, j_ref: (j_ref[0], i, 0)),
                      pl.BlockSpec((3, tr, w), lambda i, j_ref: (0, i, 0))],
            out_specs=pl.BlockSpec((tr, w), lambda i, j_ref: (i, 0))),
        out_shape=jax.ShapeDtypeStruct((rh, w), F32),
        compiler_params=_params(dimension_semantics=("arbitrary",)),
    )(j_arr, ps, got)


def sibling_join_halves(hf, *, name):
    rh, w = hf.shape

    def body(hf_ref, out_ref, send_sem, recv_sem, local_sem):
        x, y, c, _ = _place()
        mine = pltpu.make_async_copy(hf_ref, out_ref.at[pl.ds(c * rh, rh), :], local_sem)
        mine.start()
        cp = pltpu.make_async_remote_copy(src_ref=hf_ref, dst_ref=out_ref.at[pl.ds(c * rh, rh), :],
                                          send_sem=send_sem, recv_sem=recv_sem, device_id=(x, y, 1 - c),
                                          device_id_type=MESH)
        cp.start()
        cp.wait_send()
        pltpu.make_async_remote_copy(src_ref=hf_ref, dst_ref=out_ref.at[pl.ds((1 - c) * rh, rh), :],
                                     send_sem=send_sem, recv_sem=recv_sem, device_id=(x, y, 1 - c),
                                     device_id_type=MESH).wait_recv()
        mine.wait()

    return pl.pallas_call(
        body, name=name, in_specs=[_ANY], out_specs=_ANY,
        out_shape=jax.ShapeDtypeStruct((2 * rh, w), hf.dtype),
        scratch_shapes=[pltpu.SemaphoreType.DMA, pltpu.SemaphoreType.DMA, pltpu.SemaphoreType.DMA],
    )(hf)


def allreduce_small(part, *, name):
    m_per, n = part.shape

    def body(x_ref, sum_ref, all_ref, send_sems, recv_sems, local_sem):
        x, y, c, chips = _place()
        me, sib = (x, y, c), (x, y, 1 - c)

        def rows(px, py, pc):
            return all_ref.at[pl.ds((4 * px + 2 * py + pc) * m_per, m_per), :]

        def copy(kk, block, to, src=None):
            return pltpu.make_async_remote_copy(src_ref=rows(*block) if src is None else src, dst_ref=rows(*block),
                                                send_sem=send_sems.at[kk], recv_sem=recv_sems.at[kk],
                                                device_id=to, device_id_type=MESH)

        mine = pltpu.make_async_copy(x_ref, rows(*me), local_sem)
        mine.start()
        first = [copy(0, me, sib, src=x_ref)]
        first += [copy(1 + kk, me, (*chip, c), src=x_ref) for kk, chip in enumerate(chips)]
        for cp in first:
            cp.start()
        passed = [copy(4 + kk, (*chip, c), sib) for kk, chip in enumerate(chips)]
        for kk, chip in enumerate(chips):
            copy(1 + kk, (*chip, c), me).wait_recv()
            passed[kk].start()
        copy(0, sib, me).wait_recv()
        for kk, chip in enumerate(chips):
            copy(4 + kk, (*chip, 1 - c), me).wait_recv()
        for cp in first + passed:
            cp.wait_send()
        mine.wait()
        acc = all_ref[0:m_per, :]
        for d in range(1, 8):
            acc = acc + all_ref[d * m_per:(d + 1) * m_per, :]
        sum_ref[...] = acc

    vmem = pl.BlockSpec(memory_space=pltpu.VMEM)
    return pl.pallas_call(
        body, name=name, in_specs=[vmem], out_specs=vmem,
        out_shape=jax.ShapeDtypeStruct((m_per, n), part.dtype),
        scratch_shapes=[pltpu.VMEM((8 * m_per, n), part.dtype), pltpu.SemaphoreType.DMA((7,)),
                        pltpu.SemaphoreType.DMA((7,)), pltpu.SemaphoreType.DMA],
    )(part)


def adamw(w, g, m, v, *, name):
    r, c = w.shape
    tm = r if r * c * 4 <= (1 << 20) else _tile(r, max(8, ((1 << 20) // (c * 4)) // 8 * 8), unit=8)
    bc1, bc2 = 1.0 - ADAM_B1 ** ADAM_STEP, 1.0 - ADAM_B2 ** ADAM_STEP

    def fn(i, n, rv, cv, sc):
        w_, g_, m_, v_ = rv
        m_ = ADAM_B1 * m_ + (1.0 - ADAM_B1) * g_
        v_ = ADAM_B2 * v_ + (1.0 - ADAM_B2) * (g_ * g_)
        delta = -ADAM_LR * ((m_ / bc1) / (jnp.sqrt(v_ / bc2) + ADAM_EPS) + ADAM_WD * w_)
        return [delta, m_, v_], []

    return rowcall(fn, [w, g, m, v], [], [(c, F32)] * 3, [], tm=tm, name=name)[0]


def _head_one_hot(c):
    e = (lax.broadcasted_iota(jnp.int32, (c, LANES), 0) // HEAD
         == lax.broadcasted_iota(jnp.int32, (c, LANES), 1)).astype(F32)
    return e, e.T


def layer_step(x, mem, tgt, wt, sp):
    s, d = x.shape
    dr, dc = sp["rwkv_w0"].shape[1], sp["conv_b"].shape[1]
    n_lora = N_DECAY + N_ICLR + N_GATE
    n_rwkv = 3 * dr + n_lora
    pad_l = N_LORA_PAD - n_lora
    w_in = wt["w_in"]
    cuts = [0, dr, 2 * dr, 3 * dr, n_rwkv, n_rwkv + dc, n_rwkv + 2 * dc, n_rwkv + 2 * dc + d, n_rwkv + 2 * dc + 2 * d]
    w_r, w_k, w_v, w_l, w_ca, w_cb, w_gr, w_gc = (w_in[:, lo:hi] for lo, hi in zip(cuts[:-1], cuts[1:]))
    w_l = jnp.pad(w_l, ((0, 0), (0, pad_l)))
    sm = sp["rwkv_shift_mix"]
    mus = [sm[:, 0:dr], sm[:, dr:2 * dr], sm[:, 2 * dr:3 * dr], jnp.pad(sm[:, 3 * dr:], ((0, 0), (0, pad_l)))]
    w_up = jnp.pad(wt["rwkv_w_up"].astype(F32), ((0, LANES - N_DECAY), (0, 0)))
    a_up = jnp.pad(wt["rwkv_a_up"].astype(F32), ((N_DECAY, 2 * LANES - N_DECAY - N_ICLR), (0, 0)))
    g_lo = N_DECAY + N_ICLR - LANES
    g_up = jnp.pad(wt["rwkv_g_up"].astype(F32), ((g_lo, pad_l), (0, 0)))
    conv_w = jnp.pad(wt["conv_w"], ((0, CONV_HALO - CONV_WIDTH), (0, 0)))
    e, et = _head_one_hot(dr)
    pre_p = [sp["rwkv_w0"], w_up, sp["rwkv_a0"], a_up, g_up, sp["rwkv_k_k"], sp["rwkv_k_a"]]
    post_p = [sp["rwkv_r_k"], sp["rwkv_gn_g"], sp["rwkv_gn_b"]]
    pairs = dr // PAIR
    tm = min(s, 128)
    tmm = mem.shape[0]

    z_r, z_k, z_v, z_l = (mm(x, w, name=f"z_{n}") for n, w in zip("rkvl", (w_r, w_k, w_v, w_l)))
    z_ca, z_cb = mm(x, w_ca, name="z_ca"), mm(x, w_cb, name="z_cb")
    z_gr, z_gc = mm(x, w_gr, name="z_gr"), mm(x, w_gc, name="z_gc")
    zs_r, zs_k, zs_v, zs_l = tokenshift_fwd([z_r, z_k, z_v, z_l], mus, tm=tm, name="shift_fwd")
    pre_o = [(dr, F32)] * 5
    k_m, lw, a_s, b_s, g = stage_fwd(f_rwkv_pre, [zs_k, zs_l], pre_p, [e, et], pre_o, tm=min(s, 64), name="pre_fwd")
    o, ck = scan_fwd(zs_r, lw, k_m, zs_v, a_s, b_s, pg=min(8, pairs), name="scan_fwd")
    post_r = [o, zs_r, k_m, zs_v, g]
    (o_r,) = stage_fwd(f_rwkv_post, post_r, post_p, [e, et], [(dr, BF16)], tm=min(s, 64), name="post_fwd")
    (u,) = stage_fwd(f_glu, [z_ca, z_cb], [], [], [(dc, F32)], tm=tm, name="glu_fwd")
    cv = conv_fwd(u, conv_w, sp["conv_b"], tm=tm, name="conv_fwd")
    cln_p = [sp["conv_ln_g"], sp["conv_ln_b"]]
    (o_c,) = stage_fwd(f_convln, [cv], cln_p, [], [(dc, BF16)], tm=tm, name="convln_fwd")
    p_r, p_c = mm(o_r, wt["proj_rwkv"], name="proj_r"), mm(o_c, wt["proj_conv"], name="proj_c")
    (merged,) = stage_fwd(f_merge, [z_gr, z_gc, p_r, p_c], [], [], [(d, BF16)], tm=tm, name="merge_fwd")
    y1 = mm(merged, wt["w_out"], name="y1")
    ln1_p, ln2_p, lnm_p = ([sp[f"{n}_g"], sp[f"{n}_b"]] for n in ("ln1", "ln2", "ln_mem"))
    (h1,) = stage_fwd(f_resln, [x, y1], ln1_p, [], [(d, F32)], tm=tm, name="ln1_fwd")
    (mem_n,) = stage_fwd(f_ln, [mem], lnm_p, [], [(d, F32)], tm=tmm, name="lnmem_fwd")
    k_mem, v_mem = mm(mem_n, wt["xattn_wk"], name="k_mem"), mm(mem_n, wt["xattn_wv"], name="v_mem")
    q = mm(h1, wt["xattn_wq"], name="q")
    (ao,) = stage_fwd(f_attn, [q], [k_mem, v_mem], [], [(d, BF16)], tm=tm, name="attn_fwd")
    ca = mm(ao, wt["xattn_wo"], name="ca")
    (h2,) = stage_fwd(f_resln, [h1, ca], ln2_p, [], [(d, F32)], tm=tm, name="ln2_fwd")
    u1 = mm(h2, wt["mlp_w1"], name="u1")
    f_dim = u1.shape[1]
    tmf = min(s, 64)
    (act,) = stage_fwd(f_relu2, [u1], [], [], [(f_dim, BF16)], tm=tmf, name="relu2_fwd")
    ff = mm(act, wt["mlp_w2"], name="ff")

    gw, gs = {}, {}
    loss, dh2, dff, gs["ln3_g"], gs["ln3_b"] = loss_bwd(h2, ff, tgt, sp["ln3_g"], sp["ln3_b"], tm=tm, name="loss_bwd")
    gw["mlp_w2"] = mm(act, dff, ta=True, out_dtype=BF16, name="g_mlp_w2")
    dact = mm(dff, wt["mlp_w2"], tb=True, name="d_act")
    (du1,), _ = stage_bwd(f_relu2, [u1], [], [], [dact], [(0, BF16)], tm=tmf, name="relu2_bwd")
    gw["mlp_w1"] = mm(h2, du1, ta=True, out_dtype=BF16, name="g_mlp_w1")
    dh2 = mm(du1, wt["mlp_w1"], tb=True, acc=dh2, name="d_h2")
    (dh1, dca), (gs["ln2_g"], gs["ln2_b"]) = stage_bwd(f_resln, [h1, ca], ln2_p, [], [dh2], [(0, F32), (1, F32)],
                                                       tm=tm, name="ln2_bwd")
    gw["xattn_wo"] = mm(ao, dca, ta=True, out_dtype=BF16, name="g_wo")
    dao = mm(dca, wt["xattn_wo"], tb=True, name="d_ao")
    (dq,), (dk_mem, dv_mem) = stage_bwd(f_attn, [q], [k_mem, v_mem], [], [dao], [(0, F32)], tm=tm, name="attn_bwd")
    gw["xattn_wq"] = mm(h1, dq, ta=True, out_dtype=BF16, name="g_wq")
    dh1 = mm(dq, wt["xattn_wq"], tb=True, acc=dh1, name="d_h1")
    gw["xattn_wk"] = mm(mem_n, dk_mem, ta=True, out_dtype=BF16, name="g_wk")
    gw["xattn_wv"] = mm(mem_n, dv_mem, ta=True, out_dtype=BF16, name="g_wv")
    dmem_n = mm(dk_mem, wt["xattn_wk"], tb=True, name="d_memn_k")
    dmem_n = mm(dv_mem, wt["xattn_wv"], tb=True, acc=dmem_n, name="d_memn_v")
    _, (gs["ln_mem_g"], gs["ln_mem_b"]) = stage_bwd(f_ln, [mem], lnm_p, [], [dmem_n], [], tm=tmm, name="lnmem_bwd")
    (dx, dy1), (gs["ln1_g"], gs["ln1_b"]) = stage_bwd(f_resln, [x, y1], ln1_p, [], [dh1], [(0, F32), (1, F32)],
                                                      tm=tm, name="ln1_bwd")
    gw["w_out"] = mm(merged, dy1, ta=True, out_dtype=BF16, name="g_w_out")
    dmerged = mm(dy1, wt["w_out"], tb=True, name="d_merged")
    (dz_gr, dz_gc, dp_r, dp_c), _ = stage_bwd(f_merge, [z_gr, z_gc, p_r, p_c], [], [], [dmerged],
                                              [(0, BF16), (1, BF16), (2, F32), (3, F32)], tm=tm, name="merge_bwd")
    gw["proj_rwkv"] = mm(o_r, dp_r, ta=True, out_dtype=BF16, name="g_proj_r")
    gw["proj_conv"] = mm(o_c, dp_c, ta=True, out_dtype=BF16, name="g_proj_c")
    do_r = mm(dp_r, wt["proj_rwkv"], tb=True, name="d_o_r")
    do_c = mm(dp_c, wt["proj_conv"], tb=True, name="d_o_c")
    (dcv,), (gs["conv_ln_g"], gs["conv_ln_b"]) = stage_bwd(f_convln, [cv], cln_p, [], [do_c], [(0, F32)],
                                                           tm=tm, name="convln_bwd")
    du, g_conv_w, gs["conv_b"] = conv_bwd(dcv, u, conv_w, tm=tm, name="conv_bwd")
    gw["conv_w"] = g_conv_w[0:CONV_WIDTH]
    (dz_ca, dz_cb), _ = stage_bwd(f_glu, [z_ca, z_cb], [], [], [du], [(0, BF16), (1, BF16)], tm=tm, name="glu_bwd")
    (d_o, dr_p, dk_p, dv_p, dg), (gs["rwkv_r_k"], gs["rwkv_gn_g"], gs["rwkv_gn_b"]) = stage_bwd(
        f_rwkv_post, post_r, post_p, [e, et], [do_r], [(k, F32) for k in range(5)], tm=min(s, 64), name="post_bwd")
    dzs_r, dlw, dk_m, dzs_v, da_s, db_s = scan_bwd(zs_r, lw, k_m, zs_v, a_s, b_s, ck, d_o, dr_p, dk_p, dv_p,
                                                   pg=min(4, pairs), name="scan_bwd")
    (dzs_k, dzs_l), pre_g = stage_bwd(f_rwkv_pre, [zs_k, zs_l], pre_p, [e, et], [dk_m, dlw, da_s, db_s, dg],
                                      [(0, F32), (1, F32)], tm=min(s, 64), name="pre_bwd")
    gs["rwkv_w0"], g_w_up, gs["rwkv_a0"], g_a_up, g_g_up, gs["rwkv_k_k"], gs["rwkv_k_a"] = pre_g
    gw["rwkv_w_up"] = g_w_up[0:N_DECAY]
    gw["rwkv_a_up"] = g_a_up[N_DECAY:N_DECAY + N_ICLR]
    gw["rwkv_g_up"] = g_g_up[g_lo:g_lo + N_GATE]
    dzs, dmus = tokenshift_bwd([dzs_r, dzs_k, dzs_v, dzs_l], [z_r, z_k, z_v, z_l], mus, tm=tm, name="shift_bwd")
    gs["rwkv_shift_mix"] = jnp.concatenate(list(dmus[:3]) + [dmus[3][:, 0:n_lora]], axis=1)
    dzs = list(dzs) + [dz_ca, dz_cb, dz_gr, dz_gc]
    g_in = []
    for n, dz, w in zip(("r", "k", "v", "l", "ca", "cb", "gr", "gc"), dzs, (w_r, w_k, w_v, w_l, w_ca, w_cb, w_gr, w_gc)):
        g_in.append(mm(x, dz, ta=True, out_dtype=BF16, name=f"g_w_{n}"))
        dx = mm(dz, w, tb=True, acc=dx, name=f"d_x_{n}")
    g_in[3] = g_in[3][:, 0:n_lora]
    gw["w_in"] = jnp.concatenate(g_in, axis=1)
    return loss, dx, gw, gs


WEIGHTS = ["w_in", "rwkv_shift_mix", "rwkv_w0", "rwkv_w_up", "rwkv_a0", "rwkv_a_up", "rwkv_g_up", "rwkv_k_k",
           "rwkv_k_a", "rwkv_r_k", "rwkv_gn_g", "rwkv_gn_b", "conv_w", "conv_b", "conv_ln_g", "conv_ln_b",
           "proj_rwkv", "proj_conv", "w_out", "ln1_g", "ln1_b", "ln_mem_g", "ln_mem_b", "xattn_wq", "xattn_wk",
           "xattn_wv", "xattn_wo", "ln2_g", "ln2_b", "mlp_w1", "mlp_w2", "ln3_g", "ln3_b"]
SHARD_AXIS = {"w_in": 1, "rwkv_w_up": 1, "rwkv_a_up": 1, "rwkv_g_up": 1, "conv_w": 1, "proj_rwkv": 1, "proj_conv": 1,
              "w_out": 0, "xattn_wq": 0, "xattn_wk": 0, "xattn_wv": 0, "xattn_wo": 0, "mlp_w1": 1, "mlp_w2": 0}
PACK_W = 1024
PACK_ROWS = 1024
PACK_TR = 512


def _pack(parts, dtype):
    flat = jnp.concatenate([p.reshape(-1).astype(dtype) for p in parts])
    unit = PACK_W * PACK_ROWS
    return jnp.pad(flat, (0, -flat.shape[0] % unit)).reshape(-1, PACK_W)


def _unpack(flat, shapes):
    out, off = [], 0
    for shp in shapes:
        n = 1
        for dim in shp:
            n *= dim
        out.append(flat[..., off:off + n].reshape(flat.shape[:-1] + tuple(shp)))
        off += n
    return out


def _f32_as_bf16_pairs(a):
    return lax.bitcast_convert_type(a, BF16)


def _bf16_pairs_as_f32(a):
    return lax.bitcast_convert_type(a, F32)


def kernel(*args):
    n_w = len(WEIGHTS)
    x, mem = args[0][0], args[1][0]
    tgt = args[2 + n_w][0]
    w_loc = {n: a for n, a in zip(WEIGHTS, args[2:2 + n_w])}
    m_loc = {n: a for n, a in zip(WEIGHTS, args[3 + n_w:3 + 2 * n_w])}
    v_loc = {n: a for n, a in zip(WEIGHTS, args[3 + 2 * n_w:3 + 3 * n_w])}
    big = [n for n in WEIGHTS if n in SHARD_AXIS]
    small = [n for n in WEIGHTS if n not in SHARD_AXIS]

    def as2d(n, a):
        if n in SHARD_AXIS:
            return a.reshape(a.shape[1], a.shape[-1])
        return a.reshape(1, -1)

    loc2d = {n: as2d(n, w_loc[n]) for n in WEIGHTS}
    send = [_f32_as_bf16_pairs(loc2d[n]) if n == "conv_w" else loc2d[n] for n in big]
    gathered = allgather_shards(_pack(send, BF16), name="allgather_weights")
    shards = _unpack(gathered.reshape(4, -1), [p.shape for p in send])
    wt = {}
    for n, sh in zip(big, shards):
        if n == "conv_w":
            sh = _bf16_pairs_as_f32(sh)
        wt[n] = jnp.concatenate([sh[j] for j in range(4)], axis=SHARD_AXIS[n])
    sp = {n: loc2d[n] for n in small}

    loss_part, grad_x, gw, gs = layer_step(x, mem, tgt, wt, sp)

    pieces = [[p for p in jnp.split(gw[n], 4, axis=SHARD_AXIS[n])] for n in big]
    g_all = jnp.stack([_pack([pc[j] for pc in pieces], BF16) for j in range(4)])
    got = sibling_swap_halves(g_all, name="rs_sibling_swap")
    ps = pair_sum(g_all, got, tr=PACK_TR, name="rs_pair_sum")
    got = scatter_to_chips(ps, name="rs_scatter")
    half = chip_sum(ps, got, tr=PACK_TR, name="rs_chip_sum")
    g_red = sibling_join_halves(half, name="rs_sibling_join")
    g_big = dict(zip(big, _unpack(g_red.reshape(-1), [loc2d[n].shape for n in big])))

    small_parts = [gs[n] for n in small] + [loss_part[0:1, 0:1]]
    flat = jnp.concatenate([p.reshape(-1).astype(F32) for p in small_parts])
    flat = jnp.pad(flat, (0, -flat.shape[0] % (8 * LANES))).reshape(-1, LANES)
    red = allreduce_small(flat, name="allreduce_small").reshape(-1)
    g_small = dict(zip(small, _unpack(red, [loc2d[n].shape for n in small])))
    n_small = sum(loc2d[n].shape[1] for n in small)
    loss = red[n_small]

    grads, deltas, new_m, new_v = {}, {}, {}, {}
    for n in big:
        g2 = g_big[n]
        d2, m2, v2 = adamw(loc2d[n], g2, as2d(n, m_loc[n]), as2d(n, v_loc[n]), name=f"adamw_{n}")
        shp = w_loc[n].shape
        grads[n], deltas[n], new_m[n], new_v[n] = (t.reshape(shp) for t in (g2, d2, m2, v2))

    def small_pack(d):
        f = jnp.concatenate([as2d(n, d[n]).reshape(-1) for n in small])
        return jnp.pad(f, (0, -f.shape[0] % (8 * LANES))).reshape(-1, LANES)

    g_pack = small_pack({n: g_small[n] for n in small})
    outs = adamw(small_pack(w_loc), g_pack, small_pack(m_loc), small_pack(v_loc), name="adamw_small")
    for dst, packed in zip((deltas, new_m, new_v), outs):
        for n, t in zip(small, _unpack(packed.reshape(-1), [loc2d[n].shape for n in small])):
            dst[n] = t.reshape(w_loc[n].shape)
    for n in small:
        grads[n] = g_small[n].reshape(w_loc[n].shape)

    return (loss, grad_x[None], *[grads[n] for n in WEIGHTS], *[deltas[n] for n in WEIGHTS],
            *[new_m[n] for n in WEIGHTS], *[new_v[n] for n in WEIGHTS])
```

```python
import functools

import jax
import jax.numpy as jnp
from jax import lax
from jax.experimental import pallas as pl
from jax.experimental.pallas import tpu as pltpu

F32, BF16 = jnp.float32, jnp.bfloat16
ALPHA = 2.0 ** 0.25
LN_EPS = 1e-5
GN_EPS = 64e-5
HEAD = 64
LANES = 128
PAIR = 2 * HEAD
XATTN_HEADS = 4
CONV_WIDTH = 31
CONV_HALO = 32
N_DECAY, N_ICLR, N_GATE = 96, 96, 256
N_LORA_PAD = 512
VMEM_LIMIT = 56 * 1024 * 1024
ADAM_LR, ADAM_B1, ADAM_B2, ADAM_EPS, ADAM_WD, ADAM_STEP = 0.001, 0.9, 0.999, 1e-8, 0.01, 10
MESH = pl.DeviceIdType.MESH
HI = lax.Precision.HIGHEST


def _params(**kw):
    return pltpu.CompilerParams(vmem_limit_bytes=VMEM_LIMIT, **kw)


def _tile(n, pref, unit=LANES):
    if n <= pref:
        return n
    t = pref
    while n % t:
        t -= unit
    return t


def mm(a, b, *, name, ta=False, tb=False, out_dtype=F32, acc=None, b_sh=False, out_sh=False):
    m, k = (a.shape[1], a.shape[0]) if ta else a.shape
    if b_sh:
        n = b.shape[1] if tb else 4 * b.shape[2]
    else:
        n = b.shape[0] if tb else b.shape[1]
    tm, tn, tk = _tile(m, 1024), _tile(n // 4 if (b_sh and not tb) or out_sh else n, 1024), _tile(k // 4 if b_sh and tb else k, 512)
    nk = k // tk
    a_spec = pl.BlockSpec((tk, tm), lambda i, j, kk: (kk, i)) if ta else pl.BlockSpec((tm, tk), lambda i, j, kk: (i, kk))
    if b_sh and tb:
        per_k = k // 4 // tk
        b_spec = pl.BlockSpec((None, tn, tk), lambda i, j, kk: (kk // per_k, j, kk % per_k))
    elif b_sh:
        per_n = n // 4 // tn
        b_spec = pl.BlockSpec((None, tk, tn), lambda i, j, kk: (j // per_n, kk, j % per_n))
    else:
        b_spec = pl.BlockSpec((tn, tk), lambda i, j, kk: (j, kk)) if tb else pl.BlockSpec((tk, tn), lambda i, j, kk: (kk, j))
    if out_sh:
        per_o = n // 4 // tn
        o_spec = pl.BlockSpec((None, tm, tn), lambda i, j, kk: (j // per_o, i, j % per_o))
    else:
        o_spec = pl.BlockSpec((tm, tn), lambda i, j, kk: (i, j))
    dims = (((0 if ta else 1,), (1 if tb else 0,)), ((), ()))

    def body(*refs):
        if acc is None:
            a_ref, b_ref, o_ref, acc_ref = refs
        else:
            a_ref, b_ref, c_ref, o_ref, acc_ref = refs
        kk = pl.program_id(2)

        @pl.when(kk == 0)
        def _():
            acc_ref[...] = jnp.zeros_like(acc_ref) if acc is None else c_ref[...].astype(F32)

        acc_ref[...] += lax.dot_general(a_ref[...].astype(BF16), b_ref[...].astype(BF16), dims,
                                        preferred_element_type=F32)

        @pl.when(kk == nk - 1)
        def _():
            o_ref[...] = acc_ref[...].astype(o_ref.dtype)

    ins = [a, b] + ([] if acc is None else [acc])
    in_specs = [a_spec, b_spec] + ([] if acc is None else [o_spec])
    return pl.pallas_call(
        body, name=name, grid=(m // tm, n // tn, nk),
        in_specs=in_specs, out_specs=o_spec,
        out_shape=jax.ShapeDtypeStruct((4, m, n // 4) if out_sh else (m, n), out_dtype),
        scratch_shapes=[pltpu.VMEM((tm, tn), F32)],
        input_output_aliases={} if acc is None else {2: 0},
        compiler_params=_params(dimension_semantics=("arbitrary", "arbitrary", "arbitrary")),
    )(*ins)


def rowcall(fn, rows, consts, out_rows, out_accs, *, tm, name, scratch=()):
    rows = [r if isinstance(r, tuple) else (r, None) for r in rows]
    s = rows[0][0].shape[0]
    n = s // tm
    in_specs, ins = [], []
    for arr, halo in rows:
        w = arr.shape[1]
        in_specs.append(pl.BlockSpec((tm, w), lambda i: (i, 0)))
        ins.append(arr)
        if halo is not None:
            kind, h = halo
            per = tm // h
            if kind == "prev":
                in_specs.append(pl.BlockSpec((h, w), lambda i, per=per: (jnp.maximum(i * per - 1, 0), 0)))
            else:
                in_specs.append(pl.BlockSpec((h, w), lambda i, per=per, last=s // h - 1: (jnp.minimum((i + 1) * per, last), 0)))
            ins.append(arr)
    for cst in consts:
        in_specs.append(pl.BlockSpec(cst.shape, lambda i, nd=cst.ndim: (0,) * nd))
        ins.append(cst)
    out_specs = [pl.BlockSpec((tm, w), lambda i: (i, 0)) for w, _ in out_rows]
    out_specs += [pl.BlockSpec(shp, lambda i, nd=len(shp): (0,) * nd) for shp, _ in out_accs]
    out_shape = [jax.ShapeDtypeStruct((s, w), dt) for w, dt in out_rows]
    out_shape += [jax.ShapeDtypeStruct(shp, dt) for shp, dt in out_accs]
    n_in, n_or, n_oa = len(ins), len(out_rows), len(out_accs)

    def body(*refs):
        i = pl.program_id(0)
        it = iter(refs[:n_in])
        row_vals = []
        for _, halo in rows:
            cur = next(it)[...]
            row_vals.append(cur if halo is None else (cur, next(it)[...]))
        const_vals = [r[...] for r in it]
        o_refs = refs[n_in:n_in + n_or]
        a_refs = refs[n_in + n_or:n_in + n_or + n_oa]
        outs, parts = fn(i, n, row_vals, const_vals, refs[n_in + n_or + n_oa:])
        for o_ref, val in zip(o_refs, outs, strict=True):
            o_ref[...] = val.astype(o_ref.dtype)
        for a_ref, part in zip(a_refs, parts, strict=True):
            part = jnp.broadcast_to(part, a_ref.shape).astype(a_ref.dtype)

            @pl.when(i == 0)
            def _(a_ref=a_ref, part=part):
                a_ref[...] = part

            @pl.when(i > 0)
            def _(a_ref=a_ref, part=part):
                a_ref[...] += part

    res = pl.pallas_call(
        body, name=name, grid=(n,), in_specs=in_specs, out_specs=out_specs, out_shape=out_shape,
        scratch_shapes=list(scratch),
        compiler_params=_params(dimension_semantics=("arbitrary",)),
    )(*ins)
    return res[:n_or], res[n_or:]


def stage_fwd(f, rows, params, consts, outs, *, tm, name):
    n_p = len(params)

    def fn(i, n, rv, cv, sc):
        return f([r.astype(F32) for r in rv], cv[:n_p], cv[n_p:]), []

    return rowcall(fn, rows, list(params) + list(consts), outs, [], tm=tm, name=name)[0]


def stage_bwd(f, rows, params, consts, cts, row_grads, *, tm, name):
    n_r, n_p = len(rows), len(params)

    def fn(i, n, rv, cv, sc):
        r = [x.astype(F32) for x in rv[:n_r]]
        ct = [x.astype(F32) for x in rv[n_r:]]
        _, vjp = jax.vjp(lambda r_, p_: f(r_, p_, cv[n_p:]), r, list(cv[:n_p]))
        d_r, d_p = vjp(ct)
        return [d_r[k] for k, _ in row_grads], d_p

    return rowcall(fn, list(rows) + list(cts), list(params) + list(consts),
                   [(rows[k].shape[1], dt) for k, dt in row_grads],
                   [(p.shape, F32) for p in params], tm=tm, name=name)


def _ln(x, g, b, eps=LN_EPS):
    xc = x - jnp.mean(x, -1, keepdims=True)
    var = jnp.mean(xc * xc, -1, keepdims=True)
    return xc * lax.rsqrt(var + eps) * g + b


def _sigmoid(x):
    return 1.0 / (1.0 + jnp.exp(-x))


def _softplus(x):
    return jnp.maximum(x, 0.0) + jnp.log(1.0 + jnp.exp(-jnp.abs(x)))


def _bdot(a, b):
    return jnp.dot(a.astype(BF16), b.astype(BF16), preferred_element_type=F32)


def _head_sum(x, e, et):
    return jnp.dot(jnp.dot(x, e, precision=HI, preferred_element_type=F32), et, precision=HI,
                   preferred_element_type=F32)


def f_rwkv_pre(rows, params, consts):
    zk, zl = rows
    w0, w_up, a0, a_up, g_up, k_k, k_a = params
    e, et = consts
    w = -_softplus(-(w0 + _bdot(jnp.tanh(zl[:, 0:LANES]), w_up))) - 0.5
    log_decay = -jnp.exp(w)
    a = _sigmoid(a0 + _bdot(zl[:, 0:2 * LANES], a_up))
    g = _bdot(_sigmoid(zl[:, LANES:N_LORA_PAD]), g_up)
    kk = zk * k_k
    kk = kk / jnp.maximum(jnp.sqrt(_head_sum(kk * kk, e, et)), 1e-12)
    return [zk * (1.0 + (a - 1.0) * k_a), log_decay, -kk, kk * a, g]


def f_rwkv_post(rows, params, consts):
    o, r, k, v, g = rows
    r_k, gn_g, gn_b = params
    e, et = consts
    oc = o - _head_sum(o, e, et) * (1.0 / HEAD)
    var = _head_sum(oc * oc, e, et) * (1.0 / HEAD)
    on = oc * lax.rsqrt(var + GN_EPS) * gn_g + gn_b
    return [(on + _head_sum(r * k * r_k, e, et) * v) * g]


def f_glu(rows, params, consts):
    return [rows[0] * _sigmoid(rows[1])]


def f_convln(rows, params, consts):
    x = _ln(rows[0], params[0], params[1])
    return [x * _sigmoid(x)]


def f_merge(rows, params, consts):
    gr, gc, pr, pc = rows
    return [_sigmoid(gr) * pr + _sigmoid(gc) * pc]


def f_resln(rows, params, consts):
    return [_ln(ALPHA * rows[0] + rows[1], params[0], params[1])]


def f_ln(rows, params, consts):
    return [_ln(rows[0], params[0], params[1])]


def f_attn(rows, params, consts):
    q, (k, v) = rows[0], params
    dh = q.shape[1] // XATTN_HEADS
    outs = []
    for h in range(XATTN_HEADS):
        sl = slice(h * dh, (h + 1) * dh)
        s = lax.dot_general(q[:, sl].astype(BF16), k[:, sl].astype(BF16), (((1,), (1,)), ((), ())),
                            preferred_element_type=F32) * dh ** -0.5
        p = jnp.exp(s - jnp.max(s, -1, keepdims=True))
        p = p / jnp.sum(p, -1, keepdims=True)
        outs.append(_bdot(p, v[:, sl]))
    return [jnp.concatenate(outs, axis=-1)]


def f_relu2(rows, params, consts):
    return [jnp.square(jnp.maximum(rows[0], 0.0))]


def loss_bwd(h, y, tgt, g, b, *, tm, name):
    def fn(i, n, rv, cv, sc):
        def loss(h_, y_, g_, b_):
            err = _ln(ALPHA * h_ + y_, g_, b_) - rv[2]
            return 0.5 * jnp.sum(jnp.mean(err * err, -1, keepdims=True))
        val, vjp = jax.vjp(loss, rv[0], rv[1], cv[0], cv[1])
        dh, dy, dg, db = vjp(jnp.ones((), F32))
        return [dh, dy], [val.reshape(1, 1), dg, db]

    w = h.shape[1]
    (dh, dy), (val, dg, db) = rowcall(fn, [h, y, tgt], [g, b], [(w, F32), (w, F32)],
                                      [((8, LANES), F32), (g.shape, F32), (b.shape, F32)], tm=tm, name=name)
    return val, dh, dy, dg, db


def _shift_down(cur, halo, first):
    rolled = pltpu.roll(cur, 1, 0)
    row0 = jnp.where(first, 0.0, halo[halo.shape[0] - 1:, :])
    return jnp.where(lax.broadcasted_iota(jnp.int32, cur.shape, 0) == 0, row0, rolled)


def _shift_up(cur, halo, last):
    rolled = pltpu.roll(cur, cur.shape[0] - 1, 0)
    rown = jnp.where(last, 0.0, halo[0:1, :])
    return jnp.where(lax.broadcasted_iota(jnp.int32, cur.shape, 0) == cur.shape[0] - 1, rown, rolled)


def tokenshift_fwd(zs, mus, *, tm, name):
    def fn(i, n, rv, cv, sc):
        return [z + (_shift_down(z, halo, i == 0) - z) * mu for (z, halo), mu in zip(rv, cv)], []

    return rowcall(fn, [(z, ("prev", 8)) for z in zs], mus, [(z.shape[1], F32) for z in zs], [], tm=tm, name=name)[0]


def tokenshift_bwd(dzs, zs, mus, *, tm, name):
    nz = len(zs)

    def fn(i, n, rv, cv, sc):
        outs, parts = [], []
        for (dz, dnext), (z, zprev), mu in zip(rv[:nz], rv[nz:], cv):
            g = dz * mu
            outs.append(dz - g + _shift_up(g, dnext * mu, i == n - 1))
            parts.append(jnp.sum(dz * (_shift_down(z, zprev, i == 0) - z), 0, keepdims=True))
        return outs, parts

    return rowcall(fn, [(d, ("next", 8)) for d in dzs] + [(z, ("prev", 8)) for z in zs], mus,
                   [(z.shape[1], F32) for z in zs], [(mu.shape, F32) for mu in mus], tm=tm, name=name)


def conv_fwd(u, w, b, *, tm, name):
    c = u.shape[1]

    def fn(i, n, rv, cv, sc):
        (cur, halo), (ext,) = rv[0], sc
        ext[0:CONV_HALO, :] = jnp.where(i == 0, 0.0, halo)
        ext[CONV_HALO:, :] = cur
        wv = cv[0]
        acc = jnp.broadcast_to(cv[1], cur.shape)
        for j in range(CONV_WIDTH):
            acc = acc + wv[j:j + 1, :] * ext[pl.ds(CONV_HALO - CONV_WIDTH + 1 + j, tm), :]
        return [acc], []

    return rowcall(fn, [(u, ("prev", CONV_HALO))], [w, b], [(c, F32)], [], tm=tm, name=name,
                   scratch=[pltpu.VMEM((tm + CONV_HALO, c), F32)])[0][0]


def conv_bwd(dc, u, w, *, tm, name):
    c = u.shape[1]

    def fn(i, n, rv, cv, sc):
        (dcur, dnext), (ucur, uprev) = rv
        dext, uext, dw_ref = sc
        dext[0:tm, :] = dcur
        dext[tm:, :] = jnp.where(i == n - 1, 0.0, dnext)
        uext[0:CONV_HALO, :] = jnp.where(i == 0, 0.0, uprev)
        uext[CONV_HALO:, :] = ucur
        wv = cv[0]
        du = jnp.zeros_like(dcur)
        dw_ref[...] = jnp.zeros_like(dw_ref)
        for j in range(CONV_WIDTH):
            du = du + wv[j:j + 1, :] * dext[pl.ds(CONV_WIDTH - 1 - j, tm), :]
            dw_ref[j:j + 1, :] = jnp.sum(dcur * uext[pl.ds(CONV_HALO - CONV_WIDTH + 1 + j, tm), :], 0, keepdims=True)
        return [du], [dw_ref[...], jnp.sum(dcur, 0, keepdims=True)]

    (du,), (dw, db) = rowcall(
        fn, [(dc, ("next", CONV_HALO)), (u, ("prev", CONV_HALO))], [w], [(c, F32)],
        [((CONV_HALO, c), F32), ((1, c), F32)], tm=tm, name=name,
        scratch=[pltpu.VMEM((tm + CONV_HALO, c), F32), pltpu.VMEM((tm + CONV_HALO, c), F32),
                 pltpu.VMEM((CONV_HALO, c), F32)])
    return du, dw, db


SCAN_TB = 128


def _seg_cols(x, left):
    return (jnp.sum(jnp.where(left, x, 0.0), 1, keepdims=True), jnp.sum(jnp.where(left, 0.0, x), 1, keepdims=True))


def _seg_bcast(x, left):
    c0, c1 = _seg_cols(x, left)
    return jnp.where(left, c0, c1)


def _col_of(xt, onehot, left):
    col = jnp.sum(jnp.where(onehot, xt, 0.0), 1, keepdims=True)
    return jnp.where(left, col[0:HEAD], col[HEAD:PAIR])


def scan_fwd(r, lw, k, v, a, b, *, pg, name):
    s, c = r.shape
    tb, lw_ = SCAN_TB, PAIR * pg
    ng, nt = c // lw_, s // tb
    blk = pl.BlockSpec((tb, lw_), lambda g, t: (t, g))

    def body(r_ref, lw_ref, k_ref, v_ref, a_ref, b_ref, o_ref, ck_ref, s_ref, vt_ref, ot_ref):
        @pl.when(pl.program_id(1) == 0)
        def _():
            s_ref[...] = jnp.zeros_like(s_ref)

        ck_ref[0] = s_ref[...]
        left = lax.broadcasted_iota(jnp.int32, (HEAD, PAIR), 1) < HEAD
        lane_t = lax.broadcasted_iota(jnp.int32, (PAIR, tb), 1)
        for p in range(pg):
            vt_ref[p] = v_ref[:, p * PAIR:(p + 1) * PAIR].T
        ot_ref[...] = jnp.zeros_like(ot_ref)

        def step8(t8, carry):
            base = pl.multiple_of(t8 * 8, 8)
            rows = pl.ds(base, 8)
            r8, k8, a8, b8 = r_ref[rows, :], k_ref[rows, :], a_ref[rows, :], b_ref[rows, :]
            w8 = jnp.exp(lw_ref[rows, :])
            onehots = [lane_t == base + j for j in range(8)]
            for p in range(pg):
                ls = slice(p * PAIR, (p + 1) * PAIR)
                st = s_ref[p * HEAD:(p + 1) * HEAD, :]
                vt = vt_ref[p]
                for j in range(8):
                    row = lambda x8: x8[j:j + 1, ls]
                    vcol = _col_of(vt, onehots[j], left)
                    sa = _seg_bcast(st * row(a8), left)
                    st = st * row(w8) + sa * row(b8) + vcol * row(k8)
                    o0, o1 = _seg_cols(st * row(r8), left)
                    ot_ref[p, 0:HEAD, :] = jnp.where(onehots[j][0:HEAD], o0, ot_ref[p, 0:HEAD, :])
                    ot_ref[p, HEAD:PAIR, :] = jnp.where(onehots[j][0:HEAD], o1, ot_ref[p, HEAD:PAIR, :])
                s_ref[p * HEAD:(p + 1) * HEAD, :] = st
            return carry

        lax.fori_loop(0, tb // 8, step8, 0)
        for p in range(pg):
            o_ref[:, p * PAIR:(p + 1) * PAIR] = ot_ref[p].T

    return pl.pallas_call(
        body, name=name, grid=(ng, nt), in_specs=[blk] * 6,
        out_specs=[blk, pl.BlockSpec((1, pg * HEAD, PAIR), lambda g, t: (t, g, 0))],
        out_shape=[jax.ShapeDtypeStruct((s, c), F32), jax.ShapeDtypeStruct((nt, c // 2, PAIR), F32)],
        scratch_shapes=[pltpu.VMEM((pg * HEAD, PAIR), F32), pltpu.VMEM((pg, PAIR, tb), F32),
                        pltpu.VMEM((pg, PAIR, tb), F32)],
        compiler_params=_params(dimension_semantics=("arbitrary", "arbitrary")),
    )(r, lw, k, v, a, b)


def scan_bwd(r, lw, k, v, a, b, ck, do, dr_in, dk_in, dv_in, *, pg, name):
    s, c = r.shape
    tb, lw_ = SCAN_TB, PAIR * pg
    ng, nt = c // lw_, s // tb
    blk = pl.BlockSpec((tb, lw_), lambda g, t: (nt - 1 - t, g))
    ck_spec = pl.BlockSpec((1, pg * HEAD, PAIR), lambda g, t: (nt - 1 - t, g, 0))

    def body(r_ref, lw_ref, k_ref, v_ref, a_ref, b_ref, ck_ref, do_ref, dri_ref, dki_ref, dvi_ref,
             dr_ref, dlw_ref, dk_ref, dv_ref, da_ref, db_ref,
             s_ref, ds_ref, vt_ref, dot_ref, dvt_ref, sall_ref, saall_ref):
        @pl.when(pl.program_id(1) == 0)
        def _():
            ds_ref[...] = jnp.zeros_like(ds_ref)

        left = lax.broadcasted_iota(jnp.int32, (HEAD, PAIR), 1) < HEAD
        lane_t = lax.broadcasted_iota(jnp.int32, (PAIR, tb), 1)
        sub8 = lax.broadcasted_iota(jnp.int32, (8, PAIR), 0)
        for p in range(pg):
            vt_ref[p] = v_ref[:, p * PAIR:(p + 1) * PAIR].T
            dot_ref[p] = do_ref[:, p * PAIR:(p + 1) * PAIR].T
        dvt_ref[...] = jnp.zeros_like(dvt_ref)
        s_ref[...] = ck_ref[0]

        def fwd8(t8, carry):
            base = pl.multiple_of(t8 * 8, 8)
            rows = pl.ds(base, 8)
            k8, a8, b8 = k_ref[rows, :], a_ref[rows, :], b_ref[rows, :]
            w8 = jnp.exp(lw_ref[rows, :])
            onehots = [lane_t == base + j for j in range(8)]
            for p in range(pg):
                ls = slice(p * PAIR, (p + 1) * PAIR)
                hs = slice(p * HEAD, (p + 1) * HEAD)
                st = s_ref[hs, :]
                vt = vt_ref[p]
                for j in range(8):
                    row = lambda x8: x8[j:j + 1, ls]
                    sa = _seg_bcast(st * row(a8), left)
                    sall_ref[base + j, hs, :] = st
                    saall_ref[base + j, hs, :] = sa
                    st = st * row(w8) + sa * row(b8) + _col_of(vt, onehots[j], left) * row(k8)
                s_ref[hs, :] = st
            return carry

        lax.fori_loop(0, tb // 8, fwd8, 0)

        def bwd8(i8, carry):
            base = pl.multiple_of((tb // 8 - 1 - i8) * 8, 8)
            rows = pl.ds(base, 8)
            r8, k8, a8, b8 = r_ref[rows, :], k_ref[rows, :], a_ref[rows, :], b_ref[rows, :]
            w8 = jnp.exp(lw_ref[rows, :])
            onehots = [lane_t == base + j for j in range(8)]
            for p in range(pg):
                ls = slice(p * PAIR, (p + 1) * PAIR)
                hs = slice(p * HEAD, (p + 1) * HEAD)
                acc = [jnp.zeros((8, PAIR), F32) for _ in range(5)]
                ds = ds_ref[hs, :]
                vt, dot = vt_ref[p], dot_ref[p]
                for j in reversed(range(8)):
                    row = lambda x8: x8[j:j + 1, ls]
                    sp, sa = sall_ref[base + j, hs, :], saall_ref[base + j, hs, :]
                    vcol = _col_of(vt, onehots[j], left)
                    docol = _col_of(dot, onehots[j], left)
                    st = sp * row(w8) + sa * row(b8) + vcol * row(k8)
                    ds = ds + docol * row(r8)
                    dsa = _seg_bcast(ds * row(b8), left)
                    dv0, dv1 = _seg_cols(ds * row(k8), left)
                    dvt_ref[p, 0:HEAD, :] = jnp.where(onehots[j][0:HEAD], dv0, dvt_ref[p, 0:HEAD, :])
                    dvt_ref[p, HEAD:PAIR, :] = jnp.where(onehots[j][0:HEAD], dv1, dvt_ref[p, HEAD:PAIR, :])
                    new = [jnp.sum(st * docol, 0, keepdims=True),
                           jnp.sum(ds * sp, 0, keepdims=True),
                           jnp.sum(ds * vcol, 0, keepdims=True),
                           jnp.sum(sp * dsa, 0, keepdims=True),
                           jnp.sum(ds * sa, 0, keepdims=True)]
                    acc = [jnp.where(sub8 == j, n_, a_) for n_, a_ in zip(new, acc)]
                    ds = ds * row(w8) + dsa * row(a8)
                ds_ref[hs, :] = ds
                dr_ref[rows, ls] = acc[0] + dri_ref[rows, ls]
                dlw_ref[rows, ls] = acc[1] * w8[:, ls]
                dk_ref[rows, ls] = acc[2] + dki_ref[rows, ls]
                da_ref[rows, ls] = acc[3]
                db_ref[rows, ls] = acc[4]
            return carry

        lax.fori_loop(0, tb // 8, bwd8, 0)
        for p in range(pg):
            ls = slice(p * PAIR, (p + 1) * PAIR)
            dv_ref[:, ls] = dvt_ref[p].T + dvi_ref[:, ls]

    return pl.pallas_call(
        body, name=name, grid=(ng, nt), in_specs=[blk] * 6 + [ck_spec] + [blk] * 4, out_specs=[blk] * 6,
        out_shape=[jax.ShapeDtypeStruct((s, c), F32)] * 6,
        scratch_shapes=[pltpu.VMEM((pg * HEAD, PAIR), F32), pltpu.VMEM((pg * HEAD, PAIR), F32),
                        pltpu.VMEM((pg, PAIR, tb), F32), pltpu.VMEM((pg, PAIR, tb), F32),
                        pltpu.VMEM((pg, PAIR, tb), F32),
                        pltpu.VMEM((tb, pg * HEAD, PAIR), F32), pltpu.VMEM((tb, pg * HEAD, PAIR), F32)],
        compiler_params=_params(dimension_semantics=("arbitrary", "arbitrary")),
    )(r, lw, k, v, a, b, ck, do, dr_in, dk_in, dv_in)


def _place():
    x, y, c = lax.axis_index("x"), lax.axis_index("y"), lax.axis_index("c")
    return x, y, c, [(1 - x, y), (x, 1 - y), (1 - x, 1 - y)]


_ANY = pl.BlockSpec(memory_space=pl.ANY)


def into_slot(w, dtype, *, name):
    r, c = w.shape
    tr = _tile(r, max(8, (1 << 20) // (c * 4) // 16 * 16), unit=16) if r % 16 == 0 else r
    j_arr = (2 * lax.axis_index("x") + lax.axis_index("y")).astype(jnp.int32).reshape(1)

    def body(j_ref, w_ref, o_ref):
        o_ref[...] = w_ref[...].astype(o_ref.dtype)

    return pl.pallas_call(
        body, name=name,
        grid_spec=pltpu.PrefetchScalarGridSpec(
            num_scalar_prefetch=1, grid=(r // tr,),
            in_specs=[pl.BlockSpec((tr, c), lambda i, j_ref: (i, 0))],
            out_specs=pl.BlockSpec((None, tr, c), lambda i, j_ref: (j_ref[0], i, 0))),
        out_shape=jax.ShapeDtypeStruct((4, r, c), dtype),
        compiler_params=_params(dimension_semantics=("arbitrary",)),
    )(j_arr, w)


def allgather_multi(bufs, *, name):
    nb = len(bufs)

    def body(*refs):
        ins, (send_sems, recv_sems) = refs[:nb], refs[2 * nb:]
        x, y, c, chips = _place()
        sib = (x, y, 1 - c)

        def half(i, jj, cc):
            rh = bufs[i].shape[1] // 2
            return ins[i].at[jj, pl.ds(cc * rh, rh), :]

        def copy(i, kk, jj, cc, to):
            return pltpu.make_async_remote_copy(src_ref=half(i, jj, cc), dst_ref=half(i, jj, cc),
                                                send_sem=send_sems.at[6 * i + kk], recv_sem=recv_sems.at[6 * i + kk],
                                                device_id=to, device_id_type=MESH)

        first = [copy(i, kk, 2 * x + y, c, (cx, cy, c)) for i in range(nb) for kk, (cx, cy) in enumerate(chips)]
        for cp in first:
            cp.start()
        passed = []
        for i in range(nb):
            for kk, (cx, cy) in enumerate(chips):
                copy(i, kk, 2 * cx + cy, c, sib).wait_recv()
                passed.append(copy(i, 3 + kk, 2 * cx + cy, c, sib))
                passed[-1].start()
        for i in range(nb):
            for kk, (cx, cy) in enumerate(chips):
                copy(i, 3 + kk, 2 * cx + cy, 1 - c, sib).wait_recv()
        for cp in first + passed:
            cp.wait_send()

    return pl.pallas_call(
        body, name=name, in_specs=[_ANY] * nb, out_specs=[_ANY] * nb,
        out_shape=[jax.ShapeDtypeStruct(b.shape, b.dtype) for b in bufs],
        input_output_aliases={i: i for i in range(nb)},
        scratch_shapes=[pltpu.SemaphoreType.DMA((6 * nb,)), pltpu.SemaphoreType.DMA((6 * nb,))],
    )(*bufs)


def sibling_swap_multi(gs, *, name):
    nb = len(gs)

    def body(*refs):
        ins, outs, (send_sems, recv_sems) = refs[:nb], refs[nb:2 * nb], refs[2 * nb:]
        x, y, c, _ = _place()
        cps = []
        for i in range(nb):
            rh = gs[i].shape[1] // 2
            cps.append(pltpu.make_async_remote_copy(
                src_ref=ins[i].at[:, pl.ds((1 - c) * rh, rh), :], dst_ref=outs[i], send_sem=send_sems.at[i],
                recv_sem=recv_sems.at[i], device_id=(x, y, 1 - c), device_id_type=MESH))
        for cp in cps:
            cp.start()
        for cp in cps:
            cp.wait()

    return pl.pallas_call(
        body, name=name, in_specs=[_ANY] * nb, out_specs=[_ANY] * nb,
        out_shape=[jax.ShapeDtypeStruct((4, g.shape[1] // 2, g.shape[2]), g.dtype) for g in gs],
        scratch_shapes=[pltpu.SemaphoreType.DMA((nb,)), pltpu.SemaphoreType.DMA((nb,))],
    )(*gs)


def pair_sum(g, got, *, tr, name):
    _, rh, w = got.shape
    nb = rh // tr
    c_arr = lax.axis_index("c").astype(jnp.int32).reshape(1)

    def body(c_ref, g_ref, got_ref, o_ref):
        o_ref[...] = (g_ref[...].astype(F32) + got_ref[...].astype(F32)).astype(o_ref.dtype)

    return pl.pallas_call(
        body, name=name,
        grid_spec=pltpu.PrefetchScalarGridSpec(
            num_scalar_prefetch=1, grid=(4, nb),
            in_specs=[pl.BlockSpec((1, tr, w), lambda j, i, c_ref: (j, c_ref[0] * nb + i, 0)),
                      pl.BlockSpec((1, tr, w), lambda j, i, c_ref: (j, i, 0))],
            out_specs=pl.BlockSpec((1, tr, w), lambda j, i, c_ref: (j, i, 0))),
        out_shape=jax.ShapeDtypeStruct((4, rh, w), got.dtype),
        compiler_params=_params(dimension_semantics=("arbitrary", "arbitrary")),
    )(c_arr, g, got)


def scatter_multi(pss, *, name):
    nb = len(pss)

    def body(*refs):
        ins, outs, (send_sems, recv_sems) = refs[:nb], refs[nb:2 * nb], refs[2 * nb:]
        x, y, c, chips = _place()
        cps = [pltpu.make_async_remote_copy(src_ref=ins[i].at[2 * cx + cy], dst_ref=outs[i].at[kk],
                                            send_sem=send_sems.at[3 * i + kk], recv_sem=recv_sems.at[3 * i + kk],
                                            device_id=(cx, cy, c), device_id_type=MESH)
               for i in range(nb) for kk, (cx, cy) in enumerate(chips)]
        for cp in cps:
            cp.start()
        for cp in cps:
            cp.wait()

    return pl.pallas_call(
        body, name=name, in_specs=[_ANY] * nb, out_specs=[_ANY] * nb,
        out_shape=[jax.ShapeDtypeStruct((3,) + p.shape[1:], p.dtype) for p in pss],
        scratch_shapes=[pltpu.SemaphoreType.DMA((3 * nb,)), pltpu.SemaphoreType.DMA((3 * nb,))],
    )(*pss)


def chip_sum(ps, got, *, tr, name):
    _, rh, w = ps.shape
    nb = rh // tr
    jc_arr = jnp.stack([2 * lax.axis_index("x") + lax.axis_index("y"), lax.axis_index("c")]).astype(jnp.int32)

    def body(jc_ref, ps_ref, got_ref, o_ref):
        acc = ps_ref[0].astype(F32)
        for kk in range(3):
            acc = acc + got_ref[kk].astype(F32)
        o_ref[...] = acc

    return pl.pallas_call(
        body, name=name,
        grid_spec=pltpu.PrefetchScalarGridSpec(
            num_scalar_prefetch=1, grid=(nb,),
            in_specs=[pl.BlockSpec((1, tr, w), lambda i, jc: (jc[0], i, 0)),
                      pl.BlockSpec((3, tr, w), lambda i, jc: (0, i, 0))],
            out_specs=pl.BlockSpec((tr, w), lambda i, jc: (jc[1] * nb + i, 0))),
        out_shape=jax.ShapeDtypeStruct((2 * rh, w), F32),
        compiler_params=_params(dimension_semantics=("arbitrary",)),
    )(jc_arr, ps, got)


def sibling_join_multi(reds, *, name):
    nb = len(reds)

    def body(*refs):
        ins, (send_sems, recv_sems) = refs[:nb], refs[2 * nb:]
        x, y, c, _ = _place()

        def copy(i, cc):
            rh = reds[i].shape[0] // 2
            rows = ins[i].at[pl.ds(cc * rh, rh), :]
            return pltpu.make_async_remote_copy(src_ref=rows, dst_ref=rows, send_sem=send_sems.at[i],
                                                recv_sem=recv_sems.at[i], device_id=(x, y, 1 - c), device_id_type=MESH)

        cps = [copy(i, c) for i in range(nb)]
        for cp in cps:
            cp.start()
        for i, cp in enumerate(cps):
            cp.wait_send()
            copy(i, 1 - c).wait_recv()

    return pl.pallas_call(
        body, name=name, in_specs=[_ANY] * nb, out_specs=[_ANY] * nb,
        out_shape=[jax.ShapeDtypeStruct(r.shape, r.dtype) for r in reds],
        input_output_aliases={i: i for i in range(nb)},
        scratch_shapes=[pltpu.SemaphoreType.DMA((nb,)), pltpu.SemaphoreType.DMA((nb,))],
    )(*reds)


def allreduce_small(part, *, name):
    m_per, n = part.shape

    def body(x_ref, sum_ref, all_ref, send_sems, recv_sems, local_sem):
        x, y, c, chips = _place()
        me, sib = (x, y, c), (x, y, 1 - c)

        def rows(px, py, pc):
            return all_ref.at[pl.ds((4 * px + 2 * py + pc) * m_per, m_per), :]

        def copy(kk, block, to, src=None):
            return pltpu.make_async_remote_copy(src_ref=rows(*block) if src is None else src, dst_ref=rows(*block),
                                                send_sem=send_sems.at[kk], recv_sem=recv_sems.at[kk],
                                                device_id=to, device_id_type=MESH)

        mine = pltpu.make_async_copy(x_ref, rows(*me), local_sem)
        mine.start()
        first = [copy(0, me, sib, src=x_ref)]
        first += [copy(1 + kk, me, (*chip, c), src=x_ref) for kk, chip in enumerate(chips)]
        for cp in first:
            cp.start()
        passed = [copy(4 + kk, (*chip, c), sib) for kk, chip in enumerate(chips)]
        for kk, chip in enumerate(chips):
            copy(1 + kk, (*chip, c), me).wait_recv()
            passed[kk].start()
        copy(0, sib, me).wait_recv()
        for kk, chip in enumerate(chips):
            copy(4 + kk, (*chip, 1 - c), me).wait_recv()
        for cp in first + passed:
            cp.wait_send()
        mine.wait()
        acc = all_ref[0:m_per, :]
        for d in range(1, 8):
            acc = acc + all_ref[d * m_per:(d + 1) * m_per, :]
        sum_ref[...] = acc

    vmem = pl.BlockSpec(memory_space=pltpu.VMEM)
    return pl.pallas_call(
        body, name=name, in_specs=[vmem], out_specs=vmem,
        out_shape=jax.ShapeDtypeStruct((m_per, n), part.dtype),
        scratch_shapes=[pltpu.VMEM((8 * m_per, n), part.dtype), pltpu.SemaphoreType.DMA((7,)),
                        pltpu.SemaphoreType.DMA((7,)), pltpu.SemaphoreType.DMA],
    )(part)


def adamw(w, g, m, v, *, name):
    r, c = w.shape
    tm = r if r * c * 4 <= (1 << 20) else _tile(r, max(8, ((1 << 20) // (c * 4)) // 8 * 8), unit=8)
    bc1, bc2 = 1.0 - ADAM_B1 ** ADAM_STEP, 1.0 - ADAM_B2 ** ADAM_STEP

    def fn(i, n, rv, cv, sc):
        w_, g_, m_, v_ = rv
        m_ = ADAM_B1 * m_ + (1.0 - ADAM_B1) * g_
        v_ = ADAM_B2 * v_ + (1.0 - ADAM_B2) * (g_ * g_)
        delta = -ADAM_LR * ((m_ / bc1) / (jnp.sqrt(v_ / bc2) + ADAM_EPS) + ADAM_WD * w_)
        return [delta, m_, v_], []

    return rowcall(fn, [w, g, m, v], [], [(c, F32)] * 3, [], tm=tm, name=name)[0]


def _head_one_hot(c):
    e = (lax.broadcasted_iota(jnp.int32, (c, LANES), 0) // HEAD
         == lax.broadcasted_iota(jnp.int32, (c, LANES), 1)).astype(F32)
    return e, e.T


def _join_cols(g):
    return jnp.concatenate([g[j] for j in range(4)], axis=1)


def _split_cols(a):
    return jnp.stack(jnp.split(a, 4, axis=1))


def layer_step(x, mem, tgt, wg, sp):
    s, d = x.shape
    dr, dc = sp["rwkv_w0"].shape[1], sp["conv_b"].shape[1]
    n_lora = N_DECAY + N_ICLR + N_GATE
    n_rwkv = 3 * dr + n_lora
    pad_l = N_LORA_PAD - n_lora
    w_in = _join_cols(wg["w_in"])
    cuts = [0, dr, 2 * dr, 3 * dr, n_rwkv, n_rwkv + dc, n_rwkv + 2 * dc, n_rwkv + 2 * dc + d, n_rwkv + 2 * dc + 2 * d]
    w_r, w_k, w_v, w_l, w_ca, w_cb, w_gr, w_gc = (w_in[:, lo:hi] for lo, hi in zip(cuts[:-1], cuts[1:]))
    w_l = jnp.pad(w_l, ((0, 0), (0, pad_l)))
    sm = sp["rwkv_shift_mix"]
    mus = [sm[:, 0:dr], sm[:, dr:2 * dr], sm[:, 2 * dr:3 * dr], jnp.pad(sm[:, 3 * dr:], ((0, 0), (0, pad_l)))]
    w_up = jnp.pad(_join_cols(wg["rwkv_w_up"]).astype(F32), ((0, LANES - N_DECAY), (0, 0)))
    a_up = jnp.pad(_join_cols(wg["rwkv_a_up"]).astype(F32), ((N_DECAY, 2 * LANES - N_DECAY - N_ICLR), (0, 0)))
    g_lo = N_DECAY + N_ICLR - LANES
    g_up = jnp.pad(_join_cols(wg["rwkv_g_up"]).astype(F32), ((g_lo, pad_l), (0, 0)))
    conv_w = _join_cols(wg["conv_w"])
    wt = {n: wg[n].reshape(-1, wg[n].shape[2]) for n in ("w_out", "xattn_wq", "xattn_wk", "xattn_wv", "xattn_wo", "mlp_w2")}
    e, et = _head_one_hot(dr)
    pre_p = [sp["rwkv_w0"], w_up, sp["rwkv_a0"], a_up, g_up, sp["rwkv_k_k"], sp["rwkv_k_a"]]
    post_p = [sp["rwkv_r_k"], sp["rwkv_gn_g"], sp["rwkv_gn_b"]]
    pairs = dr // PAIR
    tm = min(s, 128)
    tmm = mem.shape[0]

    z_r, z_k, z_v, z_l = (mm(x, w, name=f"z_{n}") for n, w in zip("rkvl", (w_r, w_k, w_v, w_l)))
    z_ca, z_cb = mm(x, w_ca, name="z_ca"), mm(x, w_cb, name="z_cb")
    z_gr, z_gc = mm(x, w_gr, name="z_gr"), mm(x, w_gc, name="z_gc")
    zs_r, zs_k, zs_v, zs_l = tokenshift_fwd([z_r, z_k, z_v, z_l], mus, tm=tm, name="shift_fwd")
    pre_o = [(dr, F32)] * 5
    k_m, lw, a_s, b_s, g = stage_fwd(f_rwkv_pre, [zs_k, zs_l], pre_p, [e, et], pre_o, tm=min(s, 64), name="pre_fwd")
    o, ck = scan_fwd(zs_r, lw, k_m, zs_v, a_s, b_s, pg=min(8, pairs), name="scan_fwd")
    post_r = [o, zs_r, k_m, zs_v, g]
    (o_r,) = stage_fwd(f_rwkv_post, post_r, post_p, [e, et], [(dr, BF16)], tm=min(s, 64), name="post_fwd")
    (u,) = stage_fwd(f_glu, [z_ca, z_cb], [], [], [(dc, F32)], tm=tm, name="glu_fwd")
    cv = conv_fwd(u, conv_w, sp["conv_b"], tm=tm, name="conv_fwd")
    cln_p = [sp["conv_ln_g"], sp["conv_ln_b"]]
    (o_c,) = stage_fwd(f_convln, [cv], cln_p, [], [(dc, BF16)], tm=tm, name="convln_fwd")
    p_r = mm(o_r, wg["proj_rwkv"], b_sh=True, name="proj_r")
    p_c = mm(o_c, wg["proj_conv"], b_sh=True, name="proj_c")
    (merged,) = stage_fwd(f_merge, [z_gr, z_gc, p_r, p_c], [], [], [(d, BF16)], tm=tm, name="merge_fwd")
    y1 = mm(merged, wt["w_out"], name="y1")
    ln1_p, ln2_p, lnm_p = ([sp[f"{n}_g"], sp[f"{n}_b"]] for n in ("ln1", "ln2", "ln_mem"))
    (h1,) = stage_fwd(f_resln, [x, y1], ln1_p, [], [(d, F32)], tm=tm, name="ln1_fwd")
    (mem_n,) = stage_fwd(f_ln, [mem], lnm_p, [], [(d, F32)], tm=tmm, name="lnmem_fwd")
    k_mem, v_mem = mm(mem_n, wt["xattn_wk"], name="k_mem"), mm(mem_n, wt["xattn_wv"], name="v_mem")
    q = mm(h1, wt["xattn_wq"], name="q")
    (ao,) = stage_fwd(f_attn, [q], [k_mem, v_mem], [], [(d, BF16)], tm=tm, name="attn_fwd")
    ca = mm(ao, wt["xattn_wo"], name="ca")
    (h2,) = stage_fwd(f_resln, [h1, ca], ln2_p, [], [(d, F32)], tm=tm, name="ln2_fwd")
    u1 = mm(h2, wg["mlp_w1"], b_sh=True, name="u1")
    f_dim = u1.shape[1]
    tmf = min(s, 64)
    (act,) = stage_fwd(f_relu2, [u1], [], [], [(f_dim, BF16)], tm=tmf, name="relu2_fwd")
    ff = mm(act, wt["mlp_w2"], name="ff")

    gw, gs = {}, {}
    loss, dh2, dff, gs["ln3_g"], gs["ln3_b"] = loss_bwd(h2, ff, tgt, sp["ln3_g"], sp["ln3_b"], tm=tm, name="loss_bwd")
    gw["mlp_w2"] = mm(act, dff, ta=True, out_dtype=BF16, name="g_mlp_w2")
    dact = mm(dff, wt["mlp_w2"], tb=True, name="d_act")
    (du1,), _ = stage_bwd(f_relu2, [u1], [], [], [dact], [(0, BF16)], tm=tmf, name="relu2_bwd")
    gw["mlp_w1"] = mm(h2, du1, ta=True, out_dtype=BF16, out_sh=True, name="g_mlp_w1")
    dh2 = mm(du1, wg["mlp_w1"], tb=True, b_sh=True, acc=dh2, name="d_h2")
    (dh1, dca), (gs["ln2_g"], gs["ln2_b"]) = stage_bwd(f_resln, [h1, ca], ln2_p, [], [dh2], [(0, F32), (1, F32)],
                                                       tm=tm, name="ln2_bwd")
    gw["xattn_wo"] = mm(ao, dca, ta=True, out_dtype=BF16, name="g_wo")
    dao = mm(dca, wt["xattn_wo"], tb=True, name="d_ao")
    (dq,), (dk_mem, dv_mem) = stage_bwd(f_attn, [q], [k_mem, v_mem], [], [dao], [(0, F32)], tm=tm, name="attn_bwd")
    gw["xattn_wq"] = mm(h1, dq, ta=True, out_dtype=BF16, name="g_wq")
    dh1 = mm(dq, wt["xattn_wq"], tb=True, acc=dh1, name="d_h1")
    gw["xattn_wk"] = mm(mem_n, dk_mem, ta=True, out_dtype=BF16, name="g_wk")
    gw["xattn_wv"] = mm(mem_n, dv_mem, ta=True, out_dtype=BF16, name="g_wv")
    dmem_n = mm(dk_mem, wt["xattn_wk"], tb=True, name="d_memn_k")
    dmem_n = mm(dv_mem, wt["xattn_wv"], tb=True, acc=dmem_n, name="d_memn_v")
    _, (gs["ln_mem_g"], gs["ln_mem_b"]) = stage_bwd(f_ln, [mem], lnm_p, [], [dmem_n], [], tm=tmm, name="lnmem_bwd")
    (dx, dy1), (gs["ln1_g"], gs["ln1_b"]) = stage_bwd(f_resln, [x, y1], ln1_p, [], [dh1], [(0, F32), (1, F32)],
                                                      tm=tm, name="ln1_bwd")
    gw["w_out"] = mm(merged, dy1, ta=True, out_dtype=BF16, name="g_w_out")
    dmerged = mm(dy1, wt["w_out"], tb=True, name="d_merged")
    (dz_gr, dz_gc, dp_r, dp_c), _ = stage_bwd(f_merge, [z_gr, z_gc, p_r, p_c], [], [], [dmerged],
                                              [(0, BF16), (1, BF16), (2, F32), (3, F32)], tm=tm, name="merge_bwd")
    gw["proj_rwkv"] = mm(o_r, dp_r, ta=True, out_dtype=BF16, out_sh=True, name="g_proj_r")
    gw["proj_conv"] = mm(o_c, dp_c, ta=True, out_dtype=BF16, out_sh=True, name="g_proj_c")
    do_r = mm(dp_r, wg["proj_rwkv"], tb=True, b_sh=True, name="d_o_r")
    do_c = mm(dp_c, wg["proj_conv"], tb=True, b_sh=True, name="d_o_c")
    (dcv,), (gs["conv_ln_g"], gs["conv_ln_b"]) = stage_bwd(f_convln, [cv], cln_p, [], [do_c], [(0, F32)],
                                                           tm=tm, name="convln_bwd")
    du, g_conv_w, gs["conv_b"] = conv_bwd(dcv, u, conv_w, tm=tm, name="conv_bwd")
    gw["conv_w"] = _split_cols(g_conv_w.astype(BF16))
    (dz_ca, dz_cb), _ = stage_bwd(f_glu, [z_ca, z_cb], [], [], [du], [(0, BF16), (1, BF16)], tm=tm, name="glu_bwd")
    (d_o, dr_p, dk_p, dv_p, dg), (gs["rwkv_r_k"], gs["rwkv_gn_g"], gs["rwkv_gn_b"]) = stage_bwd(
        f_rwkv_post, post_r, post_p, [e, et], [do_r], [(k, F32) for k in range(5)], tm=min(s, 64), name="post_bwd")
    dzs_r, dlw, dk_m, dzs_v, da_s, db_s = scan_bwd(zs_r, lw, k_m, zs_v, a_s, b_s, ck, d_o, dr_p, dk_p, dv_p,
                                                   pg=min(4, pairs), name="scan_bwd")
    (dzs_k, dzs_l), pre_g = stage_bwd(f_rwkv_pre, [zs_k, zs_l], pre_p, [e, et], [dk_m, dlw, da_s, db_s, dg],
                                      [(0, F32), (1, F32)], tm=min(s, 64), name="pre_bwd")
    gs["rwkv_w0"], g_w_up, gs["rwkv_a0"], g_a_up, g_g_up, gs["rwkv_k_k"], gs["rwkv_k_a"] = pre_g
    gw["rwkv_w_up"] = _split_cols(g_w_up[0:N_DECAY].astype(BF16))
    gw["rwkv_a_up"] = _split_cols(g_a_up[N_DECAY:N_DECAY + N_ICLR].astype(BF16))
    gw["rwkv_g_up"] = _split_cols(g_g_up[g_lo:g_lo + N_GATE].astype(BF16))
    dzs, dmus = tokenshift_bwd([dzs_r, dzs_k, dzs_v, dzs_l], [z_r, z_k, z_v, z_l], mus, tm=tm, name="shift_bwd")
    gs["rwkv_shift_mix"] = jnp.concatenate(list(dmus[:3]) + [dmus[3][:, 0:n_lora]], axis=1)
    dzs = list(dzs) + [dz_ca, dz_cb, dz_gr, dz_gc]
    g_in = []
    for n, dz, w in zip(("r", "k", "v", "l", "ca", "cb", "gr", "gc"), dzs, (w_r, w_k, w_v, w_l, w_ca, w_cb, w_gr, w_gc)):
        g_in.append(mm(x, dz, ta=True, out_dtype=BF16, name=f"g_w_{n}"))
        dx = mm(dz, w, tb=True, acc=dx, name=f"d_x_{n}")
    g_in[3] = g_in[3][:, 0:n_lora]
    gw["w_in"] = _split_cols(jnp.concatenate(g_in, axis=1))
    for n in wt:
        gw[n] = gw[n].reshape(wg[n].shape)
    return loss, dx, gw, gs


WEIGHTS = ["w_in", "rwkv_shift_mix", "rwkv_w0", "rwkv_w_up", "rwkv_a0", "rwkv_a_up", "rwkv_g_up", "rwkv_k_k",
           "rwkv_k_a", "rwkv_r_k", "rwkv_gn_g", "rwkv_gn_b", "conv_w", "conv_b", "conv_ln_g", "conv_ln_b",
           "proj_rwkv", "proj_conv", "w_out", "ln1_g", "ln1_b", "ln_mem_g", "ln_mem_b", "xattn_wq", "xattn_wk",
           "xattn_wv", "xattn_wo", "ln2_g", "ln2_b", "mlp_w1", "mlp_w2", "ln3_g", "ln3_b"]
SHARD_AXIS = {"w_in": 1, "rwkv_w_up": 1, "rwkv_a_up": 1, "rwkv_g_up": 1, "conv_w": 1, "proj_rwkv": 1, "proj_conv": 1,
              "w_out": 0, "xattn_wq": 0, "xattn_wk": 0, "xattn_wv": 0, "xattn_wo": 0, "mlp_w1": 1, "mlp_w2": 0}


def _unpack(flat, shapes):
    out, off = [], 0
    for shp in shapes:
        n = 1
        for dim in shp:
            n *= dim
        out.append(flat[..., off:off + n].reshape(flat.shape[:-1] + tuple(shp)))
        off += n
    return out


def _half_tile(rh, w):
    return _tile(rh, max(16, (2 << 20) // (w * 4) // 16 * 16), unit=16)


def kernel(*args):
    n_w = len(WEIGHTS)
    x, mem = args[0][0], args[1][0]
    tgt = args[2 + n_w][0]
    w_loc = {n: a for n, a in zip(WEIGHTS, args[2:2 + n_w])}
    m_loc = {n: a for n, a in zip(WEIGHTS, args[3 + n_w:3 + 2 * n_w])}
    v_loc = {n: a for n, a in zip(WEIGHTS, args[3 + 2 * n_w:3 + 3 * n_w])}
    big = [n for n in WEIGHTS if n in SHARD_AXIS]
    small = [n for n in WEIGHTS if n not in SHARD_AXIS]

    def as2d(n, a):
        if n in SHARD_AXIS:
            return a.reshape(a.shape[1], a.shape[-1])
        return a.reshape(1, -1)

    loc2d = {n: as2d(n, w_loc[n]) for n in WEIGHTS}
    conv_rows = loc2d["conv_w"].shape[0]
    slots = []
    for n in big:
        if n == "conv_w":
            padded = jnp.pad(loc2d[n], ((0, CONV_HALO - conv_rows), (0, 0)))
            slots.append(into_slot(padded, F32, name=f"slot_{n}"))
        else:
            slots.append(into_slot(loc2d[n], BF16, name=f"slot_{n}"))
    wg = dict(zip(big, allgather_multi(slots, name="allgather_weights")))
    sp = {n: loc2d[n] for n in small}

    loss_part, grad_x, gw, gs = layer_step(x, mem, tgt, wg, sp)

    g_list = [gw[n] for n in big]
    tiles = [_half_tile(g.shape[1] // 2, g.shape[2]) for g in g_list]
    gots = sibling_swap_multi(g_list, name="rs_sibling_swap")
    pss = [pair_sum(g, got, tr=tr, name=f"rs_pair_sum_{n}") for n, g, got, tr in zip(big, g_list, gots, tiles)]
    gots = scatter_multi(pss, name="rs_scatter")
    reds = [chip_sum(ps, got, tr=tr, name=f"rs_chip_sum_{n}") for n, ps, got, tr in zip(big, pss, gots, tiles)]
    reds = sibling_join_multi(reds, name="rs_sibling_join")
    g_big = {n: (r[0:conv_rows] if n == "conv_w" else r) for n, r in zip(big, reds)}

    small_parts = [gs[n] for n in small] + [loss_part[0:1, 0:1]]
    flat = jnp.concatenate([p.reshape(-1).astype(F32) for p in small_parts])
    flat = jnp.pad(flat, (0, -flat.shape[0] % (8 * LANES))).reshape(-1, LANES)
    red = allreduce_small(flat, name="allreduce_small").reshape(-1)
    g_small = dict(zip(small, _unpack(red, [loc2d[n].shape for n in small])))
    n_small = sum(loc2d[n].shape[1] for n in small)
    loss = red[n_small]

    grads, deltas, new_m, new_v = {}, {}, {}, {}
    for n in big:
        g2 = g_big[n]
        d2, m2, v2 = adamw(loc2d[n], g2, as2d(n, m_loc[n]), as2d(n, v_loc[n]), name=f"adamw_{n}")
        shp = w_loc[n].shape
        grads[n], deltas[n], new_m[n], new_v[n] = (t.reshape(shp) for t in (g2, d2, m2, v2))

    def small_pack(d):
        f = jnp.concatenate([as2d(n, d[n]).reshape(-1) for n in small])
        return jnp.pad(f, (0, -f.shape[0] % (8 * LANES))).reshape(-1, LANES)

    g_pack = small_pack({n: g_small[n] for n in small})
    outs = adamw(small_pack(w_loc), g_pack, small_pack(m_loc), small_pack(v_loc), name="adamw_small")
    for dst, packed in zip((deltas, new_m, new_v), outs):
        for n, t in zip(small, _unpack(packed.reshape(-1), [loc2d[n].shape for n in small])):
            dst[n] = t.reshape(w_loc[n].shape)
    for n in small:
        grads[n] = g_small[n].reshape(w_loc[n].shape)

    return (loss, grad_x[None], *[grads[n] for n in WEIGHTS], *[deltas[n] for n in WEIGHTS],
            *[new_m[n] for n in WEIGHTS], *[new_v[n] for n in WEIGHTS])
```

```python
import functools

import jax
import jax.numpy as jnp
from jax import lax
from jax.experimental import pallas as pl
from jax.experimental.pallas import tpu as pltpu

F32, BF16 = jnp.float32, jnp.bfloat16
ALPHA = 2.0 ** 0.25
LN_EPS = 1e-5
GN_EPS = 64e-5
HEAD = 64
LANES = 128
PAIR = 2 * HEAD
XATTN_HEADS = 4
CONV_WIDTH = 31
CONV_HALO = 32
N_DECAY, N_ICLR, N_GATE = 96, 96, 256
N_LORA_PAD = 512
VMEM_LIMIT = 56 * 1024 * 1024
ADAM_LR, ADAM_B1, ADAM_B2, ADAM_EPS, ADAM_WD, ADAM_STEP = 0.001, 0.9, 0.999, 1e-8, 0.01, 10
MESH = pl.DeviceIdType.MESH
HI = lax.Precision.HIGHEST


def _params(**kw):
    return pltpu.CompilerParams(vmem_limit_bytes=VMEM_LIMIT, **kw)


def _tile(n, pref, unit=LANES):
    if n <= pref:
        return n
    t = pref
    while n % t:
        t -= unit
    return t


def mm(a, b, *, name, ta=False, tb=False, out_dtype=F32, acc=None, b_sh=False, out_sh=False):
    m, k = (a.shape[1], a.shape[0]) if ta else a.shape
    if b_sh:
        n = b.shape[1] if tb else 4 * b.shape[2]
    else:
        n = b.shape[0] if tb else b.shape[1]
    tm, tn, tk = _tile(m, 1024), _tile(n // 4 if (b_sh and not tb) or out_sh else n, 1024), _tile(k // 4 if b_sh and tb else k, 512)
    nk = k // tk
    a_spec = pl.BlockSpec((tk, tm), lambda i, j, kk: (kk, i)) if ta else pl.BlockSpec((tm, tk), lambda i, j, kk: (i, kk))
    if b_sh and tb:
        per_k = k // 4 // tk
        b_spec = pl.BlockSpec((None, tn, tk), lambda i, j, kk: (kk // per_k, j, kk % per_k))
    elif b_sh:
        per_n = n // 4 // tn
        b_spec = pl.BlockSpec((None, tk, tn), lambda i, j, kk: (j // per_n, kk, j % per_n))
    else:
        b_spec = pl.BlockSpec((tn, tk), lambda i, j, kk: (j, kk)) if tb else pl.BlockSpec((tk, tn), lambda i, j, kk: (kk, j))
    if out_sh:
        per_o = n // 4 // tn
        o_spec = pl.BlockSpec((None, tm, tn), lambda i, j, kk: (j // per_o, i, j % per_o))
    else:
        o_spec = pl.BlockSpec((tm, tn), lambda i, j, kk: (i, j))
    dims = (((0 if ta else 1,), (1 if tb else 0,)), ((), ()))

    def body(*refs):
        if acc is None:
            a_ref, b_ref, o_ref, acc_ref = refs
        else:
            a_ref, b_ref, c_ref, o_ref, acc_ref = refs
        kk = pl.program_id(2)

        @pl.when(kk == 0)
        def _():
            acc_ref[...] = jnp.zeros_like(acc_ref) if acc is None else c_ref[...].astype(F32)

        acc_ref[...] += lax.dot_general(a_ref[...].astype(BF16), b_ref[...].astype(BF16), dims,
                                        preferred_element_type=F32)

        @pl.when(kk == nk - 1)
        def _():
            o_ref[...] = acc_ref[...].astype(o_ref.dtype)

    ins = [a, b] + ([] if acc is None else [acc])
    in_specs = [a_spec, b_spec] + ([] if acc is None else [o_spec])
    return pl.pallas_call(
        body, name=name, grid=(m // tm, n // tn, nk),
        in_specs=in_specs, out_specs=o_spec,
        out_shape=jax.ShapeDtypeStruct((4, m, n // 4) if out_sh else (m, n), out_dtype),
        scratch_shapes=[pltpu.VMEM((tm, tn), F32)],
        input_output_aliases={} if acc is None else {2: 0},
        compiler_params=_params(dimension_semantics=("arbitrary", "arbitrary", "arbitrary")),
    )(*ins)


def rowcall(fn, rows, consts, out_rows, out_accs, *, tm, name, scratch=()):
    rows = [r if isinstance(r, tuple) else (r, None) for r in rows]
    s = rows[0][0].shape[0]
    n = s // tm
    in_specs, ins = [], []
    for arr, halo in rows:
        w = arr.shape[1]
        in_specs.append(pl.BlockSpec((tm, w), lambda i: (i, 0)))
        ins.append(arr)
        if halo is not None:
            kind, h = halo
            per = tm // h
            if kind == "prev":
                in_specs.append(pl.BlockSpec((h, w), lambda i, per=per: (jnp.maximum(i * per - 1, 0), 0)))
            else:
                in_specs.append(pl.BlockSpec((h, w), lambda i, per=per, last=s // h - 1: (jnp.minimum((i + 1) * per, last), 0)))
            ins.append(arr)
    for cst in consts:
        in_specs.append(pl.BlockSpec(cst.shape, lambda i, nd=cst.ndim: (0,) * nd))
        ins.append(cst)
    out_specs = [pl.BlockSpec((tm, w), lambda i: (i, 0)) for w, _ in out_rows]
    out_specs += [pl.BlockSpec(shp, lambda i, nd=len(shp): (0,) * nd) for shp, _ in out_accs]
    out_shape = [jax.ShapeDtypeStruct((s, w), dt) for w, dt in out_rows]
    out_shape += [jax.ShapeDtypeStruct(shp, dt) for shp, dt in out_accs]
    n_in, n_or, n_oa = len(ins), len(out_rows), len(out_accs)

    def body(*refs):
        i = pl.program_id(0)
        it = iter(refs[:n_in])
        row_vals = []
        for _, halo in rows:
            cur = next(it)[...]
            row_vals.append(cur if halo is None else (cur, next(it)[...]))
        const_vals = [r[...] for r in it]
        o_refs = refs[n_in:n_in + n_or]
        a_refs = refs[n_in + n_or:n_in + n_or + n_oa]
        outs, parts = fn(i, n, row_vals, const_vals, refs[n_in + n_or + n_oa:])
        for o_ref, val in zip(o_refs, outs, strict=True):
            o_ref[...] = val.astype(o_ref.dtype)
        for a_ref, part in zip(a_refs, parts, strict=True):
            part = jnp.broadcast_to(part, a_ref.shape).astype(a_ref.dtype)

            @pl.when(i == 0)
            def _(a_ref=a_ref, part=part):
                a_ref[...] = part

            @pl.when(i > 0)
            def _(a_ref=a_ref, part=part):
                a_ref[...] += part

    res = pl.pallas_call(
        body, name=name, grid=(n,), in_specs=in_specs, out_specs=out_specs, out_shape=out_shape,
        scratch_shapes=list(scratch),
        compiler_params=_params(dimension_semantics=("arbitrary",)),
    )(*ins)
    return res[:n_or], res[n_or:]


def stage_fwd(f, rows, params, consts, outs, *, tm, name):
    n_p = len(params)

    def fn(i, n, rv, cv, sc):
        return f([r.astype(F32) for r in rv], cv[:n_p], cv[n_p:]), []

    return rowcall(fn, rows, list(params) + list(consts), outs, [], tm=tm, name=name)[0]


def stage_bwd(f, rows, params, consts, cts, row_grads, *, tm, name):
    n_r, n_p = len(rows), len(params)

    def fn(i, n, rv, cv, sc):
        r = [x.astype(F32) for x in rv[:n_r]]
        ct = [x.astype(F32) for x in rv[n_r:]]
        _, vjp = jax.vjp(lambda r_, p_: f(r_, p_, cv[n_p:]), r, list(cv[:n_p]))
        d_r, d_p = vjp(ct)
        return [d_r[k] for k, _ in row_grads], d_p

    return rowcall(fn, list(rows) + list(cts), list(params) + list(consts),
                   [(rows[k].shape[1], dt) for k, dt in row_grads],
                   [(p.shape, F32) for p in params], tm=tm, name=name)


def _ln(x, g, b, eps=LN_EPS):
    xc = x - jnp.mean(x, -1, keepdims=True)
    var = jnp.mean(xc * xc, -1, keepdims=True)
    return xc * lax.rsqrt(var + eps) * g + b


def _sigmoid(x):
    return 1.0 / (1.0 + jnp.exp(-x))


def _softplus(x):
    return jnp.maximum(x, 0.0) + jnp.log(1.0 + jnp.exp(-jnp.abs(x)))


def _bdot(a, b):
    return jnp.dot(a.astype(BF16), b.astype(BF16), preferred_element_type=F32)


def _head_sum(x, e, et):
    return jnp.dot(jnp.dot(x, e, precision=HI, preferred_element_type=F32), et, precision=HI,
                   preferred_element_type=F32)


def f_rwkv_pre(rows, params, consts):
    zk, zl = rows
    w0, w_up, a0, a_up, g_up, k_k, k_a = params
    e, et = consts
    w = -_softplus(-(w0 + _bdot(jnp.tanh(zl[:, 0:LANES]), w_up))) - 0.5
    log_decay = -jnp.exp(w)
    a = _sigmoid(a0 + _bdot(zl[:, 0:2 * LANES], a_up))
    g = _bdot(_sigmoid(zl[:, LANES:N_LORA_PAD]), g_up)
    kk = zk * k_k
    kk = kk / jnp.maximum(jnp.sqrt(_head_sum(kk * kk, e, et)), 1e-12)
    return [zk * (1.0 + (a - 1.0) * k_a), log_decay, -kk, kk * a, g]


def f_rwkv_post(rows, params, consts):
    o, r, k, v, g = rows
    r_k, gn_g, gn_b = params
    e, et = consts
    oc = o - _head_sum(o, e, et) * (1.0 / HEAD)
    var = _head_sum(oc * oc, e, et) * (1.0 / HEAD)
    on = oc * lax.rsqrt(var + GN_EPS) * gn_g + gn_b
    return [(on + _head_sum(r * k * r_k, e, et) * v) * g]


def f_glu(rows, params, consts):
    return [rows[0] * _sigmoid(rows[1])]


def f_convln(rows, params, consts):
    x = _ln(rows[0], params[0], params[1])
    return [x * _sigmoid(x)]


def f_merge(rows, params, consts):
    gr, gc, pr, pc = rows
    return [_sigmoid(gr) * pr + _sigmoid(gc) * pc]


def f_resln(rows, params, consts):
    return [_ln(ALPHA * rows[0] + rows[1], params[0], params[1])]


def f_ln(rows, params, consts):
    return [_ln(rows[0], params[0], params[1])]


def f_attn(rows, params, consts):
    q, (k, v) = rows[0], params
    dh = q.shape[1] // XATTN_HEADS
    outs = []
    for h in range(XATTN_HEADS):
        sl = slice(h * dh, (h + 1) * dh)
        s = lax.dot_general(q[:, sl].astype(BF16), k[:, sl].astype(BF16), (((1,), (1,)), ((), ())),
                            preferred_element_type=F32) * dh ** -0.5
        p = jnp.exp(s - jnp.max(s, -1, keepdims=True))
        p = p / jnp.sum(p, -1, keepdims=True)
        outs.append(_bdot(p, v[:, sl]))
    return [jnp.concatenate(outs, axis=-1)]


def f_relu2(rows, params, consts):
    return [jnp.square(jnp.maximum(rows[0], 0.0))]


def loss_bwd(h, y, tgt, g, b, *, tm, name):
    def fn(i, n, rv, cv, sc):
        def loss(h_, y_, g_, b_):
            err = _ln(ALPHA * h_ + y_, g_, b_) - rv[2]
            return 0.5 * jnp.sum(jnp.mean(err * err, -1, keepdims=True))
        val, vjp = jax.vjp(loss, rv[0], rv[1], cv[0], cv[1])
        dh, dy, dg, db = vjp(jnp.ones((), F32))
        return [dh, dy], [val.reshape(1, 1), dg, db]

    w = h.shape[1]
    (dh, dy), (val, dg, db) = rowcall(fn, [h, y, tgt], [g, b], [(w, F32), (w, F32)],
                                      [((8, LANES), F32), (g.shape, F32), (b.shape, F32)], tm=tm, name=name)
    return val, dh, dy, dg, db


def _shift_down(cur, halo, first):
    rolled = pltpu.roll(cur, 1, 0)
    row0 = jnp.where(first, 0.0, halo[halo.shape[0] - 1:, :])
    return jnp.where(lax.broadcasted_iota(jnp.int32, cur.shape, 0) == 0, row0, rolled)


def _shift_up(cur, halo, last):
    rolled = pltpu.roll(cur, cur.shape[0] - 1, 0)
    rown = jnp.where(last, 0.0, halo[0:1, :])
    return jnp.where(lax.broadcasted_iota(jnp.int32, cur.shape, 0) == cur.shape[0] - 1, rown, rolled)


def tokenshift_fwd(zs, mus, *, tm, name):
    def fn(i, n, rv, cv, sc):
        return [z + (_shift_down(z, halo, i == 0) - z) * mu for (z, halo), mu in zip(rv, cv)], []

    return rowcall(fn, [(z, ("prev", 8)) for z in zs], mus, [(z.shape[1], F32) for z in zs], [], tm=tm, name=name)[0]


def tokenshift_bwd(dzs, zs, mus, *, tm, name):
    nz = len(zs)

    def fn(i, n, rv, cv, sc):
        outs, parts = [], []
        for (dz, dnext), (z, zprev), mu in zip(rv[:nz], rv[nz:], cv):
            g = dz * mu
            outs.append(dz - g + _shift_up(g, dnext * mu, i == n - 1))
            parts.append(jnp.sum(dz * (_shift_down(z, zprev, i == 0) - z), 0, keepdims=True))
        return outs, parts

    return rowcall(fn, [(d, ("next", 8)) for d in dzs] + [(z, ("prev", 8)) for z in zs], mus,
                   [(z.shape[1], F32) for z in zs], [(mu.shape, F32) for mu in mus], tm=tm, name=name)


def conv_fwd(u, w, b, *, tm, name):
    c = u.shape[1]

    def fn(i, n, rv, cv, sc):
        (cur, halo), (ext,) = rv[0], sc
        ext[0:CONV_HALO, :] = jnp.where(i == 0, 0.0, halo)
        ext[CONV_HALO:, :] = cur
        wv = cv[0]
        acc = jnp.broadcast_to(cv[1], cur.shape)
        for j in range(CONV_WIDTH):
            acc = acc + wv[j:j + 1, :] * ext[pl.ds(CONV_HALO - CONV_WIDTH + 1 + j, tm), :]
        return [acc], []

    return rowcall(fn, [(u, ("prev", CONV_HALO))], [w, b], [(c, F32)], [], tm=tm, name=name,
                   scratch=[pltpu.VMEM((tm + CONV_HALO, c), F32)])[0][0]


def conv_bwd(dc, u, w, *, tm, name):
    c = u.shape[1]

    def fn(i, n, rv, cv, sc):
        (dcur, dnext), (ucur, uprev) = rv
        dext, uext, dw_ref = sc
        dext[0:tm, :] = dcur
        dext[tm:, :] = jnp.where(i == n - 1, 0.0, dnext)
        uext[0:CONV_HALO, :] = jnp.where(i == 0, 0.0, uprev)
        uext[CONV_HALO:, :] = ucur
        wv = cv[0]
        du = jnp.zeros_like(dcur)
        dw_ref[...] = jnp.zeros_like(dw_ref)
        for j in range(CONV_WIDTH):
            du = du + wv[j:j + 1, :] * dext[pl.ds(CONV_WIDTH - 1 - j, tm), :]
            dw_ref[j:j + 1, :] = jnp.sum(dcur * uext[pl.ds(CONV_HALO - CONV_WIDTH + 1 + j, tm), :], 0, keepdims=True)
        return [du], [dw_ref[...], jnp.sum(dcur, 0, keepdims=True)]

    (du,), (dw, db) = rowcall(
        fn, [(dc, ("next", CONV_HALO)), (u, ("prev", CONV_HALO))], [w], [(c, F32)],
        [((CONV_HALO, c), F32), ((1, c), F32)], tm=tm, name=name,
        scratch=[pltpu.VMEM((tm + CONV_HALO, c), F32), pltpu.VMEM((tm + CONV_HALO, c), F32),
                 pltpu.VMEM((CONV_HALO, c), F32)])
    return du, dw, db


SCAN_TB = 64


def _parts2(x):
    hi = x.astype(BF16)
    return hi, (x - hi.astype(F32)).astype(BF16)


def _split3(x):
    x1 = x.astype(BF16)
    d1 = x - x1.astype(F32)
    x2 = d1.astype(BF16)
    return x1, x2, (d1 - x2.astype(F32)).astype(BF16)


def _rows3(x):
    hi, lo = _parts2(x)
    return jnp.concatenate([hi, hi, lo], axis=1)


def _bd_parts(t, left):
    def expand(u):
        zero = jnp.zeros_like(u)
        return jnp.concatenate([jnp.where(left, u, zero), jnp.where(left, zero, u)], axis=0)

    hi, lo = _parts2(t)
    return expand(hi), expand(lo)


def _w_nn(parts):
    return jnp.concatenate([parts[0], parts[1], parts[0]], axis=0)


def _w_nt(parts):
    return jnp.concatenate([parts[0], parts[1], parts[0]], axis=1)


def _nn(lhs, w):
    return jnp.dot(lhs, w, preferred_element_type=F32)


def _nt(lhs, w):
    return lax.dot_general(lhs, w, (((1,), (1,)), ((), ())), preferred_element_type=F32)


def _col_const():
    i = lax.broadcasted_iota(jnp.int32, (48, 8 * PAIR), 0) % 16
    n = lax.broadcasted_iota(jnp.int32, (48, 8 * PAIR), 1)
    return ((i % 8 == n // PAIR) & (i // 8 == n % PAIR // HEAD)).astype(BF16)


def _col_tiles(x8, col3):
    xs = jnp.concatenate([x8[:, 0:HEAD], pltpu.roll(x8, HEAD, 1)[:, 0:HEAD]], axis=0)
    return lax.dot_general(jnp.concatenate(_split3(xs), axis=0), col3, (((0,), (0,)), ((), ())),
                           preferred_element_type=F32)


def _scan_steps(pg, left, col, s_ref, refs, rows, on_state, on_out):
    r_ref, lw_ref, k_ref, v_ref, a_ref, b_ref = refs
    r8, k8, a8, b8 = r_ref[rows, :], k_ref[rows, :], a_ref[rows, :], b_ref[rows, :]
    v8, w8 = v_ref[rows, :], jnp.exp(lw_ref[rows, :])
    sub8 = lax.broadcasted_iota(jnp.int32, (8, PAIR), 0)
    ls = [slice(p * PAIR, (p + 1) * PAIR) for p in range(pg)]
    lhs = [_rows3(jnp.concatenate([r8[:, l], a8[:, l]], axis=0)) for l in ls]
    wc, bc, kc = ([_col_tiles(x8[:, l], col) for l in ls] for x8 in (w8, b8, k8))
    sts = [s_ref[p * HEAD:(p + 1) * HEAD, :] for p in range(pg)]
    sas = [_nn(lhs[p], _w_nn(_bd_parts(sts[p], left)))[8:9, :] for p in range(pg)]
    outs = [jnp.zeros((8, PAIR), F32) for _ in range(pg)]
    for j in range(8):
        tile = slice(j * PAIR, (j + 1) * PAIR)
        for p in range(pg):
            on_state(p, j, sts[p], sas[p])
            sts[p] = sts[p] * wc[p][:, tile] + bc[p][:, tile] * sas[p] + kc[p][:, tile] * v8[j:j + 1, ls[p]]
            res = _nn(lhs[p], _w_nn(_bd_parts(sts[p], left)))
            outs[p] = jnp.where(sub8 == j, res[j:j + 1, :], outs[p])
            if j < 7:
                sas[p] = res[9 + j:10 + j, :]
    for p in range(pg):
        s_ref[p * HEAD:(p + 1) * HEAD, :] = sts[p]
        on_out(p, outs[p])


def scan_fwd(r, lw, k, v, a, b, *, pg, name):
    s, c = r.shape
    tb, lw_ = SCAN_TB, PAIR * pg
    ng, nt = c // lw_, s // tb
    blk = pl.BlockSpec((tb, lw_), lambda g, t: (t, g))

    def body(r_ref, lw_ref, k_ref, v_ref, a_ref, b_ref, o_ref, ck_ref, s_ref):
        @pl.when(pl.program_id(1) == 0)
        def _():
            s_ref[...] = jnp.zeros_like(s_ref)

        ck_ref[0] = s_ref[...]
        left = lax.broadcasted_iota(jnp.int32, (HEAD, PAIR), 1) < HEAD
        col = _col_const()

        def step8(t8, carry):
            rows = pl.ds(pl.multiple_of(t8 * 8, 8), 8)

            def on_out(p, o8):
                o_ref[rows, p * PAIR:(p + 1) * PAIR] = o8

            _scan_steps(pg, left, col, s_ref, (r_ref, lw_ref, k_ref, v_ref, a_ref, b_ref), rows,
                        lambda p, j, st, sa: None, on_out)
            return carry

        lax.fori_loop(0, tb // 8, step8, 0)

    return pl.pallas_call(
        body, name=name, grid=(ng, nt), in_specs=[blk] * 6,
        out_specs=[blk, pl.BlockSpec((1, pg * HEAD, PAIR), lambda g, t: (t, g, 0))],
        out_shape=[jax.ShapeDtypeStruct((s, c), F32), jax.ShapeDtypeStruct((nt, c // 2, PAIR), F32)],
        scratch_shapes=[pltpu.VMEM((pg * HEAD, PAIR), F32)],
        compiler_params=_params(dimension_semantics=("arbitrary", "arbitrary")),
    )(r, lw, k, v, a, b)


def scan_bwd(r, lw, k, v, a, b, ck, do, dr_in, dk_in, dv_in, *, pg, name):
    s, c = r.shape
    tb, lw_ = SCAN_TB, PAIR * pg
    ng, nt = c // lw_, s // tb
    blk = pl.BlockSpec((tb, lw_), lambda g, t: (nt - 1 - t, g))
    ck_spec = pl.BlockSpec((1, pg * HEAD, PAIR), lambda g, t: (nt - 1 - t, g, 0))

    def body(r_ref, lw_ref, k_ref, v_ref, a_ref, b_ref, ck_ref, do_ref, dri_ref, dki_ref, dvi_ref,
             dr_ref, dlw_ref, dk_ref, dv_ref, da_ref, db_ref,
             s_ref, ds_ref, tall_ref, sa_ref):
        @pl.when(pl.program_id(1) == 0)
        def _():
            ds_ref[...] = jnp.zeros_like(ds_ref)

        left = lax.broadcasted_iota(jnp.int32, (HEAD, PAIR), 1) < HEAD
        sub8 = lax.broadcasted_iota(jnp.int32, (8, PAIR), 0)
        col = _col_const()
        ls = [slice(p * PAIR, (p + 1) * PAIR) for p in range(pg)]
        hs = [slice(p * HEAD, (p + 1) * HEAD) for p in range(pg)]
        s_ref[...] = ck_ref[0]

        def fwd8(t8, carry):
            base = pl.multiple_of(t8 * 8, 8)
            rows = pl.ds(base, 8)
            sa_acc = [jnp.zeros((8, PAIR), F32) for _ in range(pg)]

            def on_state(p, j, st, sa):
                tall_ref[base + j, hs[p], :] = st
                sa_acc[p] = jnp.where(sub8 == j, sa, sa_acc[p])

            _scan_steps(pg, left, col, s_ref, (r_ref, lw_ref, k_ref, v_ref, a_ref, b_ref), rows,
                        on_state, lambda p, o8: None)
            for p in range(pg):
                sa_ref[rows, ls[p]] = sa_acc[p]
            return carry

        lax.fori_loop(0, tb // 8, fwd8, 0)
        tall_ref[tb] = s_ref[...]
        ones_lhs = jnp.ones((8, PAIR), BF16)

        def bwd8(i8, carry):
            base = pl.multiple_of((tb // 8 - 1 - i8) * 8, 8)
            rows = pl.ds(base, 8)
            r8, k8, a8, b8 = r_ref[rows, :], k_ref[rows, :], a_ref[rows, :], b_ref[rows, :]
            v8, do8, w8, sa8 = v_ref[rows, :], do_ref[rows, :], jnp.exp(lw_ref[rows, :]), sa_ref[rows, :]
            lhs_kb = [_rows3(jnp.concatenate([k8[:, l], b8[:, l]], axis=0)) for l in ls]
            lhs_vs = [jnp.concatenate([v8[:, l], sa8[:, l]], axis=0).astype(BF16) for l in ls]
            lhs_do = [do8[:, l].astype(BF16) for l in ls]
            rc, wc, ac = ([_col_tiles(x8[:, l], col) for l in ls] for x8 in (r8, w8, a8))
            dss = [ds_ref[h, :] for h in hs]
            t_post = [_bd_parts(tall_ref[base + 8, h, :], left)[0] for h in hs]
            acc = [[jnp.zeros((8, PAIR), F32) for _ in range(6)] for _ in range(pg)]
            for j in reversed(range(8)):
                tile = slice(j * PAIR, (j + 1) * PAIR)
                res_nn, res_nt, res_r, res_w, t_prev = [], [], [], [], []
                for p in range(pg):
                    t_prev_f = tall_ref[base + j, hs[p], :]
                    t_prev.append(_bd_parts(t_prev_f, left)[0])
                    dss[p] = dss[p] + rc[p][:, tile] * do8[j:j + 1, ls[p]]
                    dd = _bd_parts(dss[p], left)
                    res_nn.append(_nn(lhs_kb[p], _w_nn(dd)))
                    res_nt.append(_nt(lhs_vs[p], dd[0]))
                    res_r.append(_nt(lhs_do[p], t_post[p]))
                    res_w.append(_nt(ones_lhs, _bd_parts(dss[p] * t_prev_f, left)[0]))
                for p in range(pg):
                    dsa = res_nn[p][8 + j:9 + j, :]
                    new = [res_r[p][j:j + 1, :],
                           res_w[p][0:1, :],
                           res_nt[p][j:j + 1, :],
                           res_nn[p][j:j + 1, :],
                           _nt(jnp.broadcast_to(dsa, (8, PAIR)).astype(BF16), t_prev[p])[0:1, :],
                           res_nt[p][8 + j:9 + j, :]]
                    acc[p] = [jnp.where(sub8 == j, n_, a_) for n_, a_ in zip(new, acc[p])]
                    dss[p] = dss[p] * wc[p][:, tile] + ac[p][:, tile] * dsa
                    t_post[p] = t_prev[p]
            for p in range(pg):
                ds_ref[hs[p], :] = dss[p]
                dr_ref[rows, ls[p]] = acc[p][0] + dri_ref[rows, ls[p]]
                dlw_ref[rows, ls[p]] = acc[p][1] * w8[:, ls[p]]
                dk_ref[rows, ls[p]] = acc[p][2] + dki_ref[rows, ls[p]]
                dv_ref[rows, ls[p]] = acc[p][3] + dvi_ref[rows, ls[p]]
                da_ref[rows, ls[p]] = acc[p][4]
                db_ref[rows, ls[p]] = acc[p][5]
            return carry

        lax.fori_loop(0, tb // 8, bwd8, 0)

    return pl.pallas_call(
        body, name=name, grid=(ng, nt), in_specs=[blk] * 6 + [ck_spec] + [blk] * 4, out_specs=[blk] * 6,
        out_shape=[jax.ShapeDtypeStruct((s, c), F32)] * 6,
        scratch_shapes=[pltpu.VMEM((pg * HEAD, PAIR), F32), pltpu.VMEM((pg * HEAD, PAIR), F32),
                        pltpu.VMEM((tb + 1, pg * HEAD, PAIR), F32), pltpu.VMEM((tb, pg * PAIR), F32)],
        compiler_params=_params(dimension_semantics=("arbitrary", "arbitrary")),
    )(r, lw, k, v, a, b, ck, do, dr_in, dk_in, dv_in)


def _place():
    x, y, c = lax.axis_index("x"), lax.axis_index("y"), lax.axis_index("c")
    return x, y, c, [(1 - x, y), (x, 1 - y), (1 - x, 1 - y)]


_ANY = pl.BlockSpec(memory_space=pl.ANY)


def into_slot(w, dtype, *, name):
    r, c = w.shape
    tr = _tile(r, max(8, (1 << 20) // (c * 4) // 16 * 16), unit=16) if r % 16 == 0 else r
    j_arr = (2 * lax.axis_index("x") + lax.axis_index("y")).astype(jnp.int32).reshape(1)

    def body(j_ref, w_ref, o_ref):
        o_ref[...] = w_ref[...].astype(o_ref.dtype)

    return pl.pallas_call(
        body, name=name,
        grid_spec=pltpu.PrefetchScalarGridSpec(
            num_scalar_prefetch=1, grid=(r // tr,),
            in_specs=[pl.BlockSpec((tr, c), lambda i, j_ref: (i, 0))],
            out_specs=pl.BlockSpec((None, tr, c), lambda i, j_ref: (j_ref[0], i, 0))),
        out_shape=jax.ShapeDtypeStruct((4, r, c), dtype),
        compiler_params=_params(dimension_semantics=("arbitrary",)),
    )(j_arr, w)


def allgather_multi(bufs, *, name):
    nb = len(bufs)

    def body(*refs):
        ins, (send_sems, recv_sems) = refs[:nb], refs[2 * nb:]
        x, y, c, chips = _place()
        sib = (x, y, 1 - c)

        def half(i, jj, cc):
            rh = bufs[i].shape[1] // 2
            return ins[i].at[jj, pl.ds(cc * rh, rh), :]

        def copy(i, kk, jj, cc, to):
            return pltpu.make_async_remote_copy(src_ref=half(i, jj, cc), dst_ref=half(i, jj, cc),
                                                send_sem=send_sems.at[6 * i + kk], recv_sem=recv_sems.at[6 * i + kk],
                                                device_id=to, device_id_type=MESH)

        first = [copy(i, kk, 2 * x + y, c, (cx, cy, c)) for i in range(nb) for kk, (cx, cy) in enumerate(chips)]
        for cp in first:
            cp.start()
        passed = []
        for i in range(nb):
            for kk, (cx, cy) in enumerate(chips):
                copy(i, kk, 2 * cx + cy, c, sib).wait_recv()
                passed.append(copy(i, 3 + kk, 2 * cx + cy, c, sib))
                passed[-1].start()
        for i in range(nb):
            for kk, (cx, cy) in enumerate(chips):
                copy(i, 3 + kk, 2 * cx + cy, 1 - c, sib).wait_recv()
        for cp in first + passed:
            cp.wait_send()

    return pl.pallas_call(
        body, name=name, in_specs=[_ANY] * nb, out_specs=[_ANY] * nb,
        out_shape=[jax.ShapeDtypeStruct(b.shape, b.dtype) for b in bufs],
        input_output_aliases={i: i for i in range(nb)},
        scratch_shapes=[pltpu.SemaphoreType.DMA((6 * nb,)), pltpu.SemaphoreType.DMA((6 * nb,))],
    )(*bufs)


def sibling_swap_multi(gs, *, name):
    nb = len(gs)

    def body(*refs):
        ins, outs, (send_sems, recv_sems) = refs[:nb], refs[nb:2 * nb], refs[2 * nb:]
        x, y, c, _ = _place()
        cps = []
        for i in range(nb):
            rh = gs[i].shape[1] // 2
            cps.append(pltpu.make_async_remote_copy(
                src_ref=ins[i].at[:, pl.ds((1 - c) * rh, rh), :], dst_ref=outs[i], send_sem=send_sems.at[i],
                recv_sem=recv_sems.at[i], device_id=(x, y, 1 - c), device_id_type=MESH))
        for cp in cps:
            cp.start()
        for cp in cps:
            cp.wait()

    return pl.pallas_call(
        body, name=name, in_specs=[_ANY] * nb, out_specs=[_ANY] * nb,
        out_shape=[jax.ShapeDtypeStruct((4, g.shape[1] // 2, g.shape[2]), g.dtype) for g in gs],
        scratch_shapes=[pltpu.SemaphoreType.DMA((nb,)), pltpu.SemaphoreType.DMA((nb,))],
    )(*gs)


def pair_sum(g, got, *, tr, name):
    _, rh, w = got.shape
    nb = rh // tr
    c_arr = lax.axis_index("c").astype(jnp.int32).reshape(1)

    def body(c_ref, g_ref, got_ref, o_ref):
        o_ref[...] = (g_ref[...].astype(F32) + got_ref[...].astype(F32)).astype(o_ref.dtype)

    return pl.pallas_call(
        body, name=name,
        grid_spec=pltpu.PrefetchScalarGridSpec(
            num_scalar_prefetch=1, grid=(4, nb),
            in_specs=[pl.BlockSpec((1, tr, w), lambda j, i, c_ref: (j, c_ref[0] * nb + i, 0)),
                      pl.BlockSpec((1, tr, w), lambda j, i, c_ref: (j, i, 0))],
            out_specs=pl.BlockSpec((1, tr, w), lambda j, i, c_ref: (j, i, 0))),
        out_shape=jax.ShapeDtypeStruct((4, rh, w), got.dtype),
        compiler_params=_params(dimension_semantics=("arbitrary", "arbitrary")),
    )(c_arr, g, got)


def scatter_multi(pss, *, name):
    nb = len(pss)

    def body(*refs):
        ins, outs, (send_sems, recv_sems) = refs[:nb], refs[nb:2 * nb], refs[2 * nb:]
        x, y, c, chips = _place()
        cps = [pltpu.make_async_remote_copy(src_ref=ins[i].at[2 * cx + cy], dst_ref=outs[i].at[kk],
                                            send_sem=send_sems.at[3 * i + kk], recv_sem=recv_sems.at[3 * i + kk],
                                            device_id=(cx, cy, c), device_id_type=MESH)
               for i in range(nb) for kk, (cx, cy) in enumerate(chips)]
        for cp in cps:
            cp.start()
        for cp in cps:
            cp.wait()

    return pl.pallas_call(
        body, name=name, in_specs=[_ANY] * nb, out_specs=[_ANY] * nb,
        out_shape=[jax.ShapeDtypeStruct((3,) + p.shape[1:], p.dtype) for p in pss],
        scratch_shapes=[pltpu.SemaphoreType.DMA((3 * nb,)), pltpu.SemaphoreType.DMA((3 * nb,))],
    )(*pss)


def chip_sum(ps, got, *, tr, name):
    _, rh, w = ps.shape
    nb = rh // tr
    jc_arr = jnp.stack([2 * lax.axis_index("x") + lax.axis_index("y"), lax.axis_index("c")]).astype(jnp.int32)

    def body(jc_ref, ps_ref, got_ref, o_ref):
        acc = ps_ref[0].astype(F32)
        for kk in range(3):
            acc = acc + got_ref[kk].astype(F32)
        o_ref[...] = acc

    return pl.pallas_call(
        body, name=name,
        grid_spec=pltpu.PrefetchScalarGridSpec(
            num_scalar_prefetch=1, grid=(nb,),
            in_specs=[pl.BlockSpec((1, tr, w), lambda i, jc: (jc[0], i, 0)),
                      pl.BlockSpec((3, tr, w), lambda i, jc: (0, i, 0))],
            out_specs=pl.BlockSpec((tr, w), lambda i, jc: (jc[1] * nb + i, 0))),
        out_shape=jax.ShapeDtypeStruct((2 * rh, w), F32),
        compiler_params=_params(dimension_semantics=("arbitrary",)),
    )(jc_arr, ps, got)


def sibling_join_multi(reds, *, name):
    nb = len(reds)

    def body(*refs):
        ins, (send_sems, recv_sems) = refs[:nb], refs[2 * nb:]
        x, y, c, _ = _place()

        def copy(i, cc):
            rh = reds[i].shape[0] // 2
            rows = ins[i].at[pl.ds(cc * rh, rh), :]
            return pltpu.make_async_remote_copy(src_ref=rows, dst_ref=rows, send_sem=send_sems.at[i],
                                                recv_sem=recv_sems.at[i], device_id=(x, y, 1 - c), device_id_type=MESH)

        cps = [copy(i, c) for i in range(nb)]
        for cp in cps:
            cp.start()
        for i, cp in enumerate(cps):
            cp.wait_send()
            copy(i, 1 - c).wait_recv()

    return pl.pallas_call(
        body, name=name, in_specs=[_ANY] * nb, out_specs=[_ANY] * nb,
        out_shape=[jax.ShapeDtypeStruct(r.shape, r.dtype) for r in reds],
        input_output_aliases={i: i for i in range(nb)},
        scratch_shapes=[pltpu.SemaphoreType.DMA((nb,)), pltpu.SemaphoreType.DMA((nb,))],
    )(*reds)


def allreduce_small(part, *, name):
    m_per, n = part.shape

    def body(x_ref, sum_ref, all_ref, send_sems, recv_sems, local_sem):
        x, y, c, chips = _place()
        me, sib = (x, y, c), (x, y, 1 - c)

        def rows(px, py, pc):
            return all_ref.at[pl.ds((4 * px + 2 * py + pc) * m_per, m_per), :]

        def copy(kk, block, to, src=None):
            return pltpu.make_async_remote_copy(src_ref=rows(*block) if src is None else src, dst_ref=rows(*block),
                                                send_sem=send_sems.at[kk], recv_sem=recv_sems.at[kk],
                                                device_id=to, device_id_type=MESH)

        mine = pltpu.make_async_copy(x_ref, rows(*me), local_sem)
        mine.start()
        first = [copy(0, me, sib, src=x_ref)]
        first += [copy(1 + kk, me, (*chip, c), src=x_ref) for kk, chip in enumerate(chips)]
        for cp in first:
            cp.start()
        passed = [copy(4 + kk, (*chip, c), sib) for kk, chip in enumerate(chips)]
        for kk, chip in enumerate(chips):
            copy(1 + kk, (*chip, c), me).wait_recv()
            passed[kk].start()
        copy(0, sib, me).wait_recv()
        for kk, chip in enumerate(chips):
            copy(4 + kk, (*chip, 1 - c), me).wait_recv()
        for cp in first + passed:
            cp.wait_send()
        mine.wait()
        acc = all_ref[0:m_per, :]
        for d in range(1, 8):
            acc = acc + all_ref[d * m_per:(d + 1) * m_per, :]
        sum_ref[...] = acc

    vmem = pl.BlockSpec(memory_space=pltpu.VMEM)
    return pl.pallas_call(
        body, name=name, in_specs=[vmem], out_specs=vmem,
        out_shape=jax.ShapeDtypeStruct((m_per, n), part.dtype),
        scratch_shapes=[pltpu.VMEM((8 * m_per, n), part.dtype), pltpu.SemaphoreType.DMA((7,)),
                        pltpu.SemaphoreType.DMA((7,)), pltpu.SemaphoreType.DMA],
    )(part)


def adamw(w, g, m, v, *, name):
    r, c = w.shape
    tm = r if r * c * 4 <= (1 << 20) else _tile(r, max(8, ((1 << 20) // (c * 4)) // 8 * 8), unit=8)
    bc1, bc2 = 1.0 - ADAM_B1 ** ADAM_STEP, 1.0 - ADAM_B2 ** ADAM_STEP

    def fn(i, n, rv, cv, sc):
        w_, g_, m_, v_ = rv
        m_ = ADAM_B1 * m_ + (1.0 - ADAM_B1) * g_
        v_ = ADAM_B2 * v_ + (1.0 - ADAM_B2) * (g_ * g_)
        delta = -ADAM_LR * ((m_ / bc1) / (jnp.sqrt(v_ / bc2) + ADAM_EPS) + ADAM_WD * w_)
        return [delta, m_, v_], []

    return rowcall(fn, [w, g, m, v], [], [(c, F32)] * 3, [], tm=tm, name=name)[0]


def _head_one_hot(c):
    e = (lax.broadcasted_iota(jnp.int32, (c, LANES), 0) // HEAD
         == lax.broadcasted_iota(jnp.int32, (c, LANES), 1)).astype(F32)
    return e, e.T


def _join_cols(g):
    return jnp.concatenate([g[j] for j in range(4)], axis=1)


def _split_cols(a):
    return jnp.stack(jnp.split(a, 4, axis=1))


def layer_step(x, mem, tgt, wg, sp):
    s, d = x.shape
    dr, dc = sp["rwkv_w0"].shape[1], sp["conv_b"].shape[1]
    n_lora = N_DECAY + N_ICLR + N_GATE
    n_rwkv = 3 * dr + n_lora
    pad_l = N_LORA_PAD - n_lora
    w_in = _join_cols(wg["w_in"])
    cuts = [0, dr, 2 * dr, 3 * dr, n_rwkv, n_rwkv + dc, n_rwkv + 2 * dc, n_rwkv + 2 * dc + d, n_rwkv + 2 * dc + 2 * d]
    w_r, w_k, w_v, w_l, w_ca, w_cb, w_gr, w_gc = (w_in[:, lo:hi] for lo, hi in zip(cuts[:-1], cuts[1:]))
    w_l = jnp.pad(w_l, ((0, 0), (0, pad_l)))
    sm = sp["rwkv_shift_mix"]
    mus = [sm[:, 0:dr], sm[:, dr:2 * dr], sm[:, 2 * dr:3 * dr], jnp.pad(sm[:, 3 * dr:], ((0, 0), (0, pad_l)))]
    w_up = jnp.pad(_join_cols(wg["rwkv_w_up"]).astype(F32), ((0, LANES - N_DECAY), (0, 0)))
    a_up = jnp.pad(_join_cols(wg["rwkv_a_up"]).astype(F32), ((N_DECAY, 2 * LANES - N_DECAY - N_ICLR), (0, 0)))
    g_lo = N_DECAY + N_ICLR - LANES
    g_up = jnp.pad(_join_cols(wg["rwkv_g_up"]).astype(F32), ((g_lo, pad_l), (0, 0)))
    conv_w = _join_cols(wg["conv_w"])
    wt = {n: wg[n].reshape(-1, wg[n].shape[2]) for n in ("w_out", "xattn_wq", "xattn_wk", "xattn_wv", "xattn_wo", "mlp_w2")}
    e, et = _head_one_hot(dr)
    pre_p = [sp["rwkv_w0"], w_up, sp["rwkv_a0"], a_up, g_up, sp["rwkv_k_k"], sp["rwkv_k_a"]]
    post_p = [sp["rwkv_r_k"], sp["rwkv_gn_g"], sp["rwkv_gn_b"]]
    pairs = dr // PAIR
    tm = min(s, 128)
    tmm = mem.shape[0]

    z_r, z_k, z_v, z_l = (mm(x, w, name=f"z_{n}") for n, w in zip("rkvl", (w_r, w_k, w_v, w_l)))
    z_ca, z_cb = mm(x, w_ca, name="z_ca"), mm(x, w_cb, name="z_cb")
    z_gr, z_gc = mm(x, w_gr, name="z_gr"), mm(x, w_gc, name="z_gc")
    zs_r, zs_k, zs_v, zs_l = tokenshift_fwd([z_r, z_k, z_v, z_l], mus, tm=tm, name="shift_fwd")
    pre_o = [(dr, F32)] * 5
    k_m, lw, a_s, b_s, g = stage_fwd(f_rwkv_pre, [zs_k, zs_l], pre_p, [e, et], pre_o, tm=min(s, 64), name="pre_fwd")
    o, ck = scan_fwd(zs_r, lw, k_m, zs_v, a_s, b_s, pg=min(8, pairs), name="scan_fwd")
    post_r = [o, zs_r, k_m, zs_v, g]
    (o_r,) = stage_fwd(f_rwkv_post, post_r, post_p, [e, et], [(dr, BF16)], tm=min(s, 64), name="post_fwd")
    (u,) = stage_fwd(f_glu, [z_ca, z_cb], [], [], [(dc, F32)], tm=tm, name="glu_fwd")
    cv = conv_fwd(u, conv_w, sp["conv_b"], tm=tm, name="conv_fwd")
    cln_p = [sp["conv_ln_g"], sp["conv_ln_b"]]
    (o_c,) = stage_fwd(f_convln, [cv], cln_p, [], [(dc, BF16)], tm=tm, name="convln_fwd")
    p_r = mm(o_r, wg["proj_rwkv"], b_sh=True, name="proj_r")
    p_c = mm(o_c, wg["proj_conv"], b_sh=True, name="proj_c")
    (merged,) = stage_fwd(f_merge, [z_gr, z_gc, p_r, p_c], [], [], [(d, BF16)], tm=tm, name="merge_fwd")
    y1 = mm(merged, wt["w_out"], name="y1")
    ln1_p, ln2_p, lnm_p = ([sp[f"{n}_g"], sp[f"{n}_b"]] for n in ("ln1", "ln2", "ln_mem"))
    (h1,) = stage_fwd(f_resln, [x, y1], ln1_p, [], [(d, F32)], tm=tm, name="ln1_fwd")
    (mem_n,) = stage_fwd(f_ln, [mem], lnm_p, [], [(d, F32)], tm=tmm, name="lnmem_fwd")
    k_mem, v_mem = mm(mem_n, wt["xattn_wk"], name="k_mem"), mm(mem_n, wt["xattn_wv"], name="v_mem")
    q = mm(h1, wt["xattn_wq"], name="q")
    (ao,) = stage_fwd(f_attn, [q], [k_mem, v_mem], [], [(d, BF16)], tm=tm, name="attn_fwd")
    ca = mm(ao, wt["xattn_wo"], name="ca")
    (h2,) = stage_fwd(f_resln, [h1, ca], ln2_p, [], [(d, F32)], tm=tm, name="ln2_fwd")
    u1 = mm(h2, wg["mlp_w1"], b_sh=True, name="u1")
    f_dim = u1.shape[1]
    tmf = min(s, 64)
    (act,) = stage_fwd(f_relu2, [u1], [], [], [(f_dim, BF16)], tm=tmf, name="relu2_fwd")
    ff = mm(act, wt["mlp_w2"], name="ff")

    gw, gs = {}, {}
    loss, dh2, dff, gs["ln3_g"], gs["ln3_b"] = loss_bwd(h2, ff, tgt, sp["ln3_g"], sp["ln3_b"], tm=tm, name="loss_bwd")
    gw["mlp_w2"] = mm(act, dff, ta=True, out_dtype=BF16, name="g_mlp_w2")
    dact = mm(dff, wt["mlp_w2"], tb=True, name="d_act")
    (du1,), _ = stage_bwd(f_relu2, [u1], [], [], [dact], [(0, BF16)], tm=tmf, name="relu2_bwd")
    gw["mlp_w1"] = mm(h2, du1, ta=True, out_dtype=BF16, out_sh=True, name="g_mlp_w1")
    dh2 = mm(du1, wg["mlp_w1"], tb=True, b_sh=True, acc=dh2, name="d_h2")
    (dh1, dca), (gs["ln2_g"], gs["ln2_b"]) = stage_bwd(f_resln, [h1, ca], ln2_p, [], [dh2], [(0, F32), (1, F32)],
                                                       tm=tm, name="ln2_bwd")
    gw["xattn_wo"] = mm(ao, dca, ta=True, out_dtype=BF16, name="g_wo")
    dao = mm(dca, wt["xattn_wo"], tb=True, name="d_ao")
    (dq,), (dk_mem, dv_mem) = stage_bwd(f_attn, [q], [k_mem, v_mem], [], [dao], [(0, F32)], tm=tm, name="attn_bwd")
    gw["xattn_wq"] = mm(h1, dq, ta=True, out_dtype=BF16, name="g_wq")
    dh1 = mm(dq, wt["xattn_wq"], tb=True, acc=dh1, name="d_h1")
    gw["xattn_wk"] = mm(mem_n, dk_mem, ta=True, out_dtype=BF16, name="g_wk")
    gw["xattn_wv"] = mm(mem_n, dv_mem, ta=True, out_dtype=BF16, name="g_wv")
    dmem_n = mm(dk_mem, wt["xattn_wk"], tb=True, name="d_memn_k")
    dmem_n = mm(dv_mem, wt["xattn_wv"], tb=True, acc=dmem_n, name="d_memn_v")
    _, (gs["ln_mem_g"], gs["ln_mem_b"]) = stage_bwd(f_ln, [mem], lnm_p, [], [dmem_n], [], tm=tmm, name="lnmem_bwd")
    (dx, dy1), (gs["ln1_g"], gs["ln1_b"]) = stage_bwd(f_resln, [x, y1], ln1_p, [], [dh1], [(0, F32), (1, F32)],
                                                      tm=tm, name="ln1_bwd")
    gw["w_out"] = mm(merged, dy1, ta=True, out_dtype=BF16, name="g_w_out")
    dmerged = mm(dy1, wt["w_out"], tb=True, name="d_merged")
    (dz_gr, dz_gc, dp_r, dp_c), _ = stage_bwd(f_merge, [z_gr, z_gc, p_r, p_c], [], [], [dmerged],
                                              [(0, BF16), (1, BF16), (2, F32), (3, F32)], tm=tm, name="merge_bwd")
    gw["proj_rwkv"] = mm(o_r, dp_r, ta=True, out_dtype=BF16, out_sh=True, name="g_proj_r")
    gw["proj_conv"] = mm(o_c, dp_c, ta=True, out_dtype=BF16, out_sh=True, name="g_proj_c")
    do_r = mm(dp_r, wg["proj_rwkv"], tb=True, b_sh=True, name="d_o_r")
    do_c = mm(dp_c, wg["proj_conv"], tb=True, b_sh=True, name="d_o_c")
    (dcv,), (gs["conv_ln_g"], gs["conv_ln_b"]) = stage_bwd(f_convln, [cv], cln_p, [], [do_c], [(0, F32)],
                                                           tm=tm, name="convln_bwd")
    du, g_conv_w, gs["conv_b"] = conv_bwd(dcv, u, conv_w, tm=tm, name="conv_bwd")
    gw["conv_w"] = _split_cols(g_conv_w.astype(BF16))
    (dz_ca, dz_cb), _ = stage_bwd(f_glu, [z_ca, z_cb], [], [], [du], [(0, BF16), (1, BF16)], tm=tm, name="glu_bwd")
    (d_o, dr_p, dk_p, dv_p, dg), (gs["rwkv_r_k"], gs["rwkv_gn_g"], gs["rwkv_gn_b"]) = stage_bwd(
        f_rwkv_post, post_r, post_p, [e, et], [do_r], [(k, F32) for k in range(5)], tm=min(s, 64), name="post_bwd")
    dzs_r, dlw, dk_m, dzs_v, da_s, db_s = scan_bwd(zs_r, lw, k_m, zs_v, a_s, b_s, ck, d_o, dr_p, dk_p, dv_p,
                                                   pg=min(8, pairs), name="scan_bwd")
    (dzs_k, dzs_l), pre_g = stage_bwd(f_rwkv_pre, [zs_k, zs_l], pre_p, [e, et], [dk_m, dlw, da_s, db_s, dg],
                                      [(0, F32), (1, F32)], tm=min(s, 64), name="pre_bwd")
    gs["rwkv_w0"], g_w_up, gs["rwkv_a0"], g_a_up, g_g_up, gs["rwkv_k_k"], gs["rwkv_k_a"] = pre_g
    gw["rwkv_w_up"] = _split_cols(g_w_up[0:N_DECAY].astype(BF16))
    gw["rwkv_a_up"] = _split_cols(g_a_up[N_DECAY:N_DECAY + N_ICLR].astype(BF16))
    gw["rwkv_g_up"] = _split_cols(g_g_up[g_lo:g_lo + N_GATE].astype(BF16))
    dzs, dmus = tokenshift_bwd([dzs_r, dzs_k, dzs_v, dzs_l], [z_r, z_k, z_v, z_l], mus, tm=tm, name="shift_bwd")
    gs["rwkv_shift_mix"] = jnp.concatenate(list(dmus[:3]) + [dmus[3][:, 0:n_lora]], axis=1)
    dzs = list(dzs) + [dz_ca, dz_cb, dz_gr, dz_gc]
    g_in = []
    for n, dz, w in zip(("r", "k", "v", "l", "ca", "cb", "gr", "gc"), dzs, (w_r, w_k, w_v, w_l, w_ca, w_cb, w_gr, w_gc)):
        g_in.append(mm(x, dz, ta=True, out_dtype=BF16, name=f"g_w_{n}"))
        dx = mm(dz, w, tb=True, acc=dx, name=f"d_x_{n}")
    g_in[3] = g_in[3][:, 0:n_lora]
    gw["w_in"] = _split_cols(jnp.concatenate(g_in, axis=1))
    for n in wt:
        gw[n] = gw[n].reshape(wg[n].shape)
    return loss, dx, gw, gs


WEIGHTS = ["w_in", "rwkv_shift_mix", "rwkv_w0", "rwkv_w_up", "rwkv_a0", "rwkv_a_up", "rwkv_g_up", "rwkv_k_k",
           "rwkv_k_a", "rwkv_r_k", "rwkv_gn_g", "rwkv_gn_b", "conv_w", "conv_b", "conv_ln_g", "conv_ln_b",
           "proj_rwkv", "proj_conv", "w_out", "ln1_g", "ln1_b", "ln_mem_g", "ln_mem_b", "xattn_wq", "xattn_wk",
           "xattn_wv", "xattn_wo", "ln2_g", "ln2_b", "mlp_w1", "mlp_w2", "ln3_g", "ln3_b"]
SHARD_AXIS = {"w_in": 1, "rwkv_w_up": 1, "rwkv_a_up": 1, "rwkv_g_up": 1, "conv_w": 1, "proj_rwkv": 1, "proj_conv": 1,
              "w_out": 0, "xattn_wq": 0, "xattn_wk": 0, "xattn_wv": 0, "xattn_wo": 0, "mlp_w1": 1, "mlp_w2": 0}


def _unpack(flat, shapes):
    out, off = [], 0
    for shp in shapes:
        n = 1
        for dim in shp:
            n *= dim
        out.append(flat[..., off:off + n].reshape(flat.shape[:-1] + tuple(shp)))
        off += n
    return out


def _half_tile(rh, w):
    return _tile(rh, max(16, (2 << 20) // (w * 4) // 16 * 16), unit=16)


def kernel(*args):
    n_w = len(WEIGHTS)
    x, mem = args[0][0], args[1][0]
    tgt = args[2 + n_w][0]
    w_loc = {n: a for n, a in zip(WEIGHTS, args[2:2 + n_w])}
    m_loc = {n: a for n, a in zip(WEIGHTS, args[3 + n_w:3 + 2 * n_w])}
    v_loc = {n: a for n, a in zip(WEIGHTS, args[3 + 2 * n_w:3 + 3 * n_w])}
    big = [n for n in WEIGHTS if n in SHARD_AXIS]
    small = [n for n in WEIGHTS if n not in SHARD_AXIS]

    def as2d(n, a):
        if n in SHARD_AXIS:
            return a.reshape(a.shape[1], a.shape[-1])
        return a.reshape(1, -1)

    loc2d = {n: as2d(n, w_loc[n]) for n in WEIGHTS}
    conv_rows = loc2d["conv_w"].shape[0]
    slots = []
    for n in big:
        if n == "conv_w":
            padded = jnp.pad(loc2d[n], ((0, CONV_HALO - conv_rows), (0, 0)))
            slots.append(into_slot(padded, F32, name=f"slot_{n}"))
        else:
            slots.append(into_slot(loc2d[n], BF16, name=f"slot_{n}"))
    wg = dict(zip(big, allgather_multi(slots, name="allgather_weights")))
    sp = {n: loc2d[n] for n in small}

    loss_part, grad_x, gw, gs = layer_step(x, mem, tgt, wg, sp)

    g_list = [gw[n] for n in big]
    tiles = [_half_tile(g.shape[1] // 2, g.shape[2]) for g in g_list]
    gots = sibling_swap_multi(g_list, name="rs_sibling_swap")
    pss = [pair_sum(g, got, tr=tr, name=f"rs_pair_sum_{n}") for n, g, got, tr in zip(big, g_list, gots, tiles)]
    gots = scatter_multi(pss, name="rs_scatter")
    reds = [chip_sum(ps, got, tr=tr, name=f"rs_chip_sum_{n}") for n, ps, got, tr in zip(big, pss, gots, tiles)]
    reds = sibling_join_multi(reds, name="rs_sibling_join")
    g_big = {n: (r[0:conv_rows] if n == "conv_w" else r) for n, r in zip(big, reds)}

    small_parts = [gs[n] for n in small] + [loss_part[0:1, 0:1]]
    flat = jnp.concatenate([p.reshape(-1).astype(F32) for p in small_parts])
    flat = jnp.pad(flat, (0, -flat.shape[0] % (8 * LANES))).reshape(-1, LANES)
    red = allreduce_small(flat, name="allreduce_small").reshape(-1)
    g_small = dict(zip(small, _unpack(red, [loc2d[n].shape for n in small])))
    n_small = sum(loc2d[n].shape[1] for n in small)
    loss = red[n_small]

    grads, deltas, new_m, new_v = {}, {}, {}, {}
    for n in big:
        g2 = g_big[n]
        d2, m2, v2 = adamw(loc2d[n], g2, as2d(n, m_loc[n]), as2d(n, v_loc[n]), name=f"adamw_{n}")
        shp = w_loc[n].shape
        grads[n], deltas[n], new_m[n], new_v[n] = (t.reshape(shp) for t in (g2, d2, m2, v2))

    def small_pack(d):
        f = jnp.concatenate([as2d(n, d[n]).reshape(-1) for n in small])
        return jnp.pad(f, (0, -f.shape[0] % (8 * LANES))).reshape(-1, LANES)

    g_pack = small_pack({n: g_small[n] for n in small})
    outs = adamw(small_pack(w_loc), g_pack, small_pack(m_loc), small_pack(v_loc), name="adamw_small")
    for dst, packed in zip((deltas, new_m, new_v), outs):
        for n, t in zip(small, _unpack(packed.reshape(-1), [loc2d[n].shape for n in small])):
            dst[n] = t.reshape(w_loc[n].shape)
    for n in small:
        grads[n] = g_small[n].reshape(w_loc[n].shape)

    return (loss, grad_x[None], *[grads[n] for n in WEIGHTS], *[deltas[n] for n in WEIGHTS],
            *[new_m[n] for n in WEIGHTS], *[new_v[n] for n in WEIGHTS])
```

```python
import functools

import jax
import jax.numpy as jnp
from jax import lax
from jax.experimental import pallas as pl
from jax.experimental.pallas import tpu as pltpu

F32, BF16 = jnp.float32, jnp.bfloat16
ALPHA = 2.0 ** 0.25
LN_EPS = 1e-5
GN_EPS = 64e-5
HEAD = 64
LANES = 128
PAIR = 2 * HEAD
XATTN_HEADS = 4
CONV_WIDTH = 31
CONV_HALO = 32
N_DECAY, N_ICLR, N_GATE = 96, 96, 256
N_LORA_PAD = 512
VMEM_LIMIT = 56 * 1024 * 1024
ADAM_LR, ADAM_B1, ADAM_B2, ADAM_EPS, ADAM_WD, ADAM_STEP = 0.001, 0.9, 0.999, 1e-8, 0.01, 10
MESH = pl.DeviceIdType.MESH
HI = lax.Precision.HIGHEST


def _params(**kw):
    return pltpu.CompilerParams(vmem_limit_bytes=VMEM_LIMIT, **kw)


def _tile(n, pref, unit=LANES):
    if n <= pref:
        return n
    t = pref
    while n % t:
        t -= unit
    return t


def mm(a, b, *, name, ta=False, tb=False, out_dtype=F32, acc=None, b_sh=False, out_sh=False):
    m, k = (a.shape[1], a.shape[0]) if ta else a.shape
    if b_sh:
        n = b.shape[1] if tb else 4 * b.shape[2]
    else:
        n = b.shape[0] if tb else b.shape[1]
    tm, tn, tk = _tile(m, 1024), _tile(n // 4 if (b_sh and not tb) or out_sh else n, 1024), _tile(k // 4 if b_sh and tb else k, 512)
    nk = k // tk
    a_spec = pl.BlockSpec((tk, tm), lambda i, j, kk: (kk, i)) if ta else pl.BlockSpec((tm, tk), lambda i, j, kk: (i, kk))
    if b_sh and tb:
        per_k = k // 4 // tk
        b_spec = pl.BlockSpec((None, tn, tk), lambda i, j, kk: (kk // per_k, j, kk % per_k))
    elif b_sh:
        per_n = n // 4 // tn
        b_spec = pl.BlockSpec((None, tk, tn), lambda i, j, kk: (j // per_n, kk, j % per_n))
    else:
        b_spec = pl.BlockSpec((tn, tk), lambda i, j, kk: (j, kk)) if tb else pl.BlockSpec((tk, tn), lambda i, j, kk: (kk, j))
    if out_sh:
        per_o = n // 4 // tn
        o_spec = pl.BlockSpec((None, tm, tn), lambda i, j, kk: (j // per_o, i, j % per_o))
    else:
        o_spec = pl.BlockSpec((tm, tn), lambda i, j, kk: (i, j))
    dims = (((0 if ta else 1,), (1 if tb else 0,)), ((), ()))

    def body(*refs):
        if acc is None:
            a_ref, b_ref, o_ref, acc_ref = refs
        else:
            a_ref, b_ref, c_ref, o_ref, acc_ref = refs
        kk = pl.program_id(2)

        @pl.when(kk == 0)
        def _():
            acc_ref[...] = jnp.zeros_like(acc_ref) if acc is None else c_ref[...].astype(F32)

        acc_ref[...] += lax.dot_general(a_ref[...].astype(BF16), b_ref[...].astype(BF16), dims,
                                        preferred_element_type=F32)

        @pl.when(kk == nk - 1)
        def _():
            o_ref[...] = acc_ref[...].astype(o_ref.dtype)

    ins = [a, b] + ([] if acc is None else [acc])
    in_specs = [a_spec, b_spec] + ([] if acc is None else [o_spec])
    return pl.pallas_call(
        body, name=name, grid=(m // tm, n // tn, nk),
        in_specs=in_specs, out_specs=o_spec,
        out_shape=jax.ShapeDtypeStruct((4, m, n // 4) if out_sh else (m, n), out_dtype),
        scratch_shapes=[pltpu.VMEM((tm, tn), F32)],
        input_output_aliases={} if acc is None else {2: 0},
        compiler_params=_params(dimension_semantics=("arbitrary", "arbitrary", "arbitrary")),
    )(*ins)


def rowcall(fn, rows, consts, out_rows, out_accs, *, tm, name, scratch=()):
    rows = [r if isinstance(r, tuple) else (r, None) for r in rows]
    s = rows[0][0].shape[0]
    n = s // tm
    in_specs, ins = [], []
    for arr, halo in rows:
        w = arr.shape[1]
        in_specs.append(pl.BlockSpec((tm, w), lambda i: (i, 0)))
        ins.append(arr)
        if halo is not None:
            kind, h = halo
            per = tm // h
            if kind == "prev":
                in_specs.append(pl.BlockSpec((h, w), lambda i, per=per: (jnp.maximum(i * per - 1, 0), 0)))
            else:
                in_specs.append(pl.BlockSpec((h, w), lambda i, per=per, last=s // h - 1: (jnp.minimum((i + 1) * per, last), 0)))
            ins.append(arr)
    for cst in consts:
        in_specs.append(pl.BlockSpec(cst.shape, lambda i, nd=cst.ndim: (0,) * nd))
        ins.append(cst)
    out_specs = [pl.BlockSpec((tm, w), lambda i: (i, 0)) for w, _ in out_rows]
    out_specs += [pl.BlockSpec(shp, lambda i, nd=len(shp): (0,) * nd) for shp, _ in out_accs]
    out_shape = [jax.ShapeDtypeStruct((s, w), dt) for w, dt in out_rows]
    out_shape += [jax.ShapeDtypeStruct(shp, dt) for shp, dt in out_accs]
    n_in, n_or, n_oa = len(ins), len(out_rows), len(out_accs)

    def body(*refs):
        i = pl.program_id(0)
        it = iter(refs[:n_in])
        row_vals = []
        for _, halo in rows:
            cur = next(it)[...]
            row_vals.append(cur if halo is None else (cur, next(it)[...]))
        const_vals = [r[...] for r in it]
        o_refs = refs[n_in:n_in + n_or]
        a_refs = refs[n_in + n_or:n_in + n_or + n_oa]
        outs, parts = fn(i, n, row_vals, const_vals, refs[n_in + n_or + n_oa:])
        for o_ref, val in zip(o_refs, outs, strict=True):
            o_ref[...] = val.astype(o_ref.dtype)
        for a_ref, part in zip(a_refs, parts, strict=True):
            part = jnp.broadcast_to(part, a_ref.shape).astype(a_ref.dtype)

            @pl.when(i == 0)
            def _(a_ref=a_ref, part=part):
                a_ref[...] = part

            @pl.when(i > 0)
            def _(a_ref=a_ref, part=part):
                a_ref[...] += part

    res = pl.pallas_call(
        body, name=name, grid=(n,), in_specs=in_specs, out_specs=out_specs, out_shape=out_shape,
        scratch_shapes=list(scratch),
        compiler_params=_params(dimension_semantics=("arbitrary",)),
    )(*ins)
    return res[:n_or], res[n_or:]


def stage_fwd(f, rows, params, consts, outs, *, tm, name):
    n_p = len(params)

    def fn(i, n, rv, cv, sc):
        return f([r.astype(F32) for r in rv], cv[:n_p], cv[n_p:]), []

    return rowcall(fn, rows, list(params) + list(consts), outs, [], tm=tm, name=name)[0]


def stage_bwd(f, rows, params, consts, cts, row_grads, *, tm, name):
    n_r, n_p = len(rows), len(params)

    def fn(i, n, rv, cv, sc):
        r = [x.astype(F32) for x in rv[:n_r]]
        ct = [x.astype(F32) for x in rv[n_r:]]
        _, vjp = jax.vjp(lambda r_, p_: f(r_, p_, cv[n_p:]), r, list(cv[:n_p]))
        d_r, d_p = vjp(ct)
        return [d_r[k] for k, _ in row_grads], d_p

    return rowcall(fn, list(rows) + list(cts), list(params) + list(consts),
                   [(rows[k].shape[1], dt) for k, dt in row_grads],
                   [(p.shape, F32) for p in params], tm=tm, name=name)


def _ln(x, g, b, eps=LN_EPS):
    xc = x - jnp.mean(x, -1, keepdims=True)
    var = jnp.mean(xc * xc, -1, keepdims=True)
    return xc * lax.rsqrt(var + eps) * g + b


def _sigmoid(x):
    return 1.0 / (1.0 + jnp.exp(-x))


def _softplus(x):
    return jnp.maximum(x, 0.0) + jnp.log(1.0 + jnp.exp(-jnp.abs(x)))


def _bdot(a, b):
    return jnp.dot(a.astype(BF16), b.astype(BF16), preferred_element_type=F32)


def _head_sum(x, e, et):
    return jnp.dot(jnp.dot(x, e, precision=HI, preferred_element_type=F32), et, precision=HI,
                   preferred_element_type=F32)


def f_rwkv_pre(rows, params, consts):
    zk, zl = rows
    w0, w_up, a0, a_up, g_up, k_k, k_a = params
    e, et = consts
    w = -_softplus(-(w0 + _bdot(jnp.tanh(zl[:, 0:LANES]), w_up))) - 0.5
    log_decay = -jnp.exp(w)
    a = _sigmoid(a0 + _bdot(zl[:, 0:2 * LANES], a_up))
    g = _bdot(_sigmoid(zl[:, LANES:N_LORA_PAD]), g_up)
    kk = zk * k_k
    kk = kk / jnp.maximum(jnp.sqrt(_head_sum(kk * kk, e, et)), 1e-12)
    return [zk * (1.0 + (a - 1.0) * k_a), log_decay, -kk, kk * a, g]


def f_rwkv_post(rows, params, consts):
    o, r, k, v, g = rows
    r_k, gn_g, gn_b = params
    e, et = consts
    oc = o - _head_sum(o, e, et) * (1.0 / HEAD)
    var = _head_sum(oc * oc, e, et) * (1.0 / HEAD)
    on = oc * lax.rsqrt(var + GN_EPS) * gn_g + gn_b
    return [(on + _head_sum(r * k * r_k, e, et) * v) * g]


def f_glu(rows, params, consts):
    return [rows[0] * _sigmoid(rows[1])]


def f_convln(rows, params, consts):
    x = _ln(rows[0], params[0], params[1])
    return [x * _sigmoid(x)]


def f_merge(rows, params, consts):
    gr, gc, pr, pc = rows
    return [_sigmoid(gr) * pr + _sigmoid(gc) * pc]


def f_resln(rows, params, consts):
    return [_ln(ALPHA * rows[0] + rows[1], params[0], params[1])]


def f_ln(rows, params, consts):
    return [_ln(rows[0], params[0], params[1])]


def f_attn(rows, params, consts):
    q, (k, v) = rows[0], params
    dh = q.shape[1] // XATTN_HEADS
    outs = []
    for h in range(XATTN_HEADS):
        sl = slice(h * dh, (h + 1) * dh)
        s = lax.dot_general(q[:, sl].astype(BF16), k[:, sl].astype(BF16), (((1,), (1,)), ((), ())),
                            preferred_element_type=F32) * dh ** -0.5
        p = jnp.exp(s - jnp.max(s, -1, keepdims=True))
        p = p / jnp.sum(p, -1, keepdims=True)
        outs.append(_bdot(p, v[:, sl]))
    return [jnp.concatenate(outs, axis=-1)]


def f_relu2(rows, params, consts):
    return [jnp.square(jnp.maximum(rows[0], 0.0))]


def loss_bwd(h, y, tgt, g, b, *, tm, name):
    def fn(i, n, rv, cv, sc):
        def loss(h_, y_, g_, b_):
            err = _ln(ALPHA * h_ + y_, g_, b_) - rv[2]
            return 0.5 * jnp.sum(jnp.mean(err * err, -1, keepdims=True))
        val, vjp = jax.vjp(loss, rv[0], rv[1], cv[0], cv[1])
        dh, dy, dg, db = vjp(jnp.ones((), F32))
        return [dh, dy], [val.reshape(1, 1), dg, db]

    w = h.shape[1]
    (dh, dy), (val, dg, db) = rowcall(fn, [h, y, tgt], [g, b], [(w, F32), (w, F32)],
                                      [((8, LANES), F32), (g.shape, F32), (b.shape, F32)], tm=tm, name=name)
    return val, dh, dy, dg, db


def _shift_down(cur, halo, first):
    rolled = pltpu.roll(cur, 1, 0)
    row0 = jnp.where(first, 0.0, halo[halo.shape[0] - 1:, :])
    return jnp.where(lax.broadcasted_iota(jnp.int32, cur.shape, 0) == 0, row0, rolled)


def _shift_up(cur, halo, last):
    rolled = pltpu.roll(cur, cur.shape[0] - 1, 0)
    rown = jnp.where(last, 0.0, halo[0:1, :])
    return jnp.where(lax.broadcasted_iota(jnp.int32, cur.shape, 0) == cur.shape[0] - 1, rown, rolled)


def tokenshift_fwd(zs, mus, *, tm, name):
    def fn(i, n, rv, cv, sc):
        return [z + (_shift_down(z, halo, i == 0) - z) * mu for (z, halo), mu in zip(rv, cv)], []

    return rowcall(fn, [(z, ("prev", 8)) for z in zs], mus, [(z.shape[1], F32) for z in zs], [], tm=tm, name=name)[0]


def tokenshift_bwd(dzs, zs, mus, *, tm, name):
    nz = len(zs)

    def fn(i, n, rv, cv, sc):
        outs, parts = [], []
        for (dz, dnext), (z, zprev), mu in zip(rv[:nz], rv[nz:], cv):
            g = dz * mu
            outs.append(dz - g + _shift_up(g, dnext * mu, i == n - 1))
            parts.append(jnp.sum(dz * (_shift_down(z, zprev, i == 0) - z), 0, keepdims=True))
        return outs, parts

    return rowcall(fn, [(d, ("next", 8)) for d in dzs] + [(z, ("prev", 8)) for z in zs], mus,
                   [(z.shape[1], F32) for z in zs], [(mu.shape, F32) for mu in mus], tm=tm, name=name)


def conv_fwd(u, w, b, *, tm, name):
    c = u.shape[1]

    def fn(i, n, rv, cv, sc):
        (cur, halo), (ext,) = rv[0], sc
        ext[0:CONV_HALO, :] = jnp.where(i == 0, 0.0, halo)
        ext[CONV_HALO:, :] = cur
        wv = cv[0]
        acc = jnp.broadcast_to(cv[1], cur.shape)
        for j in range(CONV_WIDTH):
            acc = acc + wv[j:j + 1, :] * ext[pl.ds(CONV_HALO - CONV_WIDTH + 1 + j, tm), :]
        return [acc], []

    return rowcall(fn, [(u, ("prev", CONV_HALO))], [w, b], [(c, F32)], [], tm=tm, name=name,
                   scratch=[pltpu.VMEM((tm + CONV_HALO, c), F32)])[0][0]


def conv_bwd(dc, u, w, *, tm, name):
    c = u.shape[1]

    def fn(i, n, rv, cv, sc):
        (dcur, dnext), (ucur, uprev) = rv
        dext, uext, dw_ref = sc
        dext[0:tm, :] = dcur
        dext[tm:, :] = jnp.where(i == n - 1, 0.0, dnext)
        uext[0:CONV_HALO, :] = jnp.where(i == 0, 0.0, uprev)
        uext[CONV_HALO:, :] = ucur
        wv = cv[0]
        du = jnp.zeros_like(dcur)
        dw_ref[...] = jnp.zeros_like(dw_ref)
        for j in range(CONV_WIDTH):
            du = du + wv[j:j + 1, :] * dext[pl.ds(CONV_WIDTH - 1 - j, tm), :]
            dw_ref[j:j + 1, :] = jnp.sum(dcur * uext[pl.ds(CONV_HALO - CONV_WIDTH + 1 + j, tm), :], 0, keepdims=True)
        return [du], [dw_ref[...], jnp.sum(dcur, 0, keepdims=True)]

    (du,), (dw, db) = rowcall(
        fn, [(dc, ("next", CONV_HALO)), (u, ("prev", CONV_HALO))], [w], [(c, F32)],
        [((CONV_HALO, c), F32), ((1, c), F32)], tm=tm, name=name,
        scratch=[pltpu.VMEM((tm + CONV_HALO, c), F32), pltpu.VMEM((tm + CONV_HALO, c), F32),
                 pltpu.VMEM((CONV_HALO, c), F32)])
    return du, dw, db


SCAN_TB = 64


def _parts2(x):
    hi = x.astype(BF16)
    return hi, (x - hi.astype(F32)).astype(BF16)


def _split3(x):
    x1 = x.astype(BF16)
    d1 = x - x1.astype(F32)
    x2 = d1.astype(BF16)
    return x1, x2, (d1 - x2.astype(F32)).astype(BF16)


def _rows3(x):
    hi, lo = _parts2(x)
    return jnp.concatenate([hi, hi, lo], axis=1)


def _bd_parts(t, left):
    def expand(u):
        zero = jnp.zeros_like(u)
        return jnp.concatenate([jnp.where(left, u, zero), jnp.where(left, zero, u)], axis=0)

    hi, lo = _parts2(t)
    return expand(hi), expand(lo)


def _w_nn(parts):
    return jnp.concatenate([parts[0], parts[1], parts[0]], axis=0)


def _w_nt(parts):
    return jnp.concatenate([parts[0], parts[1], parts[0]], axis=1)


def _nn(lhs, w):
    return jnp.dot(lhs, w, preferred_element_type=F32)


def _nt(lhs, w):
    return lax.dot_general(lhs, w, (((1,), (1,)), ((), ())), preferred_element_type=F32)


def _col_const():
    i = lax.broadcasted_iota(jnp.int32, (48, 8 * PAIR), 0) % 16
    n = lax.broadcasted_iota(jnp.int32, (48, 8 * PAIR), 1)
    return ((i % 8 == n // PAIR) & (i // 8 == n % PAIR // HEAD)).astype(BF16)


def _col_tiles(x8, col3):
    xs = jnp.concatenate([x8[:, 0:HEAD], pltpu.roll(x8, HEAD, 1)[:, 0:HEAD]], axis=0)
    return lax.dot_general(jnp.concatenate(_split3(xs), axis=0), col3, (((0,), (0,)), ((), ())),
                           preferred_element_type=F32)


def _scan_steps(pg, left, col, s_ref, refs, rows, on_state, on_out):
    r_ref, lw_ref, k_ref, v_ref, a_ref, b_ref = refs
    r8, k8, a8, b8 = r_ref[rows, :], k_ref[rows, :], a_ref[rows, :], b_ref[rows, :]
    v8, w8 = v_ref[rows, :], jnp.exp(lw_ref[rows, :])
    sub8 = lax.broadcasted_iota(jnp.int32, (8, PAIR), 0)
    ls = [slice(p * PAIR, (p + 1) * PAIR) for p in range(pg)]
    lhs = [_rows3(jnp.concatenate([r8[:, l], a8[:, l]], axis=0)) for l in ls]
    wc, bc, kc = ([_col_tiles(x8[:, l], col) for l in ls] for x8 in (w8, b8, k8))
    sts = [s_ref[p * HEAD:(p + 1) * HEAD, :] for p in range(pg)]
    sas = [_nn(lhs[p], _w_nn(_bd_parts(sts[p], left)))[8:9, :] for p in range(pg)]
    outs = [jnp.zeros((8, PAIR), F32) for _ in range(pg)]
    for j in range(8):
        tile = slice(j * PAIR, (j + 1) * PAIR)
        for p in range(pg):
            on_state(p, j, sts[p], sas[p])
            sts[p] = sts[p] * wc[p][:, tile] + bc[p][:, tile] * sas[p] + kc[p][:, tile] * v8[j:j + 1, ls[p]]
            res = _nn(lhs[p], _w_nn(_bd_parts(sts[p], left)))
            outs[p] = jnp.where(sub8 == j, res[j:j + 1, :], outs[p])
            if j < 7:
                sas[p] = res[9 + j:10 + j, :]
    for p in range(pg):
        s_ref[p * HEAD:(p + 1) * HEAD, :] = sts[p]
        on_out(p, outs[p])


def _place():
    x, y, c = lax.axis_index("x"), lax.axis_index("y"), lax.axis_index("c")
    return x, y, c, [(1 - x, y), (x, 1 - y), (1 - x, 1 - y)]


def _allgather_ops(ins, shapes, send_sems, recv_sems):
    x, y, c, chips = _place()
    sib = (x, y, 1 - c)
    nb = len(ins)

    def copy(i, kk, jj, cc, to):
        rh = shapes[i][1] // 2
        rows = ins[i].at[jj, pl.ds(cc * rh, rh), :]
        return pltpu.make_async_remote_copy(src_ref=rows, dst_ref=rows, send_sem=send_sems.at[6 * i + kk],
                                            recv_sem=recv_sems.at[6 * i + kk], device_id=to, device_id_type=MESH)

    def start():
        for i in range(nb):
            for kk, (cx, cy) in enumerate(chips):
                copy(i, kk, 2 * x + y, c, (cx, cy, c)).start()

    def finish():
        for i in range(nb):
            for kk, (cx, cy) in enumerate(chips):
                copy(i, kk, 2 * cx + cy, c, sib).wait_recv()
                copy(i, 3 + kk, 2 * cx + cy, c, sib).start()
        for i in range(nb):
            for kk, (cx, cy) in enumerate(chips):
                copy(i, 3 + kk, 2 * cx + cy, 1 - c, sib).wait_recv()
        for i in range(nb):
            for kk, (cx, cy) in enumerate(chips):
                copy(i, kk, 2 * x + y, c, (cx, cy, c)).wait_send()
                copy(i, 3 + kk, 2 * cx + cy, c, sib).wait_send()

    return start, finish


def _scatter_ops(ins, outs, send_sems, recv_sems):
    x, y, c, chips = _place()
    cps = [pltpu.make_async_remote_copy(src_ref=ins[i].at[2 * cx + cy], dst_ref=outs[i].at[kk],
                                        send_sem=send_sems.at[3 * i + kk], recv_sem=recv_sems.at[3 * i + kk],
                                        device_id=(cx, cy, c), device_id_type=MESH)
           for i in range(len(ins)) for kk, (cx, cy) in enumerate(chips)]

    def start():
        for cp in cps:
            cp.start()

    def finish():
        for cp in cps:
            cp.wait()

    return start, finish


def scan_fwd(r, lw, k, v, a, b, *, pg, name, gather=()):
    s, c = r.shape
    tb, lw_ = SCAN_TB, PAIR * pg
    ng, nt = c // lw_, s // tb
    blk = pl.BlockSpec((tb, lw_), lambda g, t: (t, g))
    nb = len(gather)

    def body(*refs):
        r_ref, lw_ref, k_ref, v_ref, a_ref, b_ref = refs[:6]
        o_ref, ck_ref = refs[6 + nb:8 + nb]
        s_ref = refs[8 + 2 * nb]
        first = (pl.program_id(0) == 0) & (pl.program_id(1) == 0)
        last = (pl.program_id(0) == ng - 1) & (pl.program_id(1) == nt - 1)
        if nb:
            start, finish = _allgather_ops(refs[6:6 + nb], [g.shape for g in gather], *refs[9 + 2 * nb:])
            pl.when(first)(start)

        @pl.when(pl.program_id(1) == 0)
        def _():
            s_ref[...] = jnp.zeros_like(s_ref)

        ck_ref[0] = s_ref[...]
        left = lax.broadcasted_iota(jnp.int32, (HEAD, PAIR), 1) < HEAD
        col = _col_const()

        def step8(t8, carry):
            rows = pl.ds(pl.multiple_of(t8 * 8, 8), 8)

            def on_out(p, o8):
                o_ref[rows, p * PAIR:(p + 1) * PAIR] = o8

            _scan_steps(pg, left, col, s_ref, (r_ref, lw_ref, k_ref, v_ref, a_ref, b_ref), rows,
                        lambda p, j, st, sa: None, on_out)
            return carry

        lax.fori_loop(0, tb // 8, step8, 0)
        if nb:
            pl.when(last)(finish)

    any_spec = pl.BlockSpec(memory_space=pl.ANY)
    res = pl.pallas_call(
        body, name=name, grid=(ng, nt), in_specs=[blk] * 6 + [any_spec] * nb,
        out_specs=[blk, pl.BlockSpec((1, pg * HEAD, PAIR), lambda g, t: (t, g, 0))] + [any_spec] * nb,
        out_shape=[jax.ShapeDtypeStruct((s, c), F32), jax.ShapeDtypeStruct((nt, c // 2, PAIR), F32)]
        + [jax.ShapeDtypeStruct(g.shape, g.dtype) for g in gather],
        input_output_aliases={6 + i: 2 + i for i in range(nb)},
        scratch_shapes=[pltpu.VMEM((pg * HEAD, PAIR), F32)]
        + ([pltpu.SemaphoreType.DMA((6 * nb,)), pltpu.SemaphoreType.DMA((6 * nb,))] if nb else []),
        compiler_params=_params(dimension_semantics=("arbitrary", "arbitrary")),
    )(r, lw, k, v, a, b, *gather)
    return res[0], res[1], list(res[2:])


def scan_bwd(r, lw, k, v, a, b, ck, do, dr_in, dk_in, dv_in, *, pg, name, scatter=()):
    s, c = r.shape
    tb, lw_ = SCAN_TB, PAIR * pg
    ng, nt = c // lw_, s // tb
    blk = pl.BlockSpec((tb, lw_), lambda g, t: (nt - 1 - t, g))
    ck_spec = pl.BlockSpec((1, pg * HEAD, PAIR), lambda g, t: (nt - 1 - t, g, 0))
    nb = len(scatter)

    def body(*refs):
        r_ref, lw_ref, k_ref, v_ref, a_ref, b_ref, ck_ref, do_ref, dri_ref, dki_ref, dvi_ref = refs[:11]
        dr_ref, dlw_ref, dk_ref, dv_ref, da_ref, db_ref = refs[11 + nb:17 + nb]
        s_ref, ds_ref, tall_ref, sa_ref = refs[17 + 2 * nb:21 + 2 * nb]
        if nb:
            start, finish = _scatter_ops(refs[11:11 + nb], refs[17 + nb:17 + 2 * nb], *refs[21 + 2 * nb:])
            pl.when((pl.program_id(0) == 0) & (pl.program_id(1) == 0))(start)

        @pl.when(pl.program_id(1) == 0)
        def _():
            ds_ref[...] = jnp.zeros_like(ds_ref)

        left = lax.broadcasted_iota(jnp.int32, (HEAD, PAIR), 1) < HEAD
        sub8 = lax.broadcasted_iota(jnp.int32, (8, PAIR), 0)
        col = _col_const()
        ls = [slice(p * PAIR, (p + 1) * PAIR) for p in range(pg)]
        hs = [slice(p * HEAD, (p + 1) * HEAD) for p in range(pg)]
        s_ref[...] = ck_ref[0]

        def fwd8(t8, carry):
            base = pl.multiple_of(t8 * 8, 8)
            rows = pl.ds(base, 8)
            sa_acc = [jnp.zeros((8, PAIR), F32) for _ in range(pg)]

            def on_state(p, j, st, sa):
                tall_ref[base + j, hs[p], :] = st
                sa_acc[p] = jnp.where(sub8 == j, sa, sa_acc[p])

            _scan_steps(pg, left, col, s_ref, (r_ref, lw_ref, k_ref, v_ref, a_ref, b_ref), rows,
                        on_state, lambda p, o8: None)
            for p in range(pg):
                sa_ref[rows, ls[p]] = sa_acc[p]
            return carry

        lax.fori_loop(0, tb // 8, fwd8, 0)
        tall_ref[tb] = s_ref[...]
        ones_lhs = jnp.ones((8, PAIR), BF16)

        def bwd8(i8, carry):
            base = pl.multiple_of((tb // 8 - 1 - i8) * 8, 8)
            rows = pl.ds(base, 8)
            r8, k8, a8, b8 = r_ref[rows, :], k_ref[rows, :], a_ref[rows, :], b_ref[rows, :]
            v8, do8, w8, sa8 = v_ref[rows, :], do_ref[rows, :], jnp.exp(lw_ref[rows, :]), sa_ref[rows, :]
            lhs_kb = [_rows3(jnp.concatenate([k8[:, l], b8[:, l]], axis=0)) for l in ls]
            lhs_vs = [jnp.concatenate([v8[:, l], sa8[:, l]], axis=0).astype(BF16) for l in ls]
            lhs_do = [do8[:, l].astype(BF16) for l in ls]
            rc, wc, ac = ([_col_tiles(x8[:, l], col) for l in ls] for x8 in (r8, w8, a8))
            dss = [ds_ref[h, :] for h in hs]
            t_post = [_bd_parts(tall_ref[base + 8, h, :], left)[0] for h in hs]
            acc = [[jnp.zeros((8, PAIR), F32) for _ in range(6)] for _ in range(pg)]
            for j in reversed(range(8)):
                tile = slice(j * PAIR, (j + 1) * PAIR)
                res_nn, res_nt, res_r, res_w, t_prev = [], [], [], [], []
                for p in range(pg):
                    t_prev_f = tall_ref[base + j, hs[p], :]
                    t_prev.append(_bd_parts(t_prev_f, left)[0])
                    dss[p] = dss[p] + rc[p][:, tile] * do8[j:j + 1, ls[p]]
                    dd = _bd_parts(dss[p], left)
                    res_nn.append(_nn(lhs_kb[p], _w_nn(dd)))
                    res_nt.append(_nt(lhs_vs[p], dd[0]))
                    res_r.append(_nt(lhs_do[p], t_post[p]))
                    res_w.append(_nt(ones_lhs, _bd_parts(dss[p] * t_prev_f, left)[0]))
                for p in range(pg):
                    dsa = res_nn[p][8 + j:9 + j, :]
                    new = [res_r[p][j:j + 1, :],
                           res_w[p][0:1, :],
                           res_nt[p][j:j + 1, :],
                           res_nn[p][j:j + 1, :],
                           _nt(jnp.broadcast_to(dsa, (8, PAIR)).astype(BF16), t_prev[p])[0:1, :],
                           res_nt[p][8 + j:9 + j, :]]
                    acc[p] = [jnp.where(sub8 == j, n_, a_) for n_, a_ in zip(new, acc[p])]
                    dss[p] = dss[p] * wc[p][:, tile] + ac[p][:, tile] * dsa
                    t_post[p] = t_prev[p]
            for p in range(pg):
                ds_ref[hs[p], :] = dss[p]
                dr_ref[rows, ls[p]] = acc[p][0] + dri_ref[rows, ls[p]]
                dlw_ref[rows, ls[p]] = acc[p][1] * w8[:, ls[p]]
                dk_ref[rows, ls[p]] = acc[p][2] + dki_ref[rows, ls[p]]
                dv_ref[rows, ls[p]] = acc[p][3] + dvi_ref[rows, ls[p]]
                da_ref[rows, ls[p]] = acc[p][4]
                db_ref[rows, ls[p]] = acc[p][5]
            return carry

        lax.fori_loop(0, tb // 8, bwd8, 0)
        if nb:
            pl.when((pl.program_id(0) == ng - 1) & (pl.program_id(1) == nt - 1))(finish)

    any_spec = pl.BlockSpec(memory_space=pl.ANY)
    res = pl.pallas_call(
        body, name=name, grid=(ng, nt), in_specs=[blk] * 6 + [ck_spec] + [blk] * 4 + [any_spec] * nb,
        out_specs=[blk] * 6 + [any_spec] * nb,
        out_shape=[jax.ShapeDtypeStruct((s, c), F32)] * 6
        + [jax.ShapeDtypeStruct((3,) + p.shape[1:], p.dtype) for p in scatter],
        scratch_shapes=[pltpu.VMEM((pg * HEAD, PAIR), F32), pltpu.VMEM((pg * HEAD, PAIR), F32),
                        pltpu.VMEM((tb + 1, pg * HEAD, PAIR), F32), pltpu.VMEM((tb, pg * PAIR), F32)]
        + ([pltpu.SemaphoreType.DMA((3 * nb,)), pltpu.SemaphoreType.DMA((3 * nb,))] if nb else []),
        compiler_params=_params(dimension_semantics=("arbitrary", "arbitrary")),
    )(r, lw, k, v, a, b, ck, do, dr_in, dk_in, dv_in, *scatter)
    return res[:6], list(res[6:])


_ANY = pl.BlockSpec(memory_space=pl.ANY)


def into_slot(w, dtype, *, name):
    r, c = w.shape
    tr = _tile(r, max(8, (1 << 20) // (c * 4) // 16 * 16), unit=16) if r % 16 == 0 else r
    j_arr = (2 * lax.axis_index("x") + lax.axis_index("y")).astype(jnp.int32).reshape(1)

    def body(j_ref, w_ref, o_ref):
        o_ref[...] = w_ref[...].astype(o_ref.dtype)

    return pl.pallas_call(
        body, name=name,
        grid_spec=pltpu.PrefetchScalarGridSpec(
            num_scalar_prefetch=1, grid=(r // tr,),
            in_specs=[pl.BlockSpec((tr, c), lambda i, j_ref: (i, 0))],
            out_specs=pl.BlockSpec((None, tr, c), lambda i, j_ref: (j_ref[0], i, 0))),
        out_shape=jax.ShapeDtypeStruct((4, r, c), dtype),
        compiler_params=_params(dimension_semantics=("arbitrary",)),
    )(j_arr, w)


def allgather_multi(bufs, *, name):
    nb = len(bufs)

    def body(*refs):
        start, finish = _allgather_ops(refs[:nb], [b.shape for b in bufs], *refs[2 * nb:])
        start()
        finish()

    return pl.pallas_call(
        body, name=name, in_specs=[_ANY] * nb, out_specs=[_ANY] * nb,
        out_shape=[jax.ShapeDtypeStruct(b.shape, b.dtype) for b in bufs],
        input_output_aliases={i: i for i in range(nb)},
        scratch_shapes=[pltpu.SemaphoreType.DMA((6 * nb,)), pltpu.SemaphoreType.DMA((6 * nb,))],
    )(*bufs)


def sibling_swap_multi(gs, *, name):
    nb = len(gs)

    def body(*refs):
        ins, outs, (send_sems, recv_sems) = refs[:nb], refs[nb:2 * nb], refs[2 * nb:]
        x, y, c, _ = _place()
        cps = []
        for i in range(nb):
            rh = gs[i].shape[1] // 2
            cps.append(pltpu.make_async_remote_copy(
                src_ref=ins[i].at[:, pl.ds((1 - c) * rh, rh), :], dst_ref=outs[i], send_sem=send_sems.at[i],
                recv_sem=recv_sems.at[i], device_id=(x, y, 1 - c), device_id_type=MESH))
        for cp in cps:
            cp.start()
        for cp in cps:
            cp.wait()

    return pl.pallas_call(
        body, name=name, in_specs=[_ANY] * nb, out_specs=[_ANY] * nb,
        out_shape=[jax.ShapeDtypeStruct((4, g.shape[1] // 2, g.shape[2]), g.dtype) for g in gs],
        scratch_shapes=[pltpu.SemaphoreType.DMA((nb,)), pltpu.SemaphoreType.DMA((nb,))],
    )(*gs)


def pair_sum(g, got, *, tr, name):
    _, rh, w = got.shape
    nb = rh // tr
    c_arr = lax.axis_index("c").astype(jnp.int32).reshape(1)

    def body(c_ref, g_ref, got_ref, o_ref):
        o_ref[...] = (g_ref[...].astype(F32) + got_ref[...].astype(F32)).astype(o_ref.dtype)

    return pl.pallas_call(
        body, name=name,
        grid_spec=pltpu.PrefetchScalarGridSpec(
            num_scalar_prefetch=1, grid=(4, nb),
            in_specs=[pl.BlockSpec((1, tr, w), lambda j, i, c_ref: (j, c_ref[0] * nb + i, 0)),
                      pl.BlockSpec((1, tr, w), lambda j, i, c_ref: (j, i, 0))],
            out_specs=pl.BlockSpec((1, tr, w), lambda j, i, c_ref: (j, i, 0))),
        out_shape=jax.ShapeDtypeStruct((4, rh, w), got.dtype),
        compiler_params=_params(dimension_semantics=("arbitrary", "arbitrary")),
    )(c_arr, g, got)


def scatter_multi(pss, *, name):
    nb = len(pss)

    def body(*refs):
        start, finish = _scatter_ops(refs[:nb], refs[nb:2 * nb], *refs[2 * nb:])
        start()
        finish()

    return pl.pallas_call(
        body, name=name, in_specs=[_ANY] * nb, out_specs=[_ANY] * nb,
        out_shape=[jax.ShapeDtypeStruct((3,) + p.shape[1:], p.dtype) for p in pss],
        scratch_shapes=[pltpu.SemaphoreType.DMA((3 * nb,)), pltpu.SemaphoreType.DMA((3 * nb,))],
    )(*pss)


def chip_sum(ps, got, *, tr, name):
    _, rh, w = ps.shape
    nb = rh // tr
    jc_arr = jnp.stack([2 * lax.axis_index("x") + lax.axis_index("y"), lax.axis_index("c")]).astype(jnp.int32)

    def body(jc_ref, ps_ref, got_ref, o_ref):
        acc = ps_ref[0].astype(F32)
        for kk in range(3):
            acc = acc + got_ref[kk].astype(F32)
        o_ref[...] = acc

    return pl.pallas_call(
        body, name=name,
        grid_spec=pltpu.PrefetchScalarGridSpec(
            num_scalar_prefetch=1, grid=(nb,),
            in_specs=[pl.BlockSpec((1, tr, w), lambda i, jc: (jc[0], i, 0)),
                      pl.BlockSpec((3, tr, w), lambda i, jc: (0, i, 0))],
            out_specs=pl.BlockSpec((tr, w), lambda i, jc: (jc[1] * nb + i, 0))),
        out_shape=jax.ShapeDtypeStruct((2 * rh, w), F32),
        compiler_params=_params(dimension_semantics=("arbitrary",)),
    )(jc_arr, ps, got)


def sibling_join_multi(reds, *, name):
    nb = len(reds)

    def body(*refs):
        ins, (send_sems, recv_sems) = refs[:nb], refs[2 * nb:]
        x, y, c, _ = _place()

        def copy(i, cc):
            rh = reds[i].shape[0] // 2
            rows = ins[i].at[pl.ds(cc * rh, rh), :]
            return pltpu.make_async_remote_copy(src_ref=rows, dst_ref=rows, send_sem=send_sems.at[i],
                                                recv_sem=recv_sems.at[i], device_id=(x, y, 1 - c), device_id_type=MESH)

        cps = [copy(i, c) for i in range(nb)]
        for cp in cps:
            cp.start()
        for i, cp in enumerate(cps):
            cp.wait_send()
            copy(i, 1 - c).wait_recv()

    return pl.pallas_call(
        body, name=name, in_specs=[_ANY] * nb, out_specs=[_ANY] * nb,
        out_shape=[jax.ShapeDtypeStruct(r.shape, r.dtype) for r in reds],
        input_output_aliases={i: i for i in range(nb)},
        scratch_shapes=[pltpu.SemaphoreType.DMA((nb,)), pltpu.SemaphoreType.DMA((nb,))],
    )(*reds)


def allreduce_small(part, *, name):
    m_per, n = part.shape

    def body(x_ref, sum_ref, all_ref, send_sems, recv_sems, local_sem):
        x, y, c, chips = _place()
        me, sib = (x, y, c), (x, y, 1 - c)

        def rows(px, py, pc):
            return all_ref.at[pl.ds((4 * px + 2 * py + pc) * m_per, m_per), :]

        def copy(kk, block, to, src=None):
            return pltpu.make_async_remote_copy(src_ref=rows(*block) if src is None else src, dst_ref=rows(*block),
                                                send_sem=send_sems.at[kk], recv_sem=recv_sems.at[kk],
                                                device_id=to, device_id_type=MESH)

        mine = pltpu.make_async_copy(x_ref, rows(*me), local_sem)
        mine.start()
        first = [copy(0, me, sib, src=x_ref)]
        first += [copy(1 + kk, me, (*chip, c), src=x_ref) for kk, chip in enumerate(chips)]
        for cp in first:
            cp.start()
        passed = [copy(4 + kk, (*chip, c), sib) for kk, chip in enumerate(chips)]
        for kk, chip in enumerate(chips):
            copy(1 + kk, (*chip, c), me).wait_recv()
            passed[kk].start()
        copy(0, sib, me).wait_recv()
        for kk, chip in enumerate(chips):
            copy(4 + kk, (*chip, 1 - c), me).wait_recv()
        for cp in first + passed:
            cp.wait_send()
        mine.wait()
        acc = all_ref[0:m_per, :]
        for d in range(1, 8):
            acc = acc + all_ref[d * m_per:(d + 1) * m_per, :]
        sum_ref[...] = acc

    vmem = pl.BlockSpec(memory_space=pltpu.VMEM)
    return pl.pallas_call(
        body, name=name, in_specs=[vmem], out_specs=vmem,
        out_shape=jax.ShapeDtypeStruct((m_per, n), part.dtype),
        scratch_shapes=[pltpu.VMEM((8 * m_per, n), part.dtype), pltpu.SemaphoreType.DMA((7,)),
                        pltpu.SemaphoreType.DMA((7,)), pltpu.SemaphoreType.DMA],
    )(part)


def adamw(w, g, m, v, *, name):
    r, c = w.shape
    tm = r if r * c * 4 <= (1 << 20) else _tile(r, max(8, ((1 << 20) // (c * 4)) // 8 * 8), unit=8)
    bc1, bc2 = 1.0 - ADAM_B1 ** ADAM_STEP, 1.0 - ADAM_B2 ** ADAM_STEP

    def fn(i, n, rv, cv, sc):
        w_, g_, m_, v_ = rv
        m_ = ADAM_B1 * m_ + (1.0 - ADAM_B1) * g_
        v_ = ADAM_B2 * v_ + (1.0 - ADAM_B2) * (g_ * g_)
        delta = -ADAM_LR * ((m_ / bc1) / (jnp.sqrt(v_ / bc2) + ADAM_EPS) + ADAM_WD * w_)
        return [delta, m_, v_], []

    return rowcall(fn, [w, g, m, v], [], [(c, F32)] * 3, [], tm=tm, name=name)[0]


def _head_one_hot(c):
    e = (lax.broadcasted_iota(jnp.int32, (c, LANES), 0) // HEAD
         == lax.broadcasted_iota(jnp.int32, (c, LANES), 1)).astype(F32)
    return e, e.T


def _join_cols(g):
    return jnp.concatenate([g[j] for j in range(4)], axis=1)


def _split_cols(a):
    return jnp.stack(jnp.split(a, 4, axis=1))


LATE = ("w_out", "xattn_wq", "xattn_wk", "xattn_wv", "xattn_wo", "mlp_w1", "mlp_w2")


def reduce_over_core_pair(names, g_list):
    gots = sibling_swap_multi(g_list, name=f"rs_sibling_swap_{names[0]}")
    return [pair_sum(g, got, tr=_half_tile(g.shape[1] // 2, g.shape[2]), name=f"rs_pair_sum_{n}")
            for n, g, got in zip(names, g_list, gots)]


def layer_step(x, mem, tgt, wg, sp, late_slots=None):
    s, d = x.shape
    dr, dc = sp["rwkv_w0"].shape[1], sp["conv_b"].shape[1]
    n_lora = N_DECAY + N_ICLR + N_GATE
    n_rwkv = 3 * dr + n_lora
    pad_l = N_LORA_PAD - n_lora
    w_in = _join_cols(wg["w_in"])
    cuts = [0, dr, 2 * dr, 3 * dr, n_rwkv, n_rwkv + dc, n_rwkv + 2 * dc, n_rwkv + 2 * dc + d, n_rwkv + 2 * dc + 2 * d]
    w_r, w_k, w_v, w_l, w_ca, w_cb, w_gr, w_gc = (w_in[:, lo:hi] for lo, hi in zip(cuts[:-1], cuts[1:]))
    w_l = jnp.pad(w_l, ((0, 0), (0, pad_l)))
    sm = sp["rwkv_shift_mix"]
    mus = [sm[:, 0:dr], sm[:, dr:2 * dr], sm[:, 2 * dr:3 * dr], jnp.pad(sm[:, 3 * dr:], ((0, 0), (0, pad_l)))]
    w_up = jnp.pad(_join_cols(wg["rwkv_w_up"]).astype(F32), ((0, LANES - N_DECAY), (0, 0)))
    a_up = jnp.pad(_join_cols(wg["rwkv_a_up"]).astype(F32), ((N_DECAY, 2 * LANES - N_DECAY - N_ICLR), (0, 0)))
    g_lo = N_DECAY + N_ICLR - LANES
    g_up = jnp.pad(_join_cols(wg["rwkv_g_up"]).astype(F32), ((g_lo, pad_l), (0, 0)))
    conv_w = _join_cols(wg["conv_w"])
    row_sharded = ("w_out", "xattn_wq", "xattn_wk", "xattn_wv", "xattn_wo", "mlp_w2")
    e, et = _head_one_hot(dr)
    pre_p = [sp["rwkv_w0"], w_up, sp["rwkv_a0"], a_up, g_up, sp["rwkv_k_k"], sp["rwkv_k_a"]]
    post_p = [sp["rwkv_r_k"], sp["rwkv_gn_g"], sp["rwkv_gn_b"]]
    pairs = dr // PAIR
    tm = min(s, 128)
    tmm = mem.shape[0]

    z_r, z_k, z_v, z_l = (mm(x, w, name=f"z_{n}") for n, w in zip("rkvl", (w_r, w_k, w_v, w_l)))
    z_ca, z_cb = mm(x, w_ca, name="z_ca"), mm(x, w_cb, name="z_cb")
    z_gr, z_gc = mm(x, w_gr, name="z_gr"), mm(x, w_gc, name="z_gc")
    zs_r, zs_k, zs_v, zs_l = tokenshift_fwd([z_r, z_k, z_v, z_l], mus, tm=tm, name="shift_fwd")
    pre_o = [(dr, F32)] * 5
    k_m, lw, a_s, b_s, g = stage_fwd(f_rwkv_pre, [zs_k, zs_l], pre_p, [e, et], pre_o, tm=min(s, 64), name="pre_fwd")
    o, ck, gathered = scan_fwd(zs_r, lw, k_m, zs_v, a_s, b_s, pg=min(8, pairs), name="scan_fwd",
                               gather=[late_slots[n] for n in LATE] if late_slots else ())
    wg = dict(wg, **dict(zip(LATE, gathered)))
    wt = {n: wg[n].reshape(-1, wg[n].shape[2]) for n in row_sharded}
    post_r = [o, zs_r, k_m, zs_v, g]
    (o_r,) = stage_fwd(f_rwkv_post, post_r, post_p, [e, et], [(dr, BF16)], tm=min(s, 64), name="post_fwd")
    (u,) = stage_fwd(f_glu, [z_ca, z_cb], [], [], [(dc, F32)], tm=tm, name="glu_fwd")
    cv = conv_fwd(u, conv_w, sp["conv_b"], tm=tm, name="conv_fwd")
    cln_p = [sp["conv_ln_g"], sp["conv_ln_b"]]
    (o_c,) = stage_fwd(f_convln, [cv], cln_p, [], [(dc, BF16)], tm=tm, name="convln_fwd")
    p_r = mm(o_r, wg["proj_rwkv"], b_sh=True, name="proj_r")
    p_c = mm(o_c, wg["proj_conv"], b_sh=True, name="proj_c")
    (merged,) = stage_fwd(f_merge, [z_gr, z_gc, p_r, p_c], [], [], [(d, BF16)], tm=tm, name="merge_fwd")
    y1 = mm(merged, wt["w_out"], name="y1")
    ln1_p, ln2_p, lnm_p = ([sp[f"{n}_g"], sp[f"{n}_b"]] for n in ("ln1", "ln2", "ln_mem"))
    (h1,) = stage_fwd(f_resln, [x, y1], ln1_p, [], [(d, F32)], tm=tm, name="ln1_fwd")
    (mem_n,) = stage_fwd(f_ln, [mem], lnm_p, [], [(d, F32)], tm=tmm, name="lnmem_fwd")
    k_mem, v_mem = mm(mem_n, wt["xattn_wk"], name="k_mem"), mm(mem_n, wt["xattn_wv"], name="v_mem")
    q = mm(h1, wt["xattn_wq"], name="q")
    (ao,) = stage_fwd(f_attn, [q], [k_mem, v_mem], [], [(d, BF16)], tm=tm, name="attn_fwd")
    ca = mm(ao, wt["xattn_wo"], name="ca")
    (h2,) = stage_fwd(f_resln, [h1, ca], ln2_p, [], [(d, F32)], tm=tm, name="ln2_fwd")
    u1 = mm(h2, wg["mlp_w1"], b_sh=True, name="u1")
    f_dim = u1.shape[1]
    tmf = min(s, 64)
    (act,) = stage_fwd(f_relu2, [u1], [], [], [(f_dim, BF16)], tm=tmf, name="relu2_fwd")
    ff = mm(act, wt["mlp_w2"], name="ff")

    gw, gs = {}, {}
    loss, dh2, dff, gs["ln3_g"], gs["ln3_b"] = loss_bwd(h2, ff, tgt, sp["ln3_g"], sp["ln3_b"], tm=tm, name="loss_bwd")
    gw["mlp_w2"] = mm(act, dff, ta=True, out_dtype=BF16, name="g_mlp_w2")
    dact = mm(dff, wt["mlp_w2"], tb=True, name="d_act")
    (du1,), _ = stage_bwd(f_relu2, [u1], [], [], [dact], [(0, BF16)], tm=tmf, name="relu2_bwd")
    gw["mlp_w1"] = mm(h2, du1, ta=True, out_dtype=BF16, out_sh=True, name="g_mlp_w1")
    dh2 = mm(du1, wg["mlp_w1"], tb=True, b_sh=True, acc=dh2, name="d_h2")
    (dh1, dca), (gs["ln2_g"], gs["ln2_b"]) = stage_bwd(f_resln, [h1, ca], ln2_p, [], [dh2], [(0, F32), (1, F32)],
                                                       tm=tm, name="ln2_bwd")
    gw["xattn_wo"] = mm(ao, dca, ta=True, out_dtype=BF16, name="g_wo")
    dao = mm(dca, wt["xattn_wo"], tb=True, name="d_ao")
    (dq,), (dk_mem, dv_mem) = stage_bwd(f_attn, [q], [k_mem, v_mem], [], [dao], [(0, F32)], tm=tm, name="attn_bwd")
    gw["xattn_wq"] = mm(h1, dq, ta=True, out_dtype=BF16, name="g_wq")
    dh1 = mm(dq, wt["xattn_wq"], tb=True, acc=dh1, name="d_h1")
    gw["xattn_wk"] = mm(mem_n, dk_mem, ta=True, out_dtype=BF16, name="g_wk")
    gw["xattn_wv"] = mm(mem_n, dv_mem, ta=True, out_dtype=BF16, name="g_wv")
    dmem_n = mm(dk_mem, wt["xattn_wk"], tb=True, name="d_memn_k")
    dmem_n = mm(dv_mem, wt["xattn_wv"], tb=True, acc=dmem_n, name="d_memn_v")
    _, (gs["ln_mem_g"], gs["ln_mem_b"]) = stage_bwd(f_ln, [mem], lnm_p, [], [dmem_n], [], tm=tmm, name="lnmem_bwd")
    (dx, dy1), (gs["ln1_g"], gs["ln1_b"]) = stage_bwd(f_resln, [x, y1], ln1_p, [], [dh1], [(0, F32), (1, F32)],
                                                      tm=tm, name="ln1_bwd")
    gw["w_out"] = mm(merged, dy1, ta=True, out_dtype=BF16, name="g_w_out")
    dmerged = mm(dy1, wt["w_out"], tb=True, name="d_merged")
    (dz_gr, dz_gc, dp_r, dp_c), _ = stage_bwd(f_merge, [z_gr, z_gc, p_r, p_c], [], [], [dmerged],
                                              [(0, BF16), (1, BF16), (2, F32), (3, F32)], tm=tm, name="merge_bwd")
    gw["proj_rwkv"] = mm(o_r, dp_r, ta=True, out_dtype=BF16, out_sh=True, name="g_proj_r")
    gw["proj_conv"] = mm(o_c, dp_c, ta=True, out_dtype=BF16, out_sh=True, name="g_proj_c")
    do_r = mm(dp_r, wg["proj_rwkv"], tb=True, b_sh=True, name="d_o_r")
    do_c = mm(dp_c, wg["proj_conv"], tb=True, b_sh=True, name="d_o_c")
    (dcv,), (gs["conv_ln_g"], gs["conv_ln_b"]) = stage_bwd(f_convln, [cv], cln_p, [], [do_c], [(0, F32)],
                                                           tm=tm, name="convln_bwd")
    du, g_conv_w, gs["conv_b"] = conv_bwd(dcv, u, conv_w, tm=tm, name="conv_bwd")
    gw["conv_w"] = _split_cols(g_conv_w.astype(BF16))
    (dz_ca, dz_cb), _ = stage_bwd(f_glu, [z_ca, z_cb], [], [], [du], [(0, BF16), (1, BF16)], tm=tm, name="glu_bwd")
    (d_o, dr_p, dk_p, dv_p, dg), (gs["rwkv_r_k"], gs["rwkv_gn_g"], gs["rwkv_gn_b"]) = stage_bwd(
        f_rwkv_post, post_r, post_p, [e, et], [do_r], [(k, F32) for k in range(5)], tm=min(s, 64), name="post_bwd")
    for n in row_sharded:
        gw[n] = gw[n].reshape(wg[n].shape)
    pss = reduce_over_core_pair(LATE, [gw[n] for n in LATE]) if late_slots else []
    (dzs_r, dlw, dk_m, dzs_v, da_s, db_s), sent = scan_bwd(zs_r, lw, k_m, zs_v, a_s, b_s, ck, d_o, dr_p, dk_p, dv_p,
                                                           pg=min(8, pairs), name="scan_bwd", scatter=pss)
    (dzs_k, dzs_l), pre_g = stage_bwd(f_rwkv_pre, [zs_k, zs_l], pre_p, [e, et], [dk_m, dlw, da_s, db_s, dg],
                                      [(0, F32), (1, F32)], tm=min(s, 64), name="pre_bwd")
    gs["rwkv_w0"], g_w_up, gs["rwkv_a0"], g_a_up, g_g_up, gs["rwkv_k_k"], gs["rwkv_k_a"] = pre_g
    gw["rwkv_w_up"] = _split_cols(g_w_up[0:N_DECAY].astype(BF16))
    gw["rwkv_a_up"] = _split_cols(g_a_up[N_DECAY:N_DECAY + N_ICLR].astype(BF16))
    gw["rwkv_g_up"] = _split_cols(g_g_up[g_lo:g_lo + N_GATE].astype(BF16))
    dzs, dmus = tokenshift_bwd([dzs_r, dzs_k, dzs_v, dzs_l], [z_r, z_k, z_v, z_l], mus, tm=tm, name="shift_bwd")
    gs["rwkv_shift_mix"] = jnp.concatenate(list(dmus[:3]) + [dmus[3][:, 0:n_lora]], axis=1)
    dzs = list(dzs) + [dz_ca, dz_cb, dz_gr, dz_gc]
    g_in = []
    for n, dz, w in zip(("r", "k", "v", "l", "ca", "cb", "gr", "gc"), dzs, (w_r, w_k, w_v, w_l, w_ca, w_cb, w_gr, w_gc)):
        g_in.append(mm(x, dz, ta=True, out_dtype=BF16, name=f"g_w_{n}"))
        dx = mm(dz, w, tb=True, acc=dx, name=f"d_x_{n}")
    g_in[3] = g_in[3][:, 0:n_lora]
    gw["w_in"] = _split_cols(jnp.concatenate(g_in, axis=1))
    return loss, dx, gw, gs, ((pss, sent) if late_slots else None)


WEIGHTS = ["w_in", "rwkv_shift_mix", "rwkv_w0", "rwkv_w_up", "rwkv_a0", "rwkv_a_up", "rwkv_g_up", "rwkv_k_k",
           "rwkv_k_a", "rwkv_r_k", "rwkv_gn_g", "rwkv_gn_b", "conv_w", "conv_b", "conv_ln_g", "conv_ln_b",
           "proj_rwkv", "proj_conv", "w_out", "ln1_g", "ln1_b", "ln_mem_g", "ln_mem_b", "xattn_wq", "xattn_wk",
           "xattn_wv", "xattn_wo", "ln2_g", "ln2_b", "mlp_w1", "mlp_w2", "ln3_g", "ln3_b"]
SHARD_AXIS = {"w_in": 1, "rwkv_w_up": 1, "rwkv_a_up": 1, "rwkv_g_up": 1, "conv_w": 1, "proj_rwkv": 1, "proj_conv": 1,
              "w_out": 0, "xattn_wq": 0, "xattn_wk": 0, "xattn_wv": 0, "xattn_wo": 0, "mlp_w1": 1, "mlp_w2": 0}


def _unpack(flat, shapes):
    out, off = [], 0
    for shp in shapes:
        n = 1
        for dim in shp:
            n *= dim
        out.append(flat[..., off:off + n].reshape(flat.shape[:-1] + tuple(shp)))
        off += n
    return out


def _half_tile(rh, w):
    return _tile(rh, max(16, (2 << 20) // (w * 4) // 16 * 16), unit=16)


def kernel(*args):
    n_w = len(WEIGHTS)
    x, mem = args[0][0], args[1][0]
    tgt = args[2 + n_w][0]
    w_loc = {n: a for n, a in zip(WEIGHTS, args[2:2 + n_w])}
    m_loc = {n: a for n, a in zip(WEIGHTS, args[3 + n_w:3 + 2 * n_w])}
    v_loc = {n: a for n, a in zip(WEIGHTS, args[3 + 2 * n_w:3 + 3 * n_w])}
    big = [n for n in WEIGHTS if n in SHARD_AXIS]
    small = [n for n in WEIGHTS if n not in SHARD_AXIS]

    def as2d(n, a):
        if n in SHARD_AXIS:
            return a.reshape(a.shape[1], a.shape[-1])
        return a.reshape(1, -1)

    loc2d = {n: as2d(n, w_loc[n]) for n in WEIGHTS}
    conv_rows = loc2d["conv_w"].shape[0]
    slots = []
    for n in big:
        if n == "conv_w":
            padded = jnp.pad(loc2d[n], ((0, CONV_HALO - conv_rows), (0, 0)))
            slots.append(into_slot(padded, F32, name=f"slot_{n}"))
        else:
            slots.append(into_slot(loc2d[n], BF16, name=f"slot_{n}"))
    slots = dict(zip(big, slots))
    early = [n for n in big if n not in LATE]
    wg = dict(zip(early, allgather_multi([slots[n] for n in early], name="allgather_weights")))
    sp = {n: loc2d[n] for n in small}

    loss_part, grad_x, gw, gs, (pss_late, sent_late) = layer_step(x, mem, tgt, wg, sp, {n: slots[n] for n in LATE})

    pss = reduce_over_core_pair(early, [gw[n] for n in early])
    sent = list(scatter_multi(pss, name="rs_scatter"))
    order = early + list(LATE)
    reds = [chip_sum(ps, got, tr=_half_tile(ps.shape[1], ps.shape[2]), name=f"rs_chip_sum_{n}")
            for n, ps, got in zip(order, pss + pss_late, sent + sent_late)]
    reds = sibling_join_multi(reds, name="rs_sibling_join")
    g_big = {n: (r[0:conv_rows] if n == "conv_w" else r) for n, r in zip(order, reds)}

    small_parts = [gs[n] for n in small] + [loss_part[0:1, 0:1]]
    flat = jnp.concatenate([p.reshape(-1).astype(F32) for p in small_parts])
    flat = jnp.pad(flat, (0, -flat.shape[0] % (8 * LANES))).reshape(-1, LANES)
    red = allreduce_small(flat, name="allreduce_small").reshape(-1)
    g_small = dict(zip(small, _unpack(red, [loc2d[n].shape for n in small])))
    n_small = sum(loc2d[n].shape[1] for n in small)
    loss = red[n_small]

    grads, deltas, new_m, new_v = {}, {}, {}, {}
    for n in big:
        g2 = g_big[n]
        d2, m2, v2 = adamw(loc2d[n], g2, as2d(n, m_loc[n]), as2d(n, v_loc[n]), name=f"adamw_{n}")
        shp = w_loc[n].shape
        grads[n], deltas[n], new_m[n], new_v[n] = (t.reshape(shp) for t in (g2, d2, m2, v2))

    def small_pack(d):
        f = jnp.concatenate([as2d(n, d[n]).reshape(-1) for n in small])
        return jnp.pad(f, (0, -f.shape[0] % (8 * LANES))).reshape(-1, LANES)

    g_pack = small_pack({n: g_small[n] for n in small})
    outs = adamw(small_pack(w_loc), g_pack, small_pack(m_loc), small_pack(v_loc), name="adamw_small")
    for dst, packed in zip((deltas, new_m, new_v), outs):
        for n, t in zip(small, _unpack(packed.reshape(-1), [loc2d[n].shape for n in small])):
            dst[n] = t.reshape(w_loc[n].shape)
    for n in small:
        grads[n] = g_small[n].reshape(w_loc[n].shape)

    return (loss, grad_x[None], *[grads[n] for n in WEIGHTS], *[deltas[n] for n in WEIGHTS],
            *[new_m[n] for n in WEIGHTS], *[new_v[n] for n in WEIGHTS])
```

```python
import functools

import jax
import jax.numpy as jnp
from jax import lax
from jax.experimental import pallas as pl
from jax.experimental.pallas import tpu as pltpu

F32, BF16 = jnp.float32, jnp.bfloat16
ALPHA = 2.0 ** 0.25
LN_EPS = 1e-5
GN_EPS = 64e-5
HEAD = 64
LANES = 128
PAIR = 2 * HEAD
XATTN_HEADS = 4
CONV_WIDTH = 31
CONV_HALO = 32
N_DECAY, N_ICLR, N_GATE = 96, 96, 256
N_LORA_PAD = 512
VMEM_LIMIT = 56 * 1024 * 1024
ADAM_LR, ADAM_B1, ADAM_B2, ADAM_EPS, ADAM_WD, ADAM_STEP = 0.001, 0.9, 0.999, 1e-8, 0.01, 10
MESH = pl.DeviceIdType.MESH
HI = lax.Precision.HIGHEST


def _params(**kw):
    return pltpu.CompilerParams(vmem_limit_bytes=VMEM_LIMIT, **kw)


def _tile(n, pref, unit=LANES):
    if n <= pref:
        return n
    t = pref
    while n % t:
        t -= unit
    return t


def mm(a, b, *, name, ta=False, tb=False, out_dtype=F32, acc=None, b_sh=False, out_sh=False):
    m, k = (a.shape[1], a.shape[0]) if ta else a.shape
    if b_sh:
        n = b.shape[1] if tb else 4 * b.shape[2]
    else:
        n = b.shape[0] if tb else b.shape[1]
    tm, tn, tk = _tile(m, 1024), _tile(n // 4 if (b_sh and not tb) or out_sh else n, 1024), _tile(k // 4 if b_sh and tb else k, 1024)
    nk = k // tk
    a_spec = pl.BlockSpec((tk, tm), lambda i, j, kk: (kk, i)) if ta else pl.BlockSpec((tm, tk), lambda i, j, kk: (i, kk))
    if b_sh and tb:
        per_k = k // 4 // tk
        b_spec = pl.BlockSpec((None, tn, tk), lambda i, j, kk: (kk // per_k, j, kk % per_k))
    elif b_sh:
        per_n = n // 4 // tn
        b_spec = pl.BlockSpec((None, tk, tn), lambda i, j, kk: (j // per_n, kk, j % per_n))
    else:
        b_spec = pl.BlockSpec((tn, tk), lambda i, j, kk: (j, kk)) if tb else pl.BlockSpec((tk, tn), lambda i, j, kk: (kk, j))
    if out_sh:
        per_o = n // 4 // tn
        o_spec = pl.BlockSpec((None, tm, tn), lambda i, j, kk: (j // per_o, i, j % per_o))
    else:
        o_spec = pl.BlockSpec((tm, tn), lambda i, j, kk: (i, j))
    dims = (((0 if ta else 1,), (1 if tb else 0,)), ((), ()))

    def body(*refs):
        if acc is None:
            a_ref, b_ref, o_ref, acc_ref = refs
        else:
            a_ref, b_ref, c_ref, o_ref, acc_ref = refs
        kk = pl.program_id(2)

        @pl.when(kk == 0)
        def _():
            acc_ref[...] = jnp.zeros_like(acc_ref) if acc is None else c_ref[...].astype(F32)

        acc_ref[...] += lax.dot_general(a_ref[...].astype(BF16), b_ref[...].astype(BF16), dims,
                                        preferred_element_type=F32)

        @pl.when(kk == nk - 1)
        def _():
            o_ref[...] = acc_ref[...].astype(o_ref.dtype)

    ins = [a, b] + ([] if acc is None else [acc])
    in_specs = [a_spec, b_spec] + ([] if acc is None else [o_spec])
    return pl.pallas_call(
        body, name=name, grid=(m // tm, n // tn, nk),
        in_specs=in_specs, out_specs=o_spec,
        out_shape=jax.ShapeDtypeStruct((4, m, n // 4) if out_sh else (m, n), out_dtype),
        scratch_shapes=[pltpu.VMEM((tm, tn), F32)],
        input_output_aliases={} if acc is None else {2: 0},
        compiler_params=_params(dimension_semantics=("arbitrary", "arbitrary", "arbitrary")),
    )(*ins)


def rowcall(fn, rows, consts, out_rows, out_accs, *, tm, name, scratch=()):
    rows = [r if isinstance(r, tuple) else (r, None) for r in rows]
    s = rows[0][0].shape[0]
    n = s // tm
    in_specs, ins = [], []
    for arr, halo in rows:
        w = arr.shape[1]
        in_specs.append(pl.BlockSpec((tm, w), lambda i: (i, 0)))
        ins.append(arr)
        if halo is not None:
            kind, h = halo
            per = tm // h
            if kind == "prev":
                in_specs.append(pl.BlockSpec((h, w), lambda i, per=per: (jnp.maximum(i * per - 1, 0), 0)))
            else:
                in_specs.append(pl.BlockSpec((h, w), lambda i, per=per, last=s // h - 1: (jnp.minimum((i + 1) * per, last), 0)))
            ins.append(arr)
    for cst in consts:
        in_specs.append(pl.BlockSpec(cst.shape, lambda i, nd=cst.ndim: (0,) * nd))
        ins.append(cst)
    out_specs = [pl.BlockSpec((tm, w), lambda i: (i, 0)) for w, _ in out_rows]
    out_specs += [pl.BlockSpec(shp, lambda i, nd=len(shp): (0,) * nd) for shp, _ in out_accs]
    out_shape = [jax.ShapeDtypeStruct((s, w), dt) for w, dt in out_rows]
    out_shape += [jax.ShapeDtypeStruct(shp, dt) for shp, dt in out_accs]
    n_in, n_or, n_oa = len(ins), len(out_rows), len(out_accs)

    def body(*refs):
        i = pl.program_id(0)
        it = iter(refs[:n_in])
        row_vals = []
        for _, halo in rows:
            cur = next(it)[...]
            row_vals.append(cur if halo is None else (cur, next(it)[...]))
        const_vals = [r[...] for r in it]
        o_refs = refs[n_in:n_in + n_or]
        a_refs = refs[n_in + n_or:n_in + n_or + n_oa]
        outs, parts = fn(i, n, row_vals, const_vals, refs[n_in + n_or + n_oa:])
        for o_ref, val in zip(o_refs, outs, strict=True):
            o_ref[...] = val.astype(o_ref.dtype)
        for a_ref, part in zip(a_refs, parts, strict=True):
            part = jnp.broadcast_to(part, a_ref.shape).astype(a_ref.dtype)

            @pl.when(i == 0)
            def _(a_ref=a_ref, part=part):
                a_ref[...] = part

            @pl.when(i > 0)
            def _(a_ref=a_ref, part=part):
                a_ref[...] += part

    res = pl.pallas_call(
        body, name=name, grid=(n,), in_specs=in_specs, out_specs=out_specs, out_shape=out_shape,
        scratch_shapes=list(scratch),
        compiler_params=_params(dimension_semantics=("arbitrary",)),
    )(*ins)
    return res[:n_or], res[n_or:]


def stage_fwd(f, rows, params, consts, outs, *, tm, name):
    n_p = len(params)

    def fn(i, n, rv, cv, sc):
        return f([r.astype(F32) for r in rv], cv[:n_p], cv[n_p:]), []

    return rowcall(fn, rows, list(params) + list(consts), outs, [], tm=tm, name=name)[0]


def stage_bwd(f, rows, params, consts, cts, row_grads, *, tm, name):
    n_r, n_p = len(rows), len(params)

    def fn(i, n, rv, cv, sc):
        r = [x.astype(F32) for x in rv[:n_r]]
        ct = [x.astype(F32) for x in rv[n_r:]]
        _, vjp = jax.vjp(lambda r_, p_: f(r_, p_, cv[n_p:]), r, list(cv[:n_p]))
        d_r, d_p = vjp(ct)
        return [d_r[k] for k, _ in row_grads], d_p

    return rowcall(fn, list(rows) + list(cts), list(params) + list(consts),
                   [(rows[k].shape[1], dt) for k, dt in row_grads],
                   [(p.shape, F32) for p in params], tm=tm, name=name)


def _ln(x, g, b, eps=LN_EPS):
    xc = x - jnp.mean(x, -1, keepdims=True)
    var = jnp.mean(xc * xc, -1, keepdims=True)
    return xc * lax.rsqrt(var + eps) * g + b


def _sigmoid(x):
    return 1.0 / (1.0 + jnp.exp(-x))


def _softplus(x):
    return jnp.maximum(x, 0.0) + jnp.log(1.0 + jnp.exp(-jnp.abs(x)))


def _bdot(a, b):
    return jnp.dot(a.astype(BF16), b.astype(BF16), preferred_element_type=F32)


def _head_sum(x, e, et):
    return jnp.dot(jnp.dot(x, e, precision=HI, preferred_element_type=F32), et, precision=HI,
                   preferred_element_type=F32)


def f_rwkv_pre(rows, params, consts):
    zk, zl = rows
    w0, w_up, a0, a_up, g_up, k_k, k_a = params
    e, et = consts
    w = -_softplus(-(w0 + _bdot(jnp.tanh(zl[:, 0:LANES]), w_up))) - 0.5
    log_decay = -jnp.exp(w)
    a = _sigmoid(a0 + _bdot(zl[:, 0:2 * LANES], a_up))
    g = _bdot(_sigmoid(zl[:, LANES:N_LORA_PAD]), g_up)
    kk = zk * k_k
    kk = kk / jnp.maximum(jnp.sqrt(_head_sum(kk * kk, e, et)), 1e-12)
    return [zk * (1.0 + (a - 1.0) * k_a), log_decay, -kk, kk * a, g]


def f_rwkv_post(rows, params, consts):
    o, r, k, v, g = rows
    r_k, gn_g, gn_b = params
    e, et = consts
    oc = o - _head_sum(o, e, et) * (1.0 / HEAD)
    var = _head_sum(oc * oc, e, et) * (1.0 / HEAD)
    on = oc * lax.rsqrt(var + GN_EPS) * gn_g + gn_b
    return [(on + _head_sum(r * k * r_k, e, et) * v) * g]


def f_glu(rows, params, consts):
    return [rows[0] * _sigmoid(rows[1])]


def f_convln(rows, params, consts):
    x = _ln(rows[0], params[0], params[1])
    return [x * _sigmoid(x)]


def f_merge(rows, params, consts):
    gr, gc, pr, pc = rows
    return [_sigmoid(gr) * pr + _sigmoid(gc) * pc]


def f_resln(rows, params, consts):
    return [_ln(ALPHA * rows[0] + rows[1], params[0], params[1])]


def f_ln(rows, params, consts):
    return [_ln(rows[0], params[0], params[1])]


def f_attn(rows, params, consts):
    q, (k, v) = rows[0], params
    dh = q.shape[1] // XATTN_HEADS
    outs = []
    for h in range(XATTN_HEADS):
        sl = slice(h * dh, (h + 1) * dh)
        s = lax.dot_general(q[:, sl].astype(BF16), k[:, sl].astype(BF16), (((1,), (1,)), ((), ())),
                            preferred_element_type=F32) * dh ** -0.5
        p = jnp.exp(s - jnp.max(s, -1, keepdims=True))
        p = p / jnp.sum(p, -1, keepdims=True)
        outs.append(_bdot(p, v[:, sl]))
    return [jnp.concatenate(outs, axis=-1)]


def f_relu2(rows, params, consts):
    return [jnp.square(jnp.maximum(rows[0], 0.0))]


def loss_bwd(h, y, tgt, g, b, *, tm, name):
    def fn(i, n, rv, cv, sc):
        def loss(h_, y_, g_, b_):
            err = _ln(ALPHA * h_ + y_, g_, b_) - rv[2]
            return 0.5 * jnp.sum(jnp.mean(err * err, -1, keepdims=True))
        val, vjp = jax.vjp(loss, rv[0], rv[1], cv[0], cv[1])
        dh, dy, dg, db = vjp(jnp.ones((), F32))
        return [dh, dy], [val.reshape(1, 1), dg, db]

    w = h.shape[1]
    (dh, dy), (val, dg, db) = rowcall(fn, [h, y, tgt], [g, b], [(w, F32), (w, F32)],
                                      [((8, LANES), F32), (g.shape, F32), (b.shape, F32)], tm=tm, name=name)
    return val, dh, dy, dg, db


def _shift_down(cur, halo, first):
    rolled = pltpu.roll(cur, 1, 0)
    row0 = jnp.where(first, 0.0, halo[halo.shape[0] - 1:, :])
    return jnp.where(lax.broadcasted_iota(jnp.int32, cur.shape, 0) == 0, row0, rolled)


def _shift_up(cur, halo, last):
    rolled = pltpu.roll(cur, cur.shape[0] - 1, 0)
    rown = jnp.where(last, 0.0, halo[0:1, :])
    return jnp.where(lax.broadcasted_iota(jnp.int32, cur.shape, 0) == cur.shape[0] - 1, rown, rolled)


def tokenshift_fwd(zs, mus, *, tm, name):
    def fn(i, n, rv, cv, sc):
        return [z + (_shift_down(z, halo, i == 0) - z) * mu for (z, halo), mu in zip(rv, cv)], []

    return rowcall(fn, [(z, ("prev", 8)) for z in zs], mus, [(z.shape[1], F32) for z in zs], [], tm=tm, name=name)[0]


def tokenshift_bwd(dzs, zs, mus, *, tm, name):
    nz = len(zs)

    def fn(i, n, rv, cv, sc):
        outs, parts = [], []
        for (dz, dnext), (z, zprev), mu in zip(rv[:nz], rv[nz:], cv):
            g = dz * mu
            outs.append(dz - g + _shift_up(g, dnext * mu, i == n - 1))
            parts.append(jnp.sum(dz * (_shift_down(z, zprev, i == 0) - z), 0, keepdims=True))
        return outs, parts

    return rowcall(fn, [(d, ("next", 8)) for d in dzs] + [(z, ("prev", 8)) for z in zs], mus,
                   [(z.shape[1], F32) for z in zs], [(mu.shape, F32) for mu in mus], tm=tm, name=name)


def conv_fwd(u, w, b, *, tm, name):
    c = u.shape[1]

    def fn(i, n, rv, cv, sc):
        (cur, halo), (ext,) = rv[0], sc
        ext[0:CONV_HALO, :] = jnp.where(i == 0, 0.0, halo)
        ext[CONV_HALO:, :] = cur
        wv = cv[0]
        acc = jnp.broadcast_to(cv[1], cur.shape)
        for j in range(CONV_WIDTH):
            acc = acc + wv[j:j + 1, :] * ext[pl.ds(CONV_HALO - CONV_WIDTH + 1 + j, tm), :]
        return [acc], []

    return rowcall(fn, [(u, ("prev", CONV_HALO))], [w, b], [(c, F32)], [], tm=tm, name=name,
                   scratch=[pltpu.VMEM((tm + CONV_HALO, c), F32)])[0][0]


def conv_bwd(dc, u, w, *, tm, name):
    c = u.shape[1]

    def fn(i, n, rv, cv, sc):
        (dcur, dnext), (ucur, uprev) = rv
        dext, uext, dw_ref = sc
        dext[0:tm, :] = dcur
        dext[tm:, :] = jnp.where(i == n - 1, 0.0, dnext)
        uext[0:CONV_HALO, :] = jnp.where(i == 0, 0.0, uprev)
        uext[CONV_HALO:, :] = ucur
        wv = cv[0]
        du = jnp.zeros_like(dcur)
        dw_ref[...] = jnp.zeros_like(dw_ref)
        for j in range(CONV_WIDTH):
            du = du + wv[j:j + 1, :] * dext[pl.ds(CONV_WIDTH - 1 - j, tm), :]
            dw_ref[j:j + 1, :] = jnp.sum(dcur * uext[pl.ds(CONV_HALO - CONV_WIDTH + 1 + j, tm), :], 0, keepdims=True)
        return [du], [dw_ref[...], jnp.sum(dcur, 0, keepdims=True)]

    (du,), (dw, db) = rowcall(
        fn, [(dc, ("next", CONV_HALO)), (u, ("prev", CONV_HALO))], [w], [(c, F32)],
        [((CONV_HALO, c), F32), ((1, c), F32)], tm=tm, name=name,
        scratch=[pltpu.VMEM((tm + CONV_HALO, c), F32), pltpu.VMEM((tm + CONV_HALO, c), F32),
                 pltpu.VMEM((CONV_HALO, c), F32)])
    return du, dw, db


SCAN_TB = 64


def _parts2(x):
    hi = x.astype(BF16)
    return hi, (x - hi.astype(F32)).astype(BF16)


def _split3(x):
    x1 = x.astype(BF16)
    d1 = x - x1.astype(F32)
    x2 = d1.astype(BF16)
    return x1, x2, (d1 - x2.astype(F32)).astype(BF16)


def _rows3(x):
    hi, lo = _parts2(x)
    return jnp.concatenate([hi, hi, lo], axis=1)


def _bd_parts(t, left):
    def expand(u):
        zero = jnp.zeros_like(u)
        return jnp.concatenate([jnp.where(left, u, zero), jnp.where(left, zero, u)], axis=0)

    hi, lo = _parts2(t)
    return expand(hi), expand(lo)


def _w_nn(parts):
    return jnp.concatenate([parts[0], parts[1], parts[0]], axis=0)


def _w_nt(parts):
    return jnp.concatenate([parts[0], parts[1], parts[0]], axis=1)


def _nn(lhs, w):
    return jnp.dot(lhs, w, preferred_element_type=F32)


def _nt(lhs, w):
    return lax.dot_general(lhs, w, (((1,), (1,)), ((), ())), preferred_element_type=F32)


def _col_const():
    i = lax.broadcasted_iota(jnp.int32, (48, 8 * PAIR), 0) % 16
    n = lax.broadcasted_iota(jnp.int32, (48, 8 * PAIR), 1)
    return ((i % 8 == n // PAIR) & (i // 8 == n % PAIR // HEAD)).astype(BF16)


def _col_tiles(x8, col3):
    xs = jnp.concatenate([x8[:, 0:HEAD], pltpu.roll(x8, HEAD, 1)[:, 0:HEAD]], axis=0)
    return lax.dot_general(jnp.concatenate(_split3(xs), axis=0), col3, (((0,), (0,)), ((), ())),
                           preferred_element_type=F32)


def _scan_steps(pg, left, col, s_ref, refs, rows, on_state, on_out):
    r_ref, lw_ref, k_ref, v_ref, a_ref, b_ref = refs
    r8, k8, a8, b8 = r_ref[rows, :], k_ref[rows, :], a_ref[rows, :], b_ref[rows, :]
    v8, w8 = v_ref[rows, :], jnp.exp(lw_ref[rows, :])
    sub8 = lax.broadcasted_iota(jnp.int32, (8, PAIR), 0)
    ls = [slice(p * PAIR, (p + 1) * PAIR) for p in range(pg)]
    lhs = [_rows3(jnp.concatenate([r8[:, l], a8[:, l]], axis=0)) for l in ls]
    wc, bc, kc = ([_col_tiles(x8[:, l], col) for l in ls] for x8 in (w8, b8, k8))
    sts = [s_ref[p * HEAD:(p + 1) * HEAD, :] for p in range(pg)]
    sas = [_nn(lhs[p], _w_nn(_bd_parts(sts[p], left)))[8:9, :] for p in range(pg)]
    outs = [jnp.zeros((8, PAIR), F32) for _ in range(pg)]
    for j in range(8):
        tile = slice(j * PAIR, (j + 1) * PAIR)
        for p in range(pg):
            sts[p] = sts[p] * wc[p][:, tile] + bc[p][:, tile] * sas[p] + kc[p][:, tile] * v8[j:j + 1, ls[p]]
            on_state(p, j, sts[p], sas[p])
            res = _nn(lhs[p], _w_nn(_bd_parts(sts[p], left)))
            outs[p] = jnp.where(sub8 == j, res[j:j + 1, :], outs[p])
            if j < 7:
                sas[p] = res[9 + j:10 + j, :]
    for p in range(pg):
        s_ref[p * HEAD:(p + 1) * HEAD, :] = sts[p]
        on_out(p, outs[p])


def _place():
    x, y, c = lax.axis_index("x"), lax.axis_index("y"), lax.axis_index("c")
    return x, y, c, [(1 - x, y), (x, 1 - y), (1 - x, 1 - y)]


def _allgather_ops(ins, shapes, send_sems, recv_sems):
    x, y, c, chips = _place()
    sib = (x, y, 1 - c)
    nb = len(ins)

    def copy(i, kk, jj, cc, to):
        rh = shapes[i][1] // 2
        rows = ins[i].at[jj, pl.ds(cc * rh, rh), :]
        return pltpu.make_async_remote_copy(src_ref=rows, dst_ref=rows, send_sem=send_sems.at[6 * i + kk],
                                            recv_sem=recv_sems.at[6 * i + kk], device_id=to, device_id_type=MESH)

    def start():
        for i in range(nb):
            for kk, (cx, cy) in enumerate(chips):
                copy(i, kk, 2 * x + y, c, (cx, cy, c)).start()

    def finish():
        for i in range(nb):
            for kk, (cx, cy) in enumerate(chips):
                copy(i, kk, 2 * cx + cy, c, sib).wait_recv()
                copy(i, 3 + kk, 2 * cx + cy, c, sib).start()
        for i in range(nb):
            for kk, (cx, cy) in enumerate(chips):
                copy(i, 3 + kk, 2 * cx + cy, 1 - c, sib).wait_recv()
        for i in range(nb):
            for kk, (cx, cy) in enumerate(chips):
                copy(i, kk, 2 * x + y, c, (cx, cy, c)).wait_send()
                copy(i, 3 + kk, 2 * cx + cy, c, sib).wait_send()

    return start, finish


def _scatter_ops(ins, outs, send_sems, recv_sems):
    x, y, c, chips = _place()
    cps = [pltpu.make_async_remote_copy(src_ref=ins[i].at[2 * cx + cy], dst_ref=outs[i].at[kk],
                                        send_sem=send_sems.at[3 * i + kk], recv_sem=recv_sems.at[3 * i + kk],
                                        device_id=(cx, cy, c), device_id_type=MESH)
           for i in range(len(ins)) for kk, (cx, cy) in enumerate(chips)]

    def start():
        for cp in cps:
            cp.start()

    def finish():
        for cp in cps:
            cp.wait()

    return start, finish


def scan_fwd(r, lw, k, v, a, b, *, pg, name, gather=()):
    s, c = r.shape
    tb, lw_ = SCAN_TB, PAIR * pg
    ng, nt = c // lw_, s // tb
    blk = pl.BlockSpec((tb, lw_), lambda g, t: (t, g))
    nb = len(gather)

    def body(*refs):
        r_ref, lw_ref, k_ref, v_ref, a_ref, b_ref = refs[:6]
        o_ref, ck_ref, tpost_ref, sa_ref = refs[6 + nb:10 + nb]
        s_ref = refs[10 + 2 * nb]
        first = (pl.program_id(0) == 0) & (pl.program_id(1) == 0)
        last = (pl.program_id(0) == ng - 1) & (pl.program_id(1) == nt - 1)
        if nb:
            start, finish = _allgather_ops(refs[6:6 + nb], [g.shape for g in gather], *refs[11 + 2 * nb:])
            pl.when(first)(start)

        @pl.when(pl.program_id(1) == 0)
        def _():
            s_ref[...] = jnp.zeros_like(s_ref)

        ck_ref[0] = s_ref[...]
        left = lax.broadcasted_iota(jnp.int32, (HEAD, PAIR), 1) < HEAD
        col = _col_const()

        sub8 = lax.broadcasted_iota(jnp.int32, (8, PAIR), 0)

        def step8(t8, carry):
            base = pl.multiple_of(t8 * 8, 8)
            rows = pl.ds(base, 8)
            sa_acc = [jnp.zeros((8, PAIR), F32) for _ in range(pg)]

            def on_state(p, j, st, sa):
                tpost_ref[base + j, p * HEAD:(p + 1) * HEAD, :] = st
                sa_acc[p] = jnp.where(sub8 == j, sa, sa_acc[p])

            def on_out(p, o8):
                o_ref[rows, p * PAIR:(p + 1) * PAIR] = o8
                sa_ref[rows, p * PAIR:(p + 1) * PAIR] = sa_acc[p]

            _scan_steps(pg, left, col, s_ref, (r_ref, lw_ref, k_ref, v_ref, a_ref, b_ref), rows, on_state, on_out)
            return carry

        lax.fori_loop(0, tb // 8, step8, 0)
        if nb:
            pl.when(last)(finish)

    any_spec = pl.BlockSpec(memory_space=pl.ANY)
    res = pl.pallas_call(
        body, name=name, grid=(ng, nt), in_specs=[blk] * 6 + [any_spec] * nb,
        out_specs=[blk, pl.BlockSpec((1, pg * HEAD, PAIR), lambda g, t: (t, g, 0)),
                   pl.BlockSpec((tb, pg * HEAD, PAIR), lambda g, t: (t, g, 0)), blk] + [any_spec] * nb,
        out_shape=[jax.ShapeDtypeStruct((s, c), F32), jax.ShapeDtypeStruct((nt, c // 2, PAIR), F32),
                   jax.ShapeDtypeStruct((s, c // 2, PAIR), F32), jax.ShapeDtypeStruct((s, c), F32)]
        + [jax.ShapeDtypeStruct(g.shape, g.dtype) for g in gather],
        input_output_aliases={6 + i: 4 + i for i in range(nb)},
        scratch_shapes=[pltpu.VMEM((pg * HEAD, PAIR), F32)]
        + ([pltpu.SemaphoreType.DMA((6 * nb,)), pltpu.SemaphoreType.DMA((6 * nb,))] if nb else []),
        compiler_params=_params(dimension_semantics=("arbitrary", "arbitrary")),
    )(r, lw, k, v, a, b, *gather)
    return res[:4], list(res[4:])


def scan_bwd(r, lw, k, v, a, b, ck, tpost, sa, do, dr_in, dk_in, dv_in, *, pg, name, scatter=()):
    s, c = r.shape
    tb, lw_ = SCAN_TB, PAIR * pg
    ng, nt = c // lw_, s // tb
    blk = pl.BlockSpec((tb, lw_), lambda g, t: (nt - 1 - t, g))
    ck_spec = pl.BlockSpec((1, pg * HEAD, PAIR), lambda g, t: (nt - 1 - t, g, 0))
    nb = len(scatter)

    def body(*refs):
        (r_ref, lw_ref, k_ref, v_ref, a_ref, b_ref, ck_ref, tpost_ref, sa_ref, do_ref, dri_ref, dki_ref,
         dvi_ref) = refs[:13]
        dr_ref, dlw_ref, dk_ref, dv_ref, da_ref, db_ref = refs[13 + nb:19 + nb]
        ds_ref, rows_ref = refs[19 + 2 * nb:21 + 2 * nb]
        if nb:
            start, finish = _scatter_ops(refs[13:13 + nb], refs[19 + nb:19 + 2 * nb], *refs[21 + 2 * nb:])
            pl.when((pl.program_id(0) == 0) & (pl.program_id(1) == 0))(start)

        @pl.when(pl.program_id(1) == 0)
        def _():
            ds_ref[...] = jnp.zeros_like(ds_ref)

        left = lax.broadcasted_iota(jnp.int32, (HEAD, PAIR), 1) < HEAD
        col = _col_const()
        ls = [slice(p * PAIR, (p + 1) * PAIR) for p in range(pg)]
        hs = [slice(p * HEAD, (p + 1) * HEAD) for p in range(pg)]
        ones_lhs = jnp.ones((8, PAIR), BF16)

        def state_before(t, h):
            return jnp.where(t == 0, ck_ref[0, h, :], tpost_ref[jnp.maximum(t - 1, 0), h, :])

        def bwd8(i8, carry):
            base = pl.multiple_of((tb // 8 - 1 - i8) * 8, 8)
            rows = pl.ds(base, 8)
            r8, k8, a8, b8 = r_ref[rows, :], k_ref[rows, :], a_ref[rows, :], b_ref[rows, :]
            v8, do8, w8, sa8 = v_ref[rows, :], do_ref[rows, :], jnp.exp(lw_ref[rows, :]), sa_ref[rows, :]
            lhs_kb = [_rows3(jnp.concatenate([k8[:, l], b8[:, l]], axis=0)) for l in ls]
            lhs_vs = [jnp.concatenate([v8[:, l], sa8[:, l]], axis=0).astype(BF16) for l in ls]
            lhs_do = [do8[:, l].astype(BF16) for l in ls]
            rc, wc, ac = ([_col_tiles(x8[:, l], col) for l in ls] for x8 in (r8, w8, a8))
            dss = [ds_ref[h, :] for h in hs]
            t_post = [_bd_parts(tpost_ref[base + 7, h, :], left)[0] for h in hs]
            for j in reversed(range(8)):
                tile = slice(j * PAIR, (j + 1) * PAIR)
                res_nn, res_nt, res_r, res_w, t_prev = [], [], [], [], []
                for p in range(pg):
                    t_prev_f = state_before(base + j, hs[p])
                    t_prev.append(_bd_parts(t_prev_f, left)[0])
                    dss[p] = dss[p] + rc[p][:, tile] * do8[j:j + 1, ls[p]]
                    dd = _bd_parts(dss[p], left)
                    res_nn.append(_nn(lhs_kb[p], _w_nn(dd)))
                    res_nt.append(_nt(lhs_vs[p], dd[0]))
                    res_r.append(_nt(lhs_do[p], t_post[p]))
                    res_w.append(_nt(ones_lhs, _bd_parts(dss[p] * t_prev_f, left)[0]))
                for p in range(pg):
                    dsa = res_nn[p][8 + j:9 + j, :]
                    new = [res_r[p][j:j + 1, :],
                           res_w[p][0:1, :],
                           res_nt[p][j:j + 1, :],
                           res_nn[p][j:j + 1, :],
                           _nt(jnp.broadcast_to(dsa, (8, PAIR)).astype(BF16), t_prev[p])[0:1, :],
                           res_nt[p][8 + j:9 + j, :]]
                    for q, row in enumerate(new):
                        rows_ref[q, j:j + 1, ls[p]] = row
                    dss[p] = dss[p] * wc[p][:, tile] + ac[p][:, tile] * dsa
                    t_post[p] = t_prev[p]
            for p in range(pg):
                ds_ref[hs[p], :] = dss[p]
            dr_ref[rows, :] = rows_ref[0] + dri_ref[rows, :]
            dlw_ref[rows, :] = rows_ref[1] * w8
            dk_ref[rows, :] = rows_ref[2] + dki_ref[rows, :]
            dv_ref[rows, :] = rows_ref[3] + dvi_ref[rows, :]
            da_ref[rows, :] = rows_ref[4]
            db_ref[rows, :] = rows_ref[5]
            return carry

        lax.fori_loop(0, tb // 8, bwd8, 0)
        if nb:
            pl.when((pl.program_id(0) == ng - 1) & (pl.program_id(1) == nt - 1))(finish)

    any_spec = pl.BlockSpec(memory_space=pl.ANY)
    tpost_spec = pl.BlockSpec((tb, pg * HEAD, PAIR), lambda g, t: (nt - 1 - t, g, 0))
    res = pl.pallas_call(
        body, name=name, grid=(ng, nt),
        in_specs=[blk] * 6 + [ck_spec, tpost_spec] + [blk] * 5 + [any_spec] * nb,
        out_specs=[blk] * 6 + [any_spec] * nb,
        out_shape=[jax.ShapeDtypeStruct((s, c), F32)] * 6
        + [jax.ShapeDtypeStruct((3,) + p.shape[1:], p.dtype) for p in scatter],
        scratch_shapes=[pltpu.VMEM((pg * HEAD, PAIR), F32), pltpu.VMEM((6, 8, pg * PAIR), F32)]
        + ([pltpu.SemaphoreType.DMA((3 * nb,)), pltpu.SemaphoreType.DMA((3 * nb,))] if nb else []),
        compiler_params=_params(dimension_semantics=("arbitrary", "arbitrary")),
    )(r, lw, k, v, a, b, ck, tpost, sa, do, dr_in, dk_in, dv_in, *scatter)
    return res[:6], list(res[6:])


_ANY = pl.BlockSpec(memory_space=pl.ANY)


def into_slot(w, dtype, *, name):
    r, c = w.shape
    tr = _tile(r, max(8, (1 << 20) // (c * 4) // 16 * 16), unit=16) if r % 16 == 0 else r
    j_arr = (2 * lax.axis_index("x") + lax.axis_index("y")).astype(jnp.int32).reshape(1)

    def body(j_ref, w_ref, o_ref):
        o_ref[...] = w_ref[...].astype(o_ref.dtype)

    return pl.pallas_call(
        body, name=name,
        grid_spec=pltpu.PrefetchScalarGridSpec(
            num_scalar_prefetch=1, grid=(r // tr,),
            in_specs=[pl.BlockSpec((tr, c), lambda i, j_ref: (i, 0))],
            out_specs=pl.BlockSpec((None, tr, c), lambda i, j_ref: (j_ref[0], i, 0))),
        out_shape=jax.ShapeDtypeStruct((4, r, c), dtype),
        compiler_params=_params(dimension_semantics=("arbitrary",)),
    )(j_arr, w)


def allgather_multi(bufs, *, name):
    nb = len(bufs)

    def body(*refs):
        start, finish = _allgather_ops(refs[:nb], [b.shape for b in bufs], *refs[2 * nb:])
        start()
        finish()

    return pl.pallas_call(
        body, name=name, in_specs=[_ANY] * nb, out_specs=[_ANY] * nb,
        out_shape=[jax.ShapeDtypeStruct(b.shape, b.dtype) for b in bufs],
        input_output_aliases={i: i for i in range(nb)},
        scratch_shapes=[pltpu.SemaphoreType.DMA((6 * nb,)), pltpu.SemaphoreType.DMA((6 * nb,))],
    )(*bufs)


def sibling_swap_multi(gs, *, name):
    nb = len(gs)

    def body(*refs):
        ins, outs, (send_sems, recv_sems) = refs[:nb], refs[nb:2 * nb], refs[2 * nb:]
        x, y, c, _ = _place()
        cps = []
        for i in range(nb):
            rh = gs[i].shape[1] // 2
            cps.append(pltpu.make_async_remote_copy(
                src_ref=ins[i].at[:, pl.ds((1 - c) * rh, rh), :], dst_ref=outs[i], send_sem=send_sems.at[i],
                recv_sem=recv_sems.at[i], device_id=(x, y, 1 - c), device_id_type=MESH))
        for cp in cps:
            cp.start()
        for cp in cps:
            cp.wait()

    return pl.pallas_call(
        body, name=name, in_specs=[_ANY] * nb, out_specs=[_ANY] * nb,
        out_shape=[jax.ShapeDtypeStruct((4, g.shape[1] // 2, g.shape[2]), g.dtype) for g in gs],
        scratch_shapes=[pltpu.SemaphoreType.DMA((nb,)), pltpu.SemaphoreType.DMA((nb,))],
    )(*gs)


def pair_sum(g, got, *, tr, name):
    _, rh, w = got.shape
    nb = rh // tr
    c_arr = lax.axis_index("c").astype(jnp.int32).reshape(1)

    def body(c_ref, g_ref, got_ref, o_ref):
        o_ref[...] = (g_ref[...].astype(F32) + got_ref[...].astype(F32)).astype(o_ref.dtype)

    return pl.pallas_call(
        body, name=name,
        grid_spec=pltpu.PrefetchScalarGridSpec(
            num_scalar_prefetch=1, grid=(4, nb),
            in_specs=[pl.BlockSpec((1, tr, w), lambda j, i, c_ref: (j, c_ref[0] * nb + i, 0)),
                      pl.BlockSpec((1, tr, w), lambda j, i, c_ref: (j, i, 0))],
            out_specs=pl.BlockSpec((1, tr, w), lambda j, i, c_ref: (j, i, 0))),
        out_shape=jax.ShapeDtypeStruct((4, rh, w), got.dtype),
        compiler_params=_params(dimension_semantics=("arbitrary", "arbitrary")),
    )(c_arr, g, got)


def scatter_multi(pss, *, name):
    nb = len(pss)

    def body(*refs):
        start, finish = _scatter_ops(refs[:nb], refs[nb:2 * nb], *refs[2 * nb:])
        start()
        finish()

    return pl.pallas_call(
        body, name=name, in_specs=[_ANY] * nb, out_specs=[_ANY] * nb,
        out_shape=[jax.ShapeDtypeStruct((3,) + p.shape[1:], p.dtype) for p in pss],
        scratch_shapes=[pltpu.SemaphoreType.DMA((3 * nb,)), pltpu.SemaphoreType.DMA((3 * nb,))],
    )(*pss)


def chip_sum(ps, got, *, tr, name):
    _, rh, w = ps.shape
    nb = rh // tr
    jc_arr = jnp.stack([2 * lax.axis_index("x") + lax.axis_index("y"), lax.axis_index("c")]).astype(jnp.int32)

    def body(jc_ref, ps_ref, got_ref, o_ref):
        acc = ps_ref[0].astype(F32)
        for kk in range(3):
            acc = acc + got_ref[kk].astype(F32)
        o_ref[...] = acc

    return pl.pallas_call(
        body, name=name,
        grid_spec=pltpu.PrefetchScalarGridSpec(
            num_scalar_prefetch=1, grid=(nb,),
            in_specs=[pl.BlockSpec((1, tr, w), lambda i, jc: (jc[0], i, 0)),
                      pl.BlockSpec((3, tr, w), lambda i, jc: (0, i, 0))],
            out_specs=pl.BlockSpec((tr, w), lambda i, jc: (jc[1] * nb + i, 0))),
        out_shape=jax.ShapeDtypeStruct((2 * rh, w), F32),
        compiler_params=_params(dimension_semantics=("arbitrary",)),
    )(jc_arr, ps, got)


def sibling_join_multi(reds, *, name):
    nb = len(reds)

    def body(*refs):
        ins, (send_sems, recv_sems) = refs[:nb], refs[2 * nb:]
        x, y, c, _ = _place()

        def copy(i, cc):
            rh = reds[i].shape[0] // 2
            rows = ins[i].at[pl.ds(cc * rh, rh), :]
            return pltpu.make_async_remote_copy(src_ref=rows, dst_ref=rows, send_sem=send_sems.at[i],
                                                recv_sem=recv_sems.at[i], device_id=(x, y, 1 - c), device_id_type=MESH)

        cps = [copy(i, c) for i in range(nb)]
        for cp in cps:
            cp.start()
        for i, cp in enumerate(cps):
            cp.wait_send()
            copy(i, 1 - c).wait_recv()

    return pl.pallas_call(
        body, name=name, in_specs=[_ANY] * nb, out_specs=[_ANY] * nb,
        out_shape=[jax.ShapeDtypeStruct(r.shape, r.dtype) for r in reds],
        input_output_aliases={i: i for i in range(nb)},
        scratch_shapes=[pltpu.SemaphoreType.DMA((nb,)), pltpu.SemaphoreType.DMA((nb,))],
    )(*reds)


def allreduce_small(part, *, name):
    m_per, n = part.shape

    def body(x_ref, sum_ref, all_ref, send_sems, recv_sems, local_sem):
        x, y, c, chips = _place()
        me, sib = (x, y, c), (x, y, 1 - c)

        def rows(px, py, pc):
            return all_ref.at[pl.ds((4 * px + 2 * py + pc) * m_per, m_per), :]

        def copy(kk, block, to, src=None):
            return pltpu.make_async_remote_copy(src_ref=rows(*block) if src is None else src, dst_ref=rows(*block),
                                                send_sem=send_sems.at[kk], recv_sem=recv_sems.at[kk],
                                                device_id=to, device_id_type=MESH)

        mine = pltpu.make_async_copy(x_ref, rows(*me), local_sem)
        mine.start()
        first = [copy(0, me, sib, src=x_ref)]
        first += [copy(1 + kk, me, (*chip, c), src=x_ref) for kk, chip in enumerate(chips)]
        for cp in first:
            cp.start()
        passed = [copy(4 + kk, (*chip, c), sib) for kk, chip in enumerate(chips)]
        for kk, chip in enumerate(chips):
            copy(1 + kk, (*chip, c), me).wait_recv()
            passed[kk].start()
        copy(0, sib, me).wait_recv()
        for kk, chip in enumerate(chips):
            copy(4 + kk, (*chip, 1 - c), me).wait_recv()
        for cp in first + passed:
            cp.wait_send()
        mine.wait()
        acc = all_ref[0:m_per, :]
        for d in range(1, 8):
            acc = acc + all_ref[d * m_per:(d + 1) * m_per, :]
        sum_ref[...] = acc

    vmem = pl.BlockSpec(memory_space=pltpu.VMEM)
    return pl.pallas_call(
        body, name=name, in_specs=[vmem], out_specs=vmem,
        out_shape=jax.ShapeDtypeStruct((m_per, n), part.dtype),
        scratch_shapes=[pltpu.VMEM((8 * m_per, n), part.dtype), pltpu.SemaphoreType.DMA((7,)),
                        pltpu.SemaphoreType.DMA((7,)), pltpu.SemaphoreType.DMA],
    )(part)


def adamw(w, g, m, v, *, name):
    r, c = w.shape
    tm = r if r * c * 4 <= (1 << 20) else _tile(r, max(8, ((1 << 20) // (c * 4)) // 8 * 8), unit=8)
    bc1, bc2 = 1.0 - ADAM_B1 ** ADAM_STEP, 1.0 - ADAM_B2 ** ADAM_STEP

    def fn(i, n, rv, cv, sc):
        w_, g_, m_, v_ = rv
        m_ = ADAM_B1 * m_ + (1.0 - ADAM_B1) * g_
        v_ = ADAM_B2 * v_ + (1.0 - ADAM_B2) * (g_ * g_)
        delta = -ADAM_LR * ((m_ / bc1) / (jnp.sqrt(v_ / bc2) + ADAM_EPS) + ADAM_WD * w_)
        return [delta, m_, v_], []

    return rowcall(fn, [w, g, m, v], [], [(c, F32)] * 3, [], tm=tm, name=name)[0]


def _head_one_hot(c):
    e = (lax.broadcasted_iota(jnp.int32, (c, LANES), 0) // HEAD
         == lax.broadcasted_iota(jnp.int32, (c, LANES), 1)).astype(F32)
    return e, e.T


def _join_cols(g):
    return jnp.concatenate([g[j] for j in range(4)], axis=1)


def _split_cols(a):
    return jnp.stack(jnp.split(a, 4, axis=1))


LATE = ("w_out", "xattn_wq", "xattn_wk", "xattn_wv", "xattn_wo", "mlp_w1", "mlp_w2")


def reduce_over_core_pair(names, g_list):
    gots = sibling_swap_multi(g_list, name=f"rs_sibling_swap_{names[0]}")
    return [pair_sum(g, got, tr=_half_tile(g.shape[1] // 2, g.shape[2]), name=f"rs_pair_sum_{n}")
            for n, g, got in zip(names, g_list, gots)]


def layer_step(x, mem, tgt, wg, sp, late_slots=None):
    s, d = x.shape
    dr, dc = sp["rwkv_w0"].shape[1], sp["conv_b"].shape[1]
    n_lora = N_DECAY + N_ICLR + N_GATE
    n_rwkv = 3 * dr + n_lora
    pad_l = N_LORA_PAD - n_lora
    w_in = _join_cols(wg["w_in"])
    cuts = [0, dr, 2 * dr, 3 * dr, n_rwkv, n_rwkv + dc, n_rwkv + 2 * dc, n_rwkv + 2 * dc + d, n_rwkv + 2 * dc + 2 * d]
    w_r, w_k, w_v, w_l, w_ca, w_cb, w_gr, w_gc = (w_in[:, lo:hi] for lo, hi in zip(cuts[:-1], cuts[1:]))
    w_l = jnp.pad(w_l, ((0, 0), (0, pad_l)))
    sm = sp["rwkv_shift_mix"]
    mus = [sm[:, 0:dr], sm[:, dr:2 * dr], sm[:, 2 * dr:3 * dr], jnp.pad(sm[:, 3 * dr:], ((0, 0), (0, pad_l)))]
    w_up = jnp.pad(_join_cols(wg["rwkv_w_up"]).astype(F32), ((0, LANES - N_DECAY), (0, 0)))
    a_up = jnp.pad(_join_cols(wg["rwkv_a_up"]).astype(F32), ((N_DECAY, 2 * LANES - N_DECAY - N_ICLR), (0, 0)))
    g_lo = N_DECAY + N_ICLR - LANES
    g_up = jnp.pad(_join_cols(wg["rwkv_g_up"]).astype(F32), ((g_lo, pad_l), (0, 0)))
    conv_w = _join_cols(wg["conv_w"])
    row_sharded = ("w_out", "xattn_wq", "xattn_wk", "xattn_wv", "xattn_wo", "mlp_w2")
    e, et = _head_one_hot(dr)
    pre_p = [sp["rwkv_w0"], w_up, sp["rwkv_a0"], a_up, g_up, sp["rwkv_k_k"], sp["rwkv_k_a"]]
    post_p = [sp["rwkv_r_k"], sp["rwkv_gn_g"], sp["rwkv_gn_b"]]
    pairs = dr // PAIR
    tm = min(s, 128)
    tmm = mem.shape[0]

    z_r, z_k, z_v, z_l = (mm(x, w, name=f"z_{n}") for n, w in zip("rkvl", (w_r, w_k, w_v, w_l)))
    z_ca, z_cb = mm(x, w_ca, name="z_ca"), mm(x, w_cb, name="z_cb")
    z_gr, z_gc = mm(x, w_gr, name="z_gr"), mm(x, w_gc, name="z_gc")
    zs_r, zs_k, zs_v, zs_l = tokenshift_fwd([z_r, z_k, z_v, z_l], mus, tm=tm, name="shift_fwd")
    pre_o = [(dr, F32)] * 5
    k_m, lw, a_s, b_s, g = stage_fwd(f_rwkv_pre, [zs_k, zs_l], pre_p, [e, et], pre_o, tm=min(s, 64), name="pre_fwd")
    (o, ck, tpost, sa_rows), gathered = scan_fwd(zs_r, lw, k_m, zs_v, a_s, b_s, pg=min(8, pairs), name="scan_fwd",
                                                 gather=[late_slots[n] for n in LATE] if late_slots else ())
    wg = dict(wg, **dict(zip(LATE, gathered)))
    wt = {n: wg[n].reshape(-1, wg[n].shape[2]) for n in row_sharded}
    post_r = [o, zs_r, k_m, zs_v, g]
    (o_r,) = stage_fwd(f_rwkv_post, post_r, post_p, [e, et], [(dr, BF16)], tm=min(s, 64), name="post_fwd")
    (u,) = stage_fwd(f_glu, [z_ca, z_cb], [], [], [(dc, F32)], tm=tm, name="glu_fwd")
    cv = conv_fwd(u, conv_w, sp["conv_b"], tm=tm, name="conv_fwd")
    cln_p = [sp["conv_ln_g"], sp["conv_ln_b"]]
    (o_c,) = stage_fwd(f_convln, [cv], cln_p, [], [(dc, BF16)], tm=tm, name="convln_fwd")
    p_r = mm(o_r, wg["proj_rwkv"], b_sh=True, name="proj_r")
    p_c = mm(o_c, wg["proj_conv"], b_sh=True, name="proj_c")
    (merged,) = stage_fwd(f_merge, [z_gr, z_gc, p_r, p_c], [], [], [(d, BF16)], tm=tm, name="merge_fwd")
    y1 = mm(merged, wt["w_out"], name="y1")
    ln1_p, ln2_p, lnm_p = ([sp[f"{n}_g"], sp[f"{n}_b"]] for n in ("ln1", "ln2", "ln_mem"))
    (h1,) = stage_fwd(f_resln, [x, y1], ln1_p, [], [(d, F32)], tm=tm, name="ln1_fwd")
    (mem_n,) = stage_fwd(f_ln, [mem], lnm_p, [], [(d, F32)], tm=tmm, name="lnmem_fwd")
    k_mem, v_mem = mm(mem_n, wt["xattn_wk"], name="k_mem"), mm(mem_n, wt["xattn_wv"], name="v_mem")
    q = mm(h1, wt["xattn_wq"], name="q")
    (ao,) = stage_fwd(f_attn, [q], [k_mem, v_mem], [], [(d, BF16)], tm=tm, name="attn_fwd")
    ca = mm(ao, wt["xattn_wo"], name="ca")
    (h2,) = stage_fwd(f_resln, [h1, ca], ln2_p, [], [(d, F32)], tm=tm, name="ln2_fwd")
    u1 = mm(h2, wg["mlp_w1"], b_sh=True, name="u1")
    f_dim = u1.shape[1]
    tmf = min(s, 64)
    (act,) = stage_fwd(f_relu2, [u1], [], [], [(f_dim, BF16)], tm=tmf, name="relu2_fwd")
    ff = mm(act, wt["mlp_w2"], name="ff")

    gw, gs = {}, {}
    loss, dh2, dff, gs["ln3_g"], gs["ln3_b"] = loss_bwd(h2, ff, tgt, sp["ln3_g"], sp["ln3_b"], tm=tm, name="loss_bwd")
    gw["mlp_w2"] = mm(act, dff, ta=True, out_dtype=BF16, name="g_mlp_w2")
    dact = mm(dff, wt["mlp_w2"], tb=True, name="d_act")
    (du1,), _ = stage_bwd(f_relu2, [u1], [], [], [dact], [(0, BF16)], tm=tmf, name="relu2_bwd")
    gw["mlp_w1"] = mm(h2, du1, ta=True, out_dtype=BF16, out_sh=True, name="g_mlp_w1")
    dh2 = mm(du1, wg["mlp_w1"], tb=True, b_sh=True, acc=dh2, name="d_h2")
    (dh1, dca), (gs["ln2_g"], gs["ln2_b"]) = stage_bwd(f_resln, [h1, ca], ln2_p, [], [dh2], [(0, F32), (1, F32)],
                                                       tm=tm, name="ln2_bwd")
    gw["xattn_wo"] = mm(ao, dca, ta=True, out_dtype=BF16, name="g_wo")
    dao = mm(dca, wt["xattn_wo"], tb=True, name="d_ao")
    (dq,), (dk_mem, dv_mem) = stage_bwd(f_attn, [q], [k_mem, v_mem], [], [dao], [(0, F32)], tm=tm, name="attn_bwd")
    gw["xattn_wq"] = mm(h1, dq, ta=True, out_dtype=BF16, name="g_wq")
    dh1 = mm(dq, wt["xattn_wq"], tb=True, acc=dh1, name="d_h1")
    gw["xattn_wk"] = mm(mem_n, dk_mem, ta=True, out_dtype=BF16, name="g_wk")
    gw["xattn_wv"] = mm(mem_n, dv_mem, ta=True, out_dtype=BF16, name="g_wv")
    dmem_n = mm(dk_mem, wt["xattn_wk"], tb=True, name="d_memn_k")
    dmem_n = mm(dv_mem, wt["xattn_wv"], tb=True, acc=dmem_n, name="d_memn_v")
    _, (gs["ln_mem_g"], gs["ln_mem_b"]) = stage_bwd(f_ln, [mem], lnm_p, [], [dmem_n], [], tm=tmm, name="lnmem_bwd")
    (dx, dy1), (gs["ln1_g"], gs["ln1_b"]) = stage_bwd(f_resln, [x, y1], ln1_p, [], [dh1], [(0, F32), (1, F32)],
                                                      tm=tm, name="ln1_bwd")
    gw["w_out"] = mm(merged, dy1, ta=True, out_dtype=BF16, name="g_w_out")
    dmerged = mm(dy1, wt["w_out"], tb=True, name="d_merged")
    (dz_gr, dz_gc, dp_r, dp_c), _ = stage_bwd(f_merge, [z_gr, z_gc, p_r, p_c], [], [], [dmerged],
                                              [(0, BF16), (1, BF16), (2, F32), (3, F32)], tm=tm, name="merge_bwd")
    gw["proj_rwkv"] = mm(o_r, dp_r, ta=True, out_dtype=BF16, out_sh=True, name="g_proj_r")
    gw["proj_conv"] = mm(o_c, dp_c, ta=True, out_dtype=BF16, out_sh=True, name="g_proj_c")
    do_r = mm(dp_r, wg["proj_rwkv"], tb=True, b_sh=True, name="d_o_r")
    do_c = mm(dp_c, wg["proj_conv"], tb=True, b_sh=True, name="d_o_c")
    (dcv,), (gs["conv_ln_g"], gs["conv_ln_b"]) = stage_bwd(f_convln, [cv], cln_p, [], [do_c], [(0, F32)],
                                                           tm=tm, name="convln_bwd")
    du, g_conv_w, gs["conv_b"] = conv_bwd(dcv, u, conv_w, tm=tm, name="conv_bwd")
    gw["conv_w"] = _split_cols(g_conv_w.astype(BF16))
    (dz_ca, dz_cb), _ = stage_bwd(f_glu, [z_ca, z_cb], [], [], [du], [(0, BF16), (1, BF16)], tm=tm, name="glu_bwd")
    (d_o, dr_p, dk_p, dv_p, dg), (gs["rwkv_r_k"], gs["rwkv_gn_g"], gs["rwkv_gn_b"]) = stage_bwd(
        f_rwkv_post, post_r, post_p, [e, et], [do_r], [(k, F32) for k in range(5)], tm=min(s, 64), name="post_bwd")
    for n in row_sharded:
        gw[n] = gw[n].reshape(wg[n].shape)
    pss = reduce_over_core_pair(LATE, [gw[n] for n in LATE]) if late_slots else []
    (dzs_r, dlw, dk_m, dzs_v, da_s, db_s), sent = scan_bwd(zs_r, lw, k_m, zs_v, a_s, b_s, ck, tpost, sa_rows, d_o, dr_p,
                                                           dk_p, dv_p, pg=min(8, pairs), name="scan_bwd", scatter=pss)
    (dzs_k, dzs_l), pre_g = stage_bwd(f_rwkv_pre, [zs_k, zs_l], pre_p, [e, et], [dk_m, dlw, da_s, db_s, dg],
                                      [(0, F32), (1, F32)], tm=min(s, 64), name="pre_bwd")
    gs["rwkv_w0"], g_w_up, gs["rwkv_a0"], g_a_up, g_g_up, gs["rwkv_k_k"], gs["rwkv_k_a"] = pre_g
    gw["rwkv_w_up"] = _split_cols(g_w_up[0:N_DECAY].astype(BF16))
    gw["rwkv_a_up"] = _split_cols(g_a_up[N_DECAY:N_DECAY + N_ICLR].astype(BF16))
    gw["rwkv_g_up"] = _split_cols(g_g_up[g_lo:g_lo + N_GATE].astype(BF16))
    dzs, dmus = tokenshift_bwd([dzs_r, dzs_k, dzs_v, dzs_l], [z_r, z_k, z_v, z_l], mus, tm=tm, name="shift_bwd")
    gs["rwkv_shift_mix"] = jnp.concatenate(list(dmus[:3]) + [dmus[3][:, 0:n_lora]], axis=1)
    dzs = list(dzs) + [dz_ca, dz_cb, dz_gr, dz_gc]
    g_in = []
    for n, dz, w in zip(("r", "k", "v", "l", "ca", "cb", "gr", "gc"), dzs, (w_r, w_k, w_v, w_l, w_ca, w_cb, w_gr, w_gc)):
        g_in.append(mm(x, dz, ta=True, out_dtype=BF16, name=f"g_w_{n}"))
        dx = mm(dz, w, tb=True, acc=dx, name=f"d_x_{n}")
    g_in[3] = g_in[3][:, 0:n_lora]
    gw["w_in"] = _split_cols(jnp.concatenate(g_in, axis=1))
    return loss, dx, gw, gs, ((pss, sent) if late_slots else None)


WEIGHTS = ["w_in", "rwkv_shift_mix", "rwkv_w0", "rwkv_w_up", "rwkv_a0", "rwkv_a_up", "rwkv_g_up", "rwkv_k_k",
           "rwkv_k_a", "rwkv_r_k", "rwkv_gn_g", "rwkv_gn_b", "conv_w", "conv_b", "conv_ln_g", "conv_ln_b",
           "proj_rwkv", "proj_conv", "w_out", "ln1_g", "ln1_b", "ln_mem_g", "ln_mem_b", "xattn_wq", "xattn_wk",
           "xattn_wv", "xattn_wo", "ln2_g", "ln2_b", "mlp_w1", "mlp_w2", "ln3_g", "ln3_b"]
SHARD_AXIS = {"w_in": 1, "rwkv_w_up": 1, "rwkv_a_up": 1, "rwkv_g_up": 1, "conv_w": 1, "proj_rwkv": 1, "proj_conv": 1,
              "w_out": 0, "xattn_wq": 0, "xattn_wk": 0, "xattn_wv": 0, "xattn_wo": 0, "mlp_w1": 1, "mlp_w2": 0}


def _unpack(flat, shapes):
    out, off = [], 0
    for shp in shapes:
        n = 1
        for dim in shp:
            n *= dim
        out.append(flat[..., off:off + n].reshape(flat.shape[:-1] + tuple(shp)))
        off += n
    return out


def _half_tile(rh, w):
    return _tile(rh, max(16, (2 << 20) // (w * 4) // 16 * 16), unit=16)


def kernel(*args):
    n_w = len(WEIGHTS)
    x, mem = args[0][0], args[1][0]
    tgt = args[2 + n_w][0]
    w_loc = {n: a for n, a in zip(WEIGHTS, args[2:2 + n_w])}
    m_loc = {n: a for n, a in zip(WEIGHTS, args[3 + n_w:3 + 2 * n_w])}
    v_loc = {n: a for n, a in zip(WEIGHTS, args[3 + 2 * n_w:3 + 3 * n_w])}
    big = [n for n in WEIGHTS if n in SHARD_AXIS]
    small = [n for n in WEIGHTS if n not in SHARD_AXIS]

    def as2d(n, a):
        if n in SHARD_AXIS:
            return a.reshape(a.shape[1], a.shape[-1])
        return a.reshape(1, -1)

    loc2d = {n: as2d(n, w_loc[n]) for n in WEIGHTS}
    conv_rows = loc2d["conv_w"].shape[0]
    slots = []
    for n in big:
        if n == "conv_w":
            padded = jnp.pad(loc2d[n], ((0, CONV_HALO - conv_rows), (0, 0)))
            slots.append(into_slot(padded, F32, name=f"slot_{n}"))
        else:
            slots.append(into_slot(loc2d[n], BF16, name=f"slot_{n}"))
    slots = dict(zip(big, slots))
    early = [n for n in big if n not in LATE]
    wg = dict(zip(early, allgather_multi([slots[n] for n in early], name="allgather_weights")))
    sp = {n: loc2d[n] for n in small}

    loss_part, grad_x, gw, gs, (pss_late, sent_late) = layer_step(x, mem, tgt, wg, sp, {n: slots[n] for n in LATE})

    pss = reduce_over_core_pair(early, [gw[n] for n in early])
    sent = list(scatter_multi(pss, name="rs_scatter"))
    order = early + list(LATE)
    reds = [chip_sum(ps, got, tr=_half_tile(ps.shape[1], ps.shape[2]), name=f"rs_chip_sum_{n}")
            for n, ps, got in zip(order, pss + pss_late, sent + sent_late)]
    reds = sibling_join_multi(reds, name="rs_sibling_join")
    g_big = {n: (r[0:conv_rows] if n == "conv_w" else r) for n, r in zip(order, reds)}

    small_parts = [gs[n] for n in small] + [loss_part[0:1, 0:1]]
    flat = jnp.concatenate([p.reshape(-1).astype(F32) for p in small_parts])
    flat = jnp.pad(flat, (0, -flat.shape[0] % (8 * LANES))).reshape(-1, LANES)
    red = allreduce_small(flat, name="allreduce_small").reshape(-1)
    g_small = dict(zip(small, _unpack(red, [loc2d[n].shape for n in small])))
    n_small = sum(loc2d[n].shape[1] for n in small)
    loss = red[n_small]

    grads, deltas, new_m, new_v = {}, {}, {}, {}
    for n in big:
        g2 = g_big[n]
        d2, m2, v2 = adamw(loc2d[n], g2, as2d(n, m_loc[n]), as2d(n, v_loc[n]), name=f"adamw_{n}")
        shp = w_loc[n].shape
        grads[n], deltas[n], new_m[n], new_v[n] = (t.reshape(shp) for t in (g2, d2, m2, v2))

    def small_pack(d):
        f = jnp.concatenate([as2d(n, d[n]).reshape(-1) for n in small])
        return jnp.pad(f, (0, -f.shape[0] % (8 * LANES))).reshape(-1, LANES)

    g_pack = small_pack({n: g_small[n] for n in small})
    outs = adamw(small_pack(w_loc), g_pack, small_pack(m_loc), small_pack(v_loc), name="adamw_small")
    for dst, packed in zip((deltas, new_m, new_v), outs):
        for n, t in zip(small, _unpack(packed.reshape(-1), [loc2d[n].shape for n in small])):
            dst[n] = t.reshape(w_loc[n].shape)
    for n in small:
        grads[n] = g_small[n].reshape(w_loc[n].shape)

    return (loss, grad_x[None], *[grads[n] for n in WEIGHTS], *[deltas[n] for n in WEIGHTS],
            *[new_m[n] for n in WEIGHTS], *[new_v[n] for n in WEIGHTS])
```

```python
import functools

import jax
import jax.numpy as jnp
from jax import lax
from jax.experimental import pallas as pl
from jax.experimental.pallas import tpu as pltpu

F32, BF16 = jnp.float32, jnp.bfloat16
ALPHA = 2.0 ** 0.25
LN_EPS = 1e-5
GN_EPS = 64e-5
HEAD = 64
LANES = 128
PAIR = 2 * HEAD
XATTN_HEADS = 4
CONV_WIDTH = 31
CONV_HALO = 32
N_DECAY, N_ICLR, N_GATE = 96, 96, 256
N_LORA_PAD = 512
VMEM_LIMIT = 56 * 1024 * 1024
ADAM_LR, ADAM_B1, ADAM_B2, ADAM_EPS, ADAM_WD, ADAM_STEP = 0.001, 0.9, 0.999, 1e-8, 0.01, 10
MESH = pl.DeviceIdType.MESH
HI = lax.Precision.HIGHEST


def _params(**kw):
    return pltpu.CompilerParams(vmem_limit_bytes=VMEM_LIMIT, **kw)


def _tile(n, pref, unit=LANES):
    if n <= pref:
        return n
    t = pref
    while n % t:
        t -= unit
    return t


def mm(a, b, *, name, ta=False, tb=False, out_dtype=F32, acc=None, b_sh=False, out_sh=False):
    m, k = (a.shape[1], a.shape[0]) if ta else a.shape
    if b_sh:
        n = b.shape[1] if tb else 4 * b.shape[2]
    else:
        n = b.shape[0] if tb else b.shape[1]
    tm, tn, tk = _tile(m, 1024), _tile(n // 4 if (b_sh and not tb) or out_sh else n, 1024), _tile(k // 4 if b_sh and tb else k, 1024)
    nk = k // tk
    a_spec = pl.BlockSpec((tk, tm), lambda i, j, kk: (kk, i)) if ta else pl.BlockSpec((tm, tk), lambda i, j, kk: (i, kk))
    if b_sh and tb:
        per_k = k // 4 // tk
        b_spec = pl.BlockSpec((None, tn, tk), lambda i, j, kk: (kk // per_k, j, kk % per_k))
    elif b_sh:
        per_n = n // 4 // tn
        b_spec = pl.BlockSpec((None, tk, tn), lambda i, j, kk: (j // per_n, kk, j % per_n))
    else:
        b_spec = pl.BlockSpec((tn, tk), lambda i, j, kk: (j, kk)) if tb else pl.BlockSpec((tk, tn), lambda i, j, kk: (kk, j))
    if out_sh:
        per_o = n // 4 // tn
        o_spec = pl.BlockSpec((None, tm, tn), lambda i, j, kk: (j // per_o, i, j % per_o))
    else:
        o_spec = pl.BlockSpec((tm, tn), lambda i, j, kk: (i, j))
    dims = (((0 if ta else 1,), (1 if tb else 0,)), ((), ()))

    def body(*refs):
        if acc is None:
            a_ref, b_ref, o_ref, acc_ref = refs
        else:
            a_ref, b_ref, c_ref, o_ref, acc_ref = refs
        kk = pl.program_id(2)

        @pl.when(kk == 0)
        def _():
            acc_ref[...] = jnp.zeros_like(acc_ref) if acc is None else c_ref[...].astype(F32)

        acc_ref[...] += lax.dot_general(a_ref[...].astype(BF16), b_ref[...].astype(BF16), dims,
                                        preferred_element_type=F32)

        @pl.when(kk == nk - 1)
        def _():
            o_ref[...] = acc_ref[...].astype(o_ref.dtype)

    ins = [a, b] + ([] if acc is None else [acc])
    in_specs = [a_spec, b_spec] + ([] if acc is None else [o_spec])
    return pl.pallas_call(
        body, name=name, grid=(m // tm, n // tn, nk),
        in_specs=in_specs, out_specs=o_spec,
        out_shape=jax.ShapeDtypeStruct((4, m, n // 4) if out_sh else (m, n), out_dtype),
        scratch_shapes=[pltpu.VMEM((tm, tn), F32)],
        input_output_aliases={} if acc is None else {2: 0},
        compiler_params=_params(dimension_semantics=("arbitrary", "arbitrary", "arbitrary")),
    )(*ins)


def rowcall(fn, rows, consts, out_rows, out_accs, *, tm, name, scratch=()):
    rows = [r if isinstance(r, tuple) else (r, None) for r in rows]
    s = rows[0][0].shape[0]
    n = s // tm
    in_specs, ins = [], []
    for arr, halo in rows:
        w = arr.shape[1]
        in_specs.append(pl.BlockSpec((tm, w), lambda i: (i, 0)))
        ins.append(arr)
        if halo is not None:
            kind, h = halo
            per = tm // h
            if kind == "prev":
                in_specs.append(pl.BlockSpec((h, w), lambda i, per=per: (jnp.maximum(i * per - 1, 0), 0)))
            else:
                in_specs.append(pl.BlockSpec((h, w), lambda i, per=per, last=s // h - 1: (jnp.minimum((i + 1) * per, last), 0)))
            ins.append(arr)
    for cst in consts:
        in_specs.append(pl.BlockSpec(cst.shape, lambda i, nd=cst.ndim: (0,) * nd))
        ins.append(cst)
    out_specs = [pl.BlockSpec((tm, w), lambda i: (i, 0)) for w, _ in out_rows]
    out_specs += [pl.BlockSpec(shp, lambda i, nd=len(shp): (0,) * nd) for shp, _ in out_accs]
    out_shape = [jax.ShapeDtypeStruct((s, w), dt) for w, dt in out_rows]
    out_shape += [jax.ShapeDtypeStruct(shp, dt) for shp, dt in out_accs]
    n_in, n_or, n_oa = len(ins), len(out_rows), len(out_accs)

    def body(*refs):
        i = pl.program_id(0)
        it = iter(refs[:n_in])
        row_vals = []
        for _, halo in rows:
            cur = next(it)[...]
            row_vals.append(cur if halo is None else (cur, next(it)[...]))
        const_vals = [r[...] for r in it]
        o_refs = refs[n_in:n_in + n_or]
        a_refs = refs[n_in + n_or:n_in + n_or + n_oa]
        outs, parts = fn(i, n, row_vals, const_vals, refs[n_in + n_or + n_oa:])
        for o_ref, val in zip(o_refs, outs, strict=True):
            o_ref[...] = val.astype(o_ref.dtype)
        for a_ref, part in zip(a_refs, parts, strict=True):
            part = jnp.broadcast_to(part, a_ref.shape).astype(a_ref.dtype)

            @pl.when(i == 0)
            def _(a_ref=a_ref, part=part):
                a_ref[...] = part

            @pl.when(i > 0)
            def _(a_ref=a_ref, part=part):
                a_ref[...] += part

    res = pl.pallas_call(
        body, name=name, grid=(n,), in_specs=in_specs, out_specs=out_specs, out_shape=out_shape,
        scratch_shapes=list(scratch),
        compiler_params=_params(dimension_semantics=("arbitrary",)),
    )(*ins)
    return res[:n_or], res[n_or:]


def stage_fwd(f, rows, params, consts, outs, *, tm, name):
    n_p = len(params)

    def fn(i, n, rv, cv, sc):
        return f([r.astype(F32) for r in rv], cv[:n_p], cv[n_p:]), []

    return rowcall(fn, rows, list(params) + list(consts), outs, [], tm=tm, name=name)[0]


def stage_bwd(f, rows, params, consts, cts, row_grads, *, tm, name):
    n_r, n_p = len(rows), len(params)

    def fn(i, n, rv, cv, sc):
        r = [x.astype(F32) for x in rv[:n_r]]
        ct = [x.astype(F32) for x in rv[n_r:]]
        _, vjp = jax.vjp(lambda r_, p_: f(r_, p_, cv[n_p:]), r, list(cv[:n_p]))
        d_r, d_p = vjp(ct)
        return [d_r[k] for k, _ in row_grads], d_p

    return rowcall(fn, list(rows) + list(cts), list(params) + list(consts),
                   [(rows[k].shape[1], dt) for k, dt in row_grads],
                   [(p.shape, F32) for p in params], tm=tm, name=name)


def _ln(x, g, b, eps=LN_EPS):
    xc = x - jnp.mean(x, -1, keepdims=True)
    var = jnp.mean(xc * xc, -1, keepdims=True)
    return xc * lax.rsqrt(var + eps) * g + b


def _sigmoid(x):
    return 1.0 / (1.0 + jnp.exp(-x))


def _softplus(x):
    return jnp.maximum(x, 0.0) + jnp.log(1.0 + jnp.exp(-jnp.abs(x)))


def _bdot(a, b):
    return jnp.dot(a.astype(BF16), b.astype(BF16), preferred_element_type=F32)


def _head_sum(x, e, et):
    return jnp.dot(jnp.dot(x, e, precision=HI, preferred_element_type=F32), et, precision=HI,
                   preferred_element_type=F32)


def f_rwkv_pre(rows, params, consts):
    zk, zl = rows
    w0, w_up, a0, a_up, g_up, k_k, k_a = params
    e, et = consts
    w = -_softplus(-(w0 + _bdot(jnp.tanh(zl[:, 0:LANES]), w_up))) - 0.5
    log_decay = -jnp.exp(w)
    a = _sigmoid(a0 + _bdot(zl[:, 0:2 * LANES], a_up))
    g = _bdot(_sigmoid(zl[:, LANES:N_LORA_PAD]), g_up)
    kk = zk * k_k
    kk = kk / jnp.maximum(jnp.sqrt(_head_sum(kk * kk, e, et)), 1e-12)
    return [zk * (1.0 + (a - 1.0) * k_a), log_decay, -kk, kk * a, g]


def f_rwkv_post(rows, params, consts):
    o, r, k, v, g = rows
    r_k, gn_g, gn_b = params
    e, et = consts
    oc = o - _head_sum(o, e, et) * (1.0 / HEAD)
    var = _head_sum(oc * oc, e, et) * (1.0 / HEAD)
    on = oc * lax.rsqrt(var + GN_EPS) * gn_g + gn_b
    return [(on + _head_sum(r * k * r_k, e, et) * v) * g]


def f_glu(rows, params, consts):
    return [rows[0] * _sigmoid(rows[1])]


def f_convln(rows, params, consts):
    x = _ln(rows[0], params[0], params[1])
    return [x * _sigmoid(x)]


def f_merge(rows, params, consts):
    gr, gc, pr, pc = rows
    return [_sigmoid(gr) * pr + _sigmoid(gc) * pc]


def f_resln(rows, params, consts):
    return [_ln(ALPHA * rows[0] + rows[1], params[0], params[1])]


def f_resln_twice(rows, params, consts):
    y = _ln(ALPHA * rows[0] + rows[1], params[0], params[1])
    return [y, y]


def f_ln(rows, params, consts):
    return [_ln(rows[0], params[0], params[1])]


def f_attn(rows, params, consts):
    q, (k, v) = rows[0], params
    dh = q.shape[1] // XATTN_HEADS
    outs = []
    for h in range(XATTN_HEADS):
        sl = slice(h * dh, (h + 1) * dh)
        s = lax.dot_general(q[:, sl].astype(BF16), k[:, sl].astype(BF16), (((1,), (1,)), ((), ())),
                            preferred_element_type=F32) * dh ** -0.5
        p = jnp.exp(s - jnp.max(s, -1, keepdims=True))
        p = p / jnp.sum(p, -1, keepdims=True)
        outs.append(_bdot(p, v[:, sl]))
    return [jnp.concatenate(outs, axis=-1)]


def f_relu2(rows, params, consts):
    return [jnp.square(jnp.maximum(rows[0], 0.0))]


def loss_bwd(h, y, tgt, g, b, *, tm, name):
    def fn(i, n, rv, cv, sc):
        def loss(h_, y_, g_, b_):
            err = _ln(ALPHA * h_ + y_, g_, b_) - rv[2]
            return 0.5 * jnp.sum(jnp.mean(err * err, -1, keepdims=True))
        val, vjp = jax.vjp(loss, rv[0], rv[1], cv[0], cv[1])
        dh, dy, dg, db = vjp(jnp.ones((), F32))
        return [dh, dy], [val.reshape(1, 1), dg, db]

    w = h.shape[1]
    (dh, dy), (val, dg, db) = rowcall(fn, [h, y, tgt], [g, b], [(w, F32), (w, BF16)],
                                      [((8, LANES), F32), (g.shape, F32), (b.shape, F32)], tm=tm, name=name)
    return val, dh, dy, dg, db


def _shift_down(cur, halo, first):
    rolled = pltpu.roll(cur, 1, 0)
    row0 = jnp.where(first, 0.0, halo[halo.shape[0] - 1:, :])
    return jnp.where(lax.broadcasted_iota(jnp.int32, cur.shape, 0) == 0, row0, rolled)


def _shift_up(cur, halo, last):
    rolled = pltpu.roll(cur, cur.shape[0] - 1, 0)
    rown = jnp.where(last, 0.0, halo[0:1, :])
    return jnp.where(lax.broadcasted_iota(jnp.int32, cur.shape, 0) == cur.shape[0] - 1, rown, rolled)


def tokenshift_fwd(zs, mus, *, tm, name):
    def fn(i, n, rv, cv, sc):
        return [z + (_shift_down(z, halo, i == 0) - z) * mu for (z, halo), mu in zip(rv, cv)], []

    return rowcall(fn, [(z, ("prev", 8)) for z in zs], mus, [(z.shape[1], F32) for z in zs], [], tm=tm, name=name)[0]


def tokenshift_bwd(dzs, zs, mus, *, tm, name):
    nz = len(zs)

    def fn(i, n, rv, cv, sc):
        outs, parts = [], []
        for (dz, dnext), (z, zprev), mu in zip(rv[:nz], rv[nz:], cv):
            g = dz * mu
            outs.append(dz - g + _shift_up(g, dnext * mu, i == n - 1))
            parts.append(jnp.sum(dz * (_shift_down(z, zprev, i == 0) - z), 0, keepdims=True))
        return outs, parts

    return rowcall(fn, [(d, ("next", 8)) for d in dzs] + [(z, ("prev", 8)) for z in zs], mus,
                   [(z.shape[1], BF16) for z in zs], [(mu.shape, F32) for mu in mus], tm=tm, name=name)


def conv_fwd(u, w, b, *, tm, name):
    c = u.shape[1]

    def fn(i, n, rv, cv, sc):
        (cur, halo), (ext,) = rv[0], sc
        ext[0:CONV_HALO, :] = jnp.where(i == 0, 0.0, halo)
        ext[CONV_HALO:, :] = cur
        wv = cv[0]
        acc = jnp.broadcast_to(cv[1], cur.shape)
        for j in range(CONV_WIDTH):
            acc = acc + wv[j:j + 1, :] * ext[pl.ds(CONV_HALO - CONV_WIDTH + 1 + j, tm), :]
        return [acc], []

    return rowcall(fn, [(u, ("prev", CONV_HALO))], [w, b], [(c, F32)], [], tm=tm, name=name,
                   scratch=[pltpu.VMEM((tm + CONV_HALO, c), F32)])[0][0]


def conv_bwd(dc, u, w, *, tm, name):
    c = u.shape[1]

    def fn(i, n, rv, cv, sc):
        (dcur, dnext), (ucur, uprev) = rv
        dext, uext, dw_ref = sc
        dext[0:tm, :] = dcur
        dext[tm:, :] = jnp.where(i == n - 1, 0.0, dnext)
        uext[0:CONV_HALO, :] = jnp.where(i == 0, 0.0, uprev)
        uext[CONV_HALO:, :] = ucur
        wv = cv[0]
        du = jnp.zeros_like(dcur)
        dw_ref[...] = jnp.zeros_like(dw_ref)
        for j in range(CONV_WIDTH):
            du = du + wv[j:j + 1, :] * dext[pl.ds(CONV_WIDTH - 1 - j, tm), :]
            dw_ref[j:j + 1, :] = jnp.sum(dcur * uext[pl.ds(CONV_HALO - CONV_WIDTH + 1 + j, tm), :], 0, keepdims=True)
        return [du], [dw_ref[...], jnp.sum(dcur, 0, keepdims=True)]

    (du,), (dw, db) = rowcall(
        fn, [(dc, ("next", CONV_HALO)), (u, ("prev", CONV_HALO))], [w], [(c, F32)],
        [((CONV_HALO, c), F32), ((1, c), F32)], tm=tm, name=name,
        scratch=[pltpu.VMEM((tm + CONV_HALO, c), F32), pltpu.VMEM((tm + CONV_HALO, c), F32),
                 pltpu.VMEM((CONV_HALO, c), F32)])
    return du, dw, db


SCAN_TB = 64


def _parts2(x):
    hi = x.astype(BF16)
    return hi, (x - hi.astype(F32)).astype(BF16)


def _split3(x):
    x1 = x.astype(BF16)
    d1 = x - x1.astype(F32)
    x2 = d1.astype(BF16)
    return x1, x2, (d1 - x2.astype(F32)).astype(BF16)


def _rows3(x):
    hi, lo = _parts2(x)
    return jnp.concatenate([hi, hi, lo], axis=1)


def _bd_parts(t, left):
    def expand(u):
        zero = jnp.zeros_like(u)
        return jnp.concatenate([jnp.where(left, u, zero), jnp.where(left, zero, u)], axis=0)

    hi, lo = _parts2(t)
    return expand(hi), expand(lo)


def _w_nn(parts):
    return jnp.concatenate([parts[0], parts[1], parts[0]], axis=0)


def _w_nt(parts):
    return jnp.concatenate([parts[0], parts[1], parts[0]], axis=1)


def _nn(lhs, w):
    return jnp.dot(lhs, w, preferred_element_type=F32)


def _nt(lhs, w):
    return lax.dot_general(lhs, w, (((1,), (1,)), ((), ())), preferred_element_type=F32)


def _col_const():
    i = lax.broadcasted_iota(jnp.int32, (48, 8 * PAIR), 0) % 16
    n = lax.broadcasted_iota(jnp.int32, (48, 8 * PAIR), 1)
    return ((i % 8 == n // PAIR) & (i // 8 == n % PAIR // HEAD)).astype(BF16)


def _col_tiles(x8, col3):
    xs = jnp.concatenate([x8[:, 0:HEAD], pltpu.roll(x8, HEAD, 1)[:, 0:HEAD]], axis=0)
    return lax.dot_general(jnp.concatenate(_split3(xs), axis=0), col3, (((0,), (0,)), ((), ())),
                           preferred_element_type=F32)


def _scan_steps(pg, left, col, s_ref, refs, rows, on_state, on_out):
    r_ref, lw_ref, k_ref, v_ref, a_ref, b_ref = refs
    r8, k8, a8, b8 = r_ref[rows, :], k_ref[rows, :], a_ref[rows, :], b_ref[rows, :]
    v8, w8 = v_ref[rows, :], jnp.exp(lw_ref[rows, :])
    sub8 = lax.broadcasted_iota(jnp.int32, (8, PAIR), 0)
    ls = [slice(p * PAIR, (p + 1) * PAIR) for p in range(pg)]
    lhs = [_rows3(jnp.concatenate([r8[:, l], a8[:, l]], axis=0)) for l in ls]
    wc, bc, kc = ([_col_tiles(x8[:, l], col) for l in ls] for x8 in (w8, b8, k8))
    sts = [s_ref[p * HEAD:(p + 1) * HEAD, :] for p in range(pg)]
    sas = [_nn(lhs[p], _w_nn(_bd_parts(sts[p], left)))[8:9, :] for p in range(pg)]
    outs = [jnp.zeros((8, PAIR), F32) for _ in range(pg)]
    for j in range(8):
        tile = slice(j * PAIR, (j + 1) * PAIR)
        for p in range(pg):
            sts[p] = sts[p] * wc[p][:, tile] + bc[p][:, tile] * sas[p] + kc[p][:, tile] * v8[j:j + 1, ls[p]]
            on_state(p, j, sts[p], sas[p])
            res = _nn(lhs[p], _w_nn(_bd_parts(sts[p], left)))
            outs[p] = jnp.where(sub8 == j, res[j:j + 1, :], outs[p])
            if j < 7:
                sas[p] = res[9 + j:10 + j, :]
    for p in range(pg):
        s_ref[p * HEAD:(p + 1) * HEAD, :] = sts[p]
        on_out(p, outs[p])


def _place():
    x, y, c = lax.axis_index("x"), lax.axis_index("y"), lax.axis_index("c")
    return x, y, c, [(1 - x, y), (x, 1 - y), (1 - x, 1 - y)]


def _allgather_ops(ins, shapes, send_sems, recv_sems):
    x, y, c, chips = _place()
    sib = (x, y, 1 - c)
    nb = len(ins)

    def copy(i, kk, jj, cc, to):
        rh = shapes[i][1] // 2
        rows = ins[i].at[jj, pl.ds(cc * rh, rh), :]
        return pltpu.make_async_remote_copy(src_ref=rows, dst_ref=rows, send_sem=send_sems.at[6 * i + kk],
                                            recv_sem=recv_sems.at[6 * i + kk], device_id=to, device_id_type=MESH)

    def start():
        for i in range(nb):
            for kk, (cx, cy) in enumerate(chips):
                copy(i, kk, 2 * x + y, c, (cx, cy, c)).start()

    def finish():
        for i in range(nb):
            for kk, (cx, cy) in enumerate(chips):
                copy(i, kk, 2 * cx + cy, c, sib).wait_recv()
                copy(i, 3 + kk, 2 * cx + cy, c, sib).start()
        for i in range(nb):
            for kk, (cx, cy) in enumerate(chips):
                copy(i, 3 + kk, 2 * cx + cy, 1 - c, sib).wait_recv()
        for i in range(nb):
            for kk, (cx, cy) in enumerate(chips):
                copy(i, kk, 2 * x + y, c, (cx, cy, c)).wait_send()
                copy(i, 3 + kk, 2 * cx + cy, c, sib).wait_send()

    return start, finish


def _scatter_ops(ins, outs, send_sems, recv_sems):
    x, y, c, chips = _place()
    cps = [pltpu.make_async_remote_copy(src_ref=ins[i].at[2 * cx + cy], dst_ref=outs[i].at[kk],
                                        send_sem=send_sems.at[3 * i + kk], recv_sem=recv_sems.at[3 * i + kk],
                                        device_id=(cx, cy, c), device_id_type=MESH)
           for i in range(len(ins)) for kk, (cx, cy) in enumerate(chips)]

    def start():
        for cp in cps:
            cp.start()

    def finish():
        for cp in cps:
            cp.wait()

    return start, finish


def scan_fwd(r, lw, k, v, a, b, *, pg, name, gather=()):
    s, c = r.shape
    tb, lw_ = SCAN_TB, PAIR * pg
    ng, nt = c // lw_, s // tb
    blk = pl.BlockSpec((tb, lw_), lambda g, t: (t, g))
    nb = len(gather)

    def body(*refs):
        r_ref, lw_ref, k_ref, v_ref, a_ref, b_ref = refs[:6]
        o_ref, ck_ref, tpost_ref, sa_ref = refs[6 + nb:10 + nb]
        s_ref = refs[10 + 2 * nb]
        first = (pl.program_id(0) == 0) & (pl.program_id(1) == 0)
        last = (pl.program_id(0) == ng - 1) & (pl.program_id(1) == nt - 1)
        if nb:
            start, finish = _allgather_ops(refs[6:6 + nb], [g.shape for g in gather], *refs[11 + 2 * nb:])
            pl.when(first)(start)

        @pl.when(pl.program_id(1) == 0)
        def _():
            s_ref[...] = jnp.zeros_like(s_ref)

        ck_ref[0] = s_ref[...]
        left = lax.broadcasted_iota(jnp.int32, (HEAD, PAIR), 1) < HEAD
        col = _col_const()

        sub8 = lax.broadcasted_iota(jnp.int32, (8, PAIR), 0)

        def step8(t8, carry):
            base = pl.multiple_of(t8 * 8, 8)
            rows = pl.ds(base, 8)
            sa_acc = [jnp.zeros((8, PAIR), F32) for _ in range(pg)]

            def on_state(p, j, st, sa):
                tpost_ref[base + j, p * HEAD:(p + 1) * HEAD, :] = st
                sa_acc[p] = jnp.where(sub8 == j, sa, sa_acc[p])

            def on_out(p, o8):
                o_ref[rows, p * PAIR:(p + 1) * PAIR] = o8
                sa_ref[rows, p * PAIR:(p + 1) * PAIR] = sa_acc[p]

            _scan_steps(pg, left, col, s_ref, (r_ref, lw_ref, k_ref, v_ref, a_ref, b_ref), rows, on_state, on_out)
            return carry

        lax.fori_loop(0, tb // 8, step8, 0)
        if nb:
            pl.when(last)(finish)

    any_spec = pl.BlockSpec(memory_space=pl.ANY)
    res = pl.pallas_call(
        body, name=name, grid=(ng, nt), in_specs=[blk] * 6 + [any_spec] * nb,
        out_specs=[blk, pl.BlockSpec((1, pg * HEAD, PAIR), lambda g, t: (t, g, 0)),
                   pl.BlockSpec((tb, pg * HEAD, PAIR), lambda g, t: (t, g, 0)), blk] + [any_spec] * nb,
        out_shape=[jax.ShapeDtypeStruct((s, c), F32), jax.ShapeDtypeStruct((nt, c // 2, PAIR), F32),
                   jax.ShapeDtypeStruct((s, c // 2, PAIR), F32), jax.ShapeDtypeStruct((s, c), F32)]
        + [jax.ShapeDtypeStruct(g.shape, g.dtype) for g in gather],
        input_output_aliases={6 + i: 4 + i for i in range(nb)},
        scratch_shapes=[pltpu.VMEM((pg * HEAD, PAIR), F32)]
        + ([pltpu.SemaphoreType.DMA((6 * nb,)), pltpu.SemaphoreType.DMA((6 * nb,))] if nb else []),
        compiler_params=_params(dimension_semantics=("arbitrary", "arbitrary")),
    )(r, lw, k, v, a, b, *gather)
    return res[:4], list(res[4:])


def scan_bwd(r, lw, k, v, a, b, ck, tpost, sa, do, dr_in, dk_in, dv_in, *, pg, name, scatter=()):
    s, c = r.shape
    tb, lw_ = SCAN_TB, PAIR * pg
    ng, nt = c // lw_, s // tb
    blk = pl.BlockSpec((tb, lw_), lambda g, t: (nt - 1 - t, g))
    ck_spec = pl.BlockSpec((1, pg * HEAD, PAIR), lambda g, t: (nt - 1 - t, g, 0))
    nb = len(scatter)

    def body(*refs):
        (r_ref, lw_ref, k_ref, v_ref, a_ref, b_ref, ck_ref, tpost_ref, sa_ref, do_ref, dri_ref, dki_ref,
         dvi_ref) = refs[:13]
        dr_ref, dlw_ref, dk_ref, dv_ref, da_ref, db_ref = refs[13 + nb:19 + nb]
        ds_ref, rows_ref = refs[19 + 2 * nb:21 + 2 * nb]
        if nb:
            start, finish = _scatter_ops(refs[13:13 + nb], refs[19 + nb:19 + 2 * nb], *refs[21 + 2 * nb:])
            pl.when((pl.program_id(0) == 0) & (pl.program_id(1) == 0))(start)

        @pl.when(pl.program_id(1) == 0)
        def _():
            ds_ref[...] = jnp.zeros_like(ds_ref)

        left = lax.broadcasted_iota(jnp.int32, (HEAD, PAIR), 1) < HEAD
        col = _col_const()
        ls = [slice(p * PAIR, (p + 1) * PAIR) for p in range(pg)]
        hs = [slice(p * HEAD, (p + 1) * HEAD) for p in range(pg)]
        ones_lhs = jnp.ones((8, PAIR), BF16)

        def state_before(t, h):
            return jnp.where(t == 0, ck_ref[0, h, :], tpost_ref[jnp.maximum(t - 1, 0), h, :])

        def bwd8(i8, carry):
            base = pl.multiple_of((tb // 8 - 1 - i8) * 8, 8)
            rows = pl.ds(base, 8)
            r8, k8, a8, b8 = r_ref[rows, :], k_ref[rows, :], a_ref[rows, :], b_ref[rows, :]
            v8, do8, w8, sa8 = v_ref[rows, :], do_ref[rows, :], jnp.exp(lw_ref[rows, :]), sa_ref[rows, :]
            lhs_kb = [_rows3(jnp.concatenate([k8[:, l], b8[:, l]], axis=0)) for l in ls]
            lhs_vs = [jnp.concatenate([v8[:, l], sa8[:, l]], axis=0).astype(BF16) for l in ls]
            lhs_do = [do8[:, l].astype(BF16) for l in ls]
            rc, wc, ac = ([_col_tiles(x8[:, l], col) for l in ls] for x8 in (r8, w8, a8))
            dss = [ds_ref[h, :] for h in hs]
            t_post = [_bd_parts(tpost_ref[base + 7, h, :], left)[0] for h in hs]
            for j in reversed(range(8)):
                tile = slice(j * PAIR, (j + 1) * PAIR)
                res_nn, res_nt, res_r, res_w, t_prev = [], [], [], [], []
                for p in range(pg):
                    t_prev_f = state_before(base + j, hs[p])
                    t_prev.append(_bd_parts(t_prev_f, left)[0])
                    dss[p] = dss[p] + rc[p][:, tile] * do8[j:j + 1, ls[p]]
                    dd = _bd_parts(dss[p], left)
                    res_nn.append(_nn(lhs_kb[p], _w_nn(dd)))
                    res_nt.append(_nt(lhs_vs[p], dd[0]))
                    res_r.append(_nt(lhs_do[p], t_post[p]))
                    res_w.append(_nt(ones_lhs, _bd_parts(dss[p] * t_prev_f, left)[0]))
                for p in range(pg):
                    dsa = res_nn[p][8 + j:9 + j, :]
                    new = [res_r[p][j:j + 1, :],
                           res_w[p][0:1, :],
                           res_nt[p][j:j + 1, :],
                           res_nn[p][j:j + 1, :],
                           _nt(jnp.broadcast_to(dsa, (8, PAIR)).astype(BF16), t_prev[p])[0:1, :],
                           res_nt[p][8 + j:9 + j, :]]
                    for q, row in enumerate(new):
                        rows_ref[q, j:j + 1, ls[p]] = row
                    dss[p] = dss[p] * wc[p][:, tile] + ac[p][:, tile] * dsa
                    t_post[p] = t_prev[p]
            for p in range(pg):
                ds_ref[hs[p], :] = dss[p]
            dr_ref[rows, :] = rows_ref[0] + dri_ref[rows, :]
            dlw_ref[rows, :] = rows_ref[1] * w8
            dk_ref[rows, :] = rows_ref[2] + dki_ref[rows, :]
            dv_ref[rows, :] = rows_ref[3] + dvi_ref[rows, :]
            da_ref[rows, :] = rows_ref[4]
            db_ref[rows, :] = rows_ref[5]
            return carry

        lax.fori_loop(0, tb // 8, bwd8, 0)
        if nb:
            pl.when((pl.program_id(0) == ng - 1) & (pl.program_id(1) == nt - 1))(finish)

    any_spec = pl.BlockSpec(memory_space=pl.ANY)
    tpost_spec = pl.BlockSpec((tb, pg * HEAD, PAIR), lambda g, t: (nt - 1 - t, g, 0))
    res = pl.pallas_call(
        body, name=name, grid=(ng, nt),
        in_specs=[blk] * 6 + [ck_spec, tpost_spec] + [blk] * 5 + [any_spec] * nb,
        out_specs=[blk] * 6 + [any_spec] * nb,
        out_shape=[jax.ShapeDtypeStruct((s, c), F32)] * 6
        + [jax.ShapeDtypeStruct((3,) + p.shape[1:], p.dtype) for p in scatter],
        scratch_shapes=[pltpu.VMEM((pg * HEAD, PAIR), F32), pltpu.VMEM((6, 8, pg * PAIR), F32)]
        + ([pltpu.SemaphoreType.DMA((3 * nb,)), pltpu.SemaphoreType.DMA((3 * nb,))] if nb else []),
        compiler_params=_params(dimension_semantics=("arbitrary", "arbitrary")),
    )(r, lw, k, v, a, b, ck, tpost, sa, do, dr_in, dk_in, dv_in, *scatter)
    return res[:6], list(res[6:])


_ANY = pl.BlockSpec(memory_space=pl.ANY)


def into_slot(w, dtype, *, name):
    r, c = w.shape
    tr = _tile(r, max(8, (1 << 20) // (c * 4) // 16 * 16), unit=16) if r % 16 == 0 else r
    j_arr = (2 * lax.axis_index("x") + lax.axis_index("y")).astype(jnp.int32).reshape(1)

    def body(j_ref, w_ref, o_ref):
        o_ref[...] = w_ref[...].astype(o_ref.dtype)

    return pl.pallas_call(
        body, name=name,
        grid_spec=pltpu.PrefetchScalarGridSpec(
            num_scalar_prefetch=1, grid=(r // tr,),
            in_specs=[pl.BlockSpec((tr, c), lambda i, j_ref: (i, 0))],
            out_specs=pl.BlockSpec((None, tr, c), lambda i, j_ref: (j_ref[0], i, 0))),
        out_shape=jax.ShapeDtypeStruct((4, r, c), dtype),
        compiler_params=_params(dimension_semantics=("arbitrary",)),
    )(j_arr, w)


def allgather_multi(bufs, *, name):
    nb = len(bufs)

    def body(*refs):
        start, finish = _allgather_ops(refs[:nb], [b.shape for b in bufs], *refs[2 * nb:])
        start()
        finish()

    return pl.pallas_call(
        body, name=name, in_specs=[_ANY] * nb, out_specs=[_ANY] * nb,
        out_shape=[jax.ShapeDtypeStruct(b.shape, b.dtype) for b in bufs],
        input_output_aliases={i: i for i in range(nb)},
        scratch_shapes=[pltpu.SemaphoreType.DMA((6 * nb,)), pltpu.SemaphoreType.DMA((6 * nb,))],
    )(*bufs)


def sibling_swap_multi(gs, *, name):
    nb = len(gs)

    def body(*refs):
        ins, outs, (send_sems, recv_sems) = refs[:nb], refs[nb:2 * nb], refs[2 * nb:]
        x, y, c, _ = _place()
        cps = []
        for i in range(nb):
            rh = gs[i].shape[1] // 2
            cps.append(pltpu.make_async_remote_copy(
                src_ref=ins[i].at[:, pl.ds((1 - c) * rh, rh), :], dst_ref=outs[i], send_sem=send_sems.at[i],
                recv_sem=recv_sems.at[i], device_id=(x, y, 1 - c), device_id_type=MESH))
        for cp in cps:
            cp.start()
        for cp in cps:
            cp.wait()

    return pl.pallas_call(
        body, name=name, in_specs=[_ANY] * nb, out_specs=[_ANY] * nb,
        out_shape=[jax.ShapeDtypeStruct((4, g.shape[1] // 2, g.shape[2]), g.dtype) for g in gs],
        scratch_shapes=[pltpu.SemaphoreType.DMA((nb,)), pltpu.SemaphoreType.DMA((nb,))],
    )(*gs)


def pair_sum(g, got, *, tr, name):
    _, rh, w = got.shape
    nb = rh // tr
    c_arr = lax.axis_index("c").astype(jnp.int32).reshape(1)

    def body(c_ref, g_ref, got_ref, o_ref):
        o_ref[...] = (g_ref[...].astype(F32) + got_ref[...].astype(F32)).astype(o_ref.dtype)

    return pl.pallas_call(
        body, name=name,
        grid_spec=pltpu.PrefetchScalarGridSpec(
            num_scalar_prefetch=1, grid=(4, nb),
            in_specs=[pl.BlockSpec((1, tr, w), lambda j, i, c_ref: (j, c_ref[0] * nb + i, 0)),
                      pl.BlockSpec((1, tr, w), lambda j, i, c_ref: (j, i, 0))],
            out_specs=pl.BlockSpec((1, tr, w), lambda j, i, c_ref: (j, i, 0))),
        out_shape=jax.ShapeDtypeStruct((4, rh, w), got.dtype),
        compiler_params=_params(dimension_semantics=("arbitrary", "arbitrary")),
    )(c_arr, g, got)


def scatter_multi(pss, *, name):
    nb = len(pss)

    def body(*refs):
        start, finish = _scatter_ops(refs[:nb], refs[nb:2 * nb], *refs[2 * nb:])
        start()
        finish()

    return pl.pallas_call(
        body, name=name, in_specs=[_ANY] * nb, out_specs=[_ANY] * nb,
        out_shape=[jax.ShapeDtypeStruct((3,) + p.shape[1:], p.dtype) for p in pss],
        scratch_shapes=[pltpu.SemaphoreType.DMA((3 * nb,)), pltpu.SemaphoreType.DMA((3 * nb,))],
    )(*pss)


def chip_sum(ps, got, *, tr, name):
    _, rh, w = ps.shape
    nb = rh // tr
    jc_arr = jnp.stack([2 * lax.axis_index("x") + lax.axis_index("y"), lax.axis_index("c")]).astype(jnp.int32)

    def body(jc_ref, ps_ref, got_ref, o_ref):
        acc = ps_ref[0].astype(F32)
        for kk in range(3):
            acc = acc + got_ref[kk].astype(F32)
        o_ref[...] = acc

    return pl.pallas_call(
        body, name=name,
        grid_spec=pltpu.PrefetchScalarGridSpec(
            num_scalar_prefetch=1, grid=(nb,),
            in_specs=[pl.BlockSpec((1, tr, w), lambda i, jc: (jc[0], i, 0)),
                      pl.BlockSpec((3, tr, w), lambda i, jc: (0, i, 0))],
            out_specs=pl.BlockSpec((tr, w), lambda i, jc: (jc[1] * nb + i, 0))),
        out_shape=jax.ShapeDtypeStruct((2 * rh, w), F32),
        compiler_params=_params(dimension_semantics=("arbitrary",)),
    )(jc_arr, ps, got)


def sibling_join_multi(reds, *, name):
    nb = len(reds)

    def body(*refs):
        ins, (send_sems, recv_sems) = refs[:nb], refs[2 * nb:]
        x, y, c, _ = _place()

        def copy(i, cc):
            rh = reds[i].shape[0] // 2
            rows = ins[i].at[pl.ds(cc * rh, rh), :]
            return pltpu.make_async_remote_copy(src_ref=rows, dst_ref=rows, send_sem=send_sems.at[i],
                                                recv_sem=recv_sems.at[i], device_id=(x, y, 1 - c), device_id_type=MESH)

        cps = [copy(i, c) for i in range(nb)]
        for cp in cps:
            cp.start()
        for i, cp in enumerate(cps):
            cp.wait_send()
            copy(i, 1 - c).wait_recv()

    return pl.pallas_call(
        body, name=name, in_specs=[_ANY] * nb, out_specs=[_ANY] * nb,
        out_shape=[jax.ShapeDtypeStruct(r.shape, r.dtype) for r in reds],
        input_output_aliases={i: i for i in range(nb)},
        scratch_shapes=[pltpu.SemaphoreType.DMA((nb,)), pltpu.SemaphoreType.DMA((nb,))],
    )(*reds)


def allreduce_small(part, *, name):
    m_per, n = part.shape

    def body(x_ref, sum_ref, all_ref, send_sems, recv_sems, local_sem):
        x, y, c, chips = _place()
        me, sib = (x, y, c), (x, y, 1 - c)

        def rows(px, py, pc):
            return all_ref.at[pl.ds((4 * px + 2 * py + pc) * m_per, m_per), :]

        def copy(kk, block, to, src=None):
            return pltpu.make_async_remote_copy(src_ref=rows(*block) if src is None else src, dst_ref=rows(*block),
                                                send_sem=send_sems.at[kk], recv_sem=recv_sems.at[kk],
                                                device_id=to, device_id_type=MESH)

        mine = pltpu.make_async_copy(x_ref, rows(*me), local_sem)
        mine.start()
        first = [copy(0, me, sib, src=x_ref)]
        first += [copy(1 + kk, me, (*chip, c), src=x_ref) for kk, chip in enumerate(chips)]
        for cp in first:
            cp.start()
        passed = [copy(4 + kk, (*chip, c), sib) for kk, chip in enumerate(chips)]
        for kk, chip in enumerate(chips):
            copy(1 + kk, (*chip, c), me).wait_recv()
            passed[kk].start()
        copy(0, sib, me).wait_recv()
        for kk, chip in enumerate(chips):
            copy(4 + kk, (*chip, 1 - c), me).wait_recv()
        for cp in first + passed:
            cp.wait_send()
        mine.wait()
        acc = all_ref[0:m_per, :]
        for d in range(1, 8):
            acc = acc + all_ref[d * m_per:(d + 1) * m_per, :]
        sum_ref[...] = acc

    vmem = pl.BlockSpec(memory_space=pltpu.VMEM)
    return pl.pallas_call(
        body, name=name, in_specs=[vmem], out_specs=vmem,
        out_shape=jax.ShapeDtypeStruct((m_per, n), part.dtype),
        scratch_shapes=[pltpu.VMEM((8 * m_per, n), part.dtype), pltpu.SemaphoreType.DMA((7,)),
                        pltpu.SemaphoreType.DMA((7,)), pltpu.SemaphoreType.DMA],
    )(part)


def adamw(w, g, m, v, *, name):
    r, c = w.shape
    tm = r if r * c * 4 <= (1 << 20) else _tile(r, max(8, ((1 << 20) // (c * 4)) // 8 * 8), unit=8)
    bc1, bc2 = 1.0 - ADAM_B1 ** ADAM_STEP, 1.0 - ADAM_B2 ** ADAM_STEP

    def fn(i, n, rv, cv, sc):
        w_, g_, m_, v_ = rv
        m_ = ADAM_B1 * m_ + (1.0 - ADAM_B1) * g_
        v_ = ADAM_B2 * v_ + (1.0 - ADAM_B2) * (g_ * g_)
        delta = -ADAM_LR * ((m_ / bc1) / (jnp.sqrt(v_ / bc2) + ADAM_EPS) + ADAM_WD * w_)
        return [delta, m_, v_], []

    return rowcall(fn, [w, g, m, v], [], [(c, F32)] * 3, [], tm=tm, name=name)[0]


def _head_one_hot(c):
    e = (lax.broadcasted_iota(jnp.int32, (c, LANES), 0) // HEAD
         == lax.broadcasted_iota(jnp.int32, (c, LANES), 1)).astype(F32)
    return e, e.T


def _join_cols(g):
    return jnp.concatenate([g[j] for j in range(4)], axis=1)


def _split_cols(a):
    return jnp.stack(jnp.split(a, 4, axis=1))


LATE = ("w_out", "xattn_wq", "xattn_wk", "xattn_wv", "xattn_wo", "mlp_w1", "mlp_w2")


def reduce_over_core_pair(names, g_list):
    gots = sibling_swap_multi(g_list, name=f"rs_sibling_swap_{names[0]}")
    return [pair_sum(g, got, tr=_half_tile(g.shape[1] // 2, g.shape[2]), name=f"rs_pair_sum_{n}")
            for n, g, got in zip(names, g_list, gots)]


def layer_step(x, mem, tgt, wg, sp, late_slots=None):
    s, d = x.shape
    dr, dc = sp["rwkv_w0"].shape[1], sp["conv_b"].shape[1]
    n_lora = N_DECAY + N_ICLR + N_GATE
    n_rwkv = 3 * dr + n_lora
    pad_l = N_LORA_PAD - n_lora
    w_in = _join_cols(wg["w_in"])
    cuts = [0, dr, 2 * dr, 3 * dr, n_rwkv, n_rwkv + dc, n_rwkv + 2 * dc, n_rwkv + 2 * dc + d, n_rwkv + 2 * dc + 2 * d]
    w_r, w_k, w_v, w_l, w_ca, w_cb, w_gr, w_gc = (w_in[:, lo:hi] for lo, hi in zip(cuts[:-1], cuts[1:]))
    w_l = jnp.pad(w_l, ((0, 0), (0, pad_l)))
    sm = sp["rwkv_shift_mix"]
    mus = [sm[:, 0:dr], sm[:, dr:2 * dr], sm[:, 2 * dr:3 * dr], jnp.pad(sm[:, 3 * dr:], ((0, 0), (0, pad_l)))]
    w_up = jnp.pad(_join_cols(wg["rwkv_w_up"]).astype(F32), ((0, LANES - N_DECAY), (0, 0)))
    a_up = jnp.pad(_join_cols(wg["rwkv_a_up"]).astype(F32), ((N_DECAY, 2 * LANES - N_DECAY - N_ICLR), (0, 0)))
    g_lo = N_DECAY + N_ICLR - LANES
    g_up = jnp.pad(_join_cols(wg["rwkv_g_up"]).astype(F32), ((g_lo, pad_l), (0, 0)))
    conv_w = _join_cols(wg["conv_w"])
    row_sharded = ("w_out", "xattn_wq", "xattn_wk", "xattn_wv", "xattn_wo", "mlp_w2")
    e, et = _head_one_hot(dr)
    pre_p = [sp["rwkv_w0"], w_up, sp["rwkv_a0"], a_up, g_up, sp["rwkv_k_k"], sp["rwkv_k_a"]]
    post_p = [sp["rwkv_r_k"], sp["rwkv_gn_g"], sp["rwkv_gn_b"]]
    pairs = dr // PAIR
    tm = min(s, 128)
    tmm = mem.shape[0]

    x_bf = x.astype(BF16)
    z_r, z_k, z_v, z_l = (mm(x_bf, w, name=f"z_{n}") for n, w in zip("rkvl", (w_r, w_k, w_v, w_l)))
    z_ca, z_cb = mm(x_bf, w_ca, name="z_ca"), mm(x_bf, w_cb, name="z_cb")
    z_gr, z_gc = mm(x_bf, w_gr, name="z_gr"), mm(x_bf, w_gc, name="z_gc")
    zs_r, zs_k, zs_v, zs_l = tokenshift_fwd([z_r, z_k, z_v, z_l], mus, tm=tm, name="shift_fwd")
    pre_o = [(dr, F32)] * 5
    k_m, lw, a_s, b_s, g = stage_fwd(f_rwkv_pre, [zs_k, zs_l], pre_p, [e, et], pre_o, tm=min(s, 64), name="pre_fwd")
    (o, ck, tpost, sa_rows), gathered = scan_fwd(zs_r, lw, k_m, zs_v, a_s, b_s, pg=min(8, pairs), name="scan_fwd",
                                                 gather=[late_slots[n] for n in LATE] if late_slots else ())
    wg = dict(wg, **dict(zip(LATE, gathered)))
    wt = {n: wg[n].reshape(-1, wg[n].shape[2]) for n in row_sharded}
    post_r = [o, zs_r, k_m, zs_v, g]
    (o_r,) = stage_fwd(f_rwkv_post, post_r, post_p, [e, et], [(dr, BF16)], tm=min(s, 64), name="post_fwd")
    (u,) = stage_fwd(f_glu, [z_ca, z_cb], [], [], [(dc, F32)], tm=tm, name="glu_fwd")
    cv = conv_fwd(u, conv_w, sp["conv_b"], tm=tm, name="conv_fwd")
    cln_p = [sp["conv_ln_g"], sp["conv_ln_b"]]
    (o_c,) = stage_fwd(f_convln, [cv], cln_p, [], [(dc, BF16)], tm=tm, name="convln_fwd")
    p_r = mm(o_r, wg["proj_rwkv"], b_sh=True, name="proj_r")
    p_c = mm(o_c, wg["proj_conv"], b_sh=True, name="proj_c")
    (merged,) = stage_fwd(f_merge, [z_gr, z_gc, p_r, p_c], [], [], [(d, BF16)], tm=tm, name="merge_fwd")
    y1 = mm(merged, wt["w_out"], name="y1")
    ln1_p, ln2_p, lnm_p = ([sp[f"{n}_g"], sp[f"{n}_b"]] for n in ("ln1", "ln2", "ln_mem"))
    h1, h1_bf = stage_fwd(f_resln_twice, [x, y1], ln1_p, [], [(d, F32), (d, BF16)], tm=tm, name="ln1_fwd")
    (mem_n,) = stage_fwd(f_ln, [mem], lnm_p, [], [(d, F32)], tm=tmm, name="lnmem_fwd")
    k_mem, v_mem = mm(mem_n, wt["xattn_wk"], name="k_mem"), mm(mem_n, wt["xattn_wv"], name="v_mem")
    q = mm(h1_bf, wt["xattn_wq"], name="q")
    (ao,) = stage_fwd(f_attn, [q], [k_mem, v_mem], [], [(d, BF16)], tm=tm, name="attn_fwd")
    ca = mm(ao, wt["xattn_wo"], name="ca")
    h2, h2_bf = stage_fwd(f_resln_twice, [h1, ca], ln2_p, [], [(d, F32), (d, BF16)], tm=tm, name="ln2_fwd")
    u1 = mm(h2_bf, wg["mlp_w1"], b_sh=True, name="u1")
    f_dim = u1.shape[1]
    tmf = min(s, 64)
    (act,) = stage_fwd(f_relu2, [u1], [], [], [(f_dim, BF16)], tm=tmf, name="relu2_fwd")
    ff = mm(act, wt["mlp_w2"], name="ff")

    gw, gs = {}, {}
    loss, dh2, dff, gs["ln3_g"], gs["ln3_b"] = loss_bwd(h2, ff, tgt, sp["ln3_g"], sp["ln3_b"], tm=tm, name="loss_bwd")
    gw["mlp_w2"] = mm(act, dff, ta=True, out_dtype=BF16, name="g_mlp_w2")
    dact = mm(dff, wt["mlp_w2"], tb=True, name="d_act")
    (du1,), _ = stage_bwd(f_relu2, [u1], [], [], [dact], [(0, BF16)], tm=tmf, name="relu2_bwd")
    gw["mlp_w1"] = mm(h2_bf, du1, ta=True, out_dtype=BF16, out_sh=True, name="g_mlp_w1")
    dh2 = mm(du1, wg["mlp_w1"], tb=True, b_sh=True, acc=dh2, name="d_h2")
    (dh1, dca), (gs["ln2_g"], gs["ln2_b"]) = stage_bwd(f_resln, [h1, ca], ln2_p, [], [dh2], [(0, F32), (1, BF16)],
                                                       tm=tm, name="ln2_bwd")
    gw["xattn_wo"] = mm(ao, dca, ta=True, out_dtype=BF16, name="g_wo")
    dao = mm(dca, wt["xattn_wo"], tb=True, name="d_ao")
    (dq,), (dk_mem, dv_mem) = stage_bwd(f_attn, [q], [k_mem, v_mem], [], [dao], [(0, BF16)], tm=tm, name="attn_bwd")
    gw["xattn_wq"] = mm(h1_bf, dq, ta=True, out_dtype=BF16, name="g_wq")
    dh1 = mm(dq, wt["xattn_wq"], tb=True, acc=dh1, name="d_h1")
    gw["xattn_wk"] = mm(mem_n, dk_mem, ta=True, out_dtype=BF16, name="g_wk")
    gw["xattn_wv"] = mm(mem_n, dv_mem, ta=True, out_dtype=BF16, name="g_wv")
    dmem_n = mm(dk_mem, wt["xattn_wk"], tb=True, name="d_memn_k")
    dmem_n = mm(dv_mem, wt["xattn_wv"], tb=True, acc=dmem_n, name="d_memn_v")
    _, (gs["ln_mem_g"], gs["ln_mem_b"]) = stage_bwd(f_ln, [mem], lnm_p, [], [dmem_n], [], tm=tmm, name="lnmem_bwd")
    (dx, dy1), (gs["ln1_g"], gs["ln1_b"]) = stage_bwd(f_resln, [x, y1], ln1_p, [], [dh1], [(0, F32), (1, BF16)],
                                                      tm=tm, name="ln1_bwd")
    gw["w_out"] = mm(merged, dy1, ta=True, out_dtype=BF16, name="g_w_out")
    dmerged = mm(dy1, wt["w_out"], tb=True, name="d_merged")
    (dz_gr, dz_gc, dp_r, dp_c), _ = stage_bwd(f_merge, [z_gr, z_gc, p_r, p_c], [], [], [dmerged],
                                              [(0, BF16), (1, BF16), (2, BF16), (3, BF16)], tm=tm, name="merge_bwd")
    gw["proj_rwkv"] = mm(o_r, dp_r, ta=True, out_dtype=BF16, out_sh=True, name="g_proj_r")
    gw["proj_conv"] = mm(o_c, dp_c, ta=True, out_dtype=BF16, out_sh=True, name="g_proj_c")
    do_r = mm(dp_r, wg["proj_rwkv"], tb=True, b_sh=True, name="d_o_r")
    do_c = mm(dp_c, wg["proj_conv"], tb=True, b_sh=True, name="d_o_c")
    (dcv,), (gs["conv_ln_g"], gs["conv_ln_b"]) = stage_bwd(f_convln, [cv], cln_p, [], [do_c], [(0, F32)],
                                                           tm=tm, name="convln_bwd")
    du, g_conv_w, gs["conv_b"] = conv_bwd(dcv, u, conv_w, tm=tm, name="conv_bwd")
    gw["conv_w"] = _split_cols(g_conv_w.astype(BF16))
    (dz_ca, dz_cb), _ = stage_bwd(f_glu, [z_ca, z_cb], [], [], [du], [(0, BF16), (1, BF16)], tm=tm, name="glu_bwd")
    (d_o, dr_p, dk_p, dv_p, dg), (gs["rwkv_r_k"], gs["rwkv_gn_g"], gs["rwkv_gn_b"]) = stage_bwd(
        f_rwkv_post, post_r, post_p, [e, et], [do_r], [(k, F32) for k in range(5)], tm=min(s, 64), name="post_bwd")
    for n in row_sharded:
        gw[n] = gw[n].reshape(wg[n].shape)
    pss = reduce_over_core_pair(LATE, [gw[n] for n in LATE]) if late_slots else []
    (dzs_r, dlw, dk_m, dzs_v, da_s, db_s), sent = scan_bwd(zs_r, lw, k_m, zs_v, a_s, b_s, ck, tpost, sa_rows, d_o, dr_p,
                                                           dk_p, dv_p, pg=min(8, pairs), name="scan_bwd", scatter=pss)
    (dzs_k, dzs_l), pre_g = stage_bwd(f_rwkv_pre, [zs_k, zs_l], pre_p, [e, et], [dk_m, dlw, da_s, db_s, dg],
                                      [(0, F32), (1, F32)], tm=min(s, 64), name="pre_bwd")
    gs["rwkv_w0"], g_w_up, gs["rwkv_a0"], g_a_up, g_g_up, gs["rwkv_k_k"], gs["rwkv_k_a"] = pre_g
    gw["rwkv_w_up"] = _split_cols(g_w_up[0:N_DECAY].astype(BF16))
    gw["rwkv_a_up"] = _split_cols(g_a_up[N_DECAY:N_DECAY + N_ICLR].astype(BF16))
    gw["rwkv_g_up"] = _split_cols(g_g_up[g_lo:g_lo + N_GATE].astype(BF16))
    dzs, dmus = tokenshift_bwd([dzs_r, dzs_k, dzs_v, dzs_l], [z_r, z_k, z_v, z_l], mus, tm=tm, name="shift_bwd")
    gs["rwkv_shift_mix"] = jnp.concatenate(list(dmus[:3]) + [dmus[3][:, 0:n_lora]], axis=1)
    dzs = list(dzs) + [dz_ca, dz_cb, dz_gr, dz_gc]
    g_in = []
    for n, dz, w in zip(("r", "k", "v", "l", "ca", "cb", "gr", "gc"), dzs, (w_r, w_k, w_v, w_l, w_ca, w_cb, w_gr, w_gc)):
        g_in.append(mm(x_bf, dz, ta=True, out_dtype=BF16, name=f"g_w_{n}"))
        dx = mm(dz, w, tb=True, acc=dx, name=f"d_x_{n}")
    g_in[3] = g_in[3][:, 0:n_lora]
    gw["w_in"] = _split_cols(jnp.concatenate(g_in, axis=1))
    return loss, dx, gw, gs, ((pss, sent) if late_slots else None)


WEIGHTS = ["w_in", "rwkv_shift_mix", "rwkv_w0", "rwkv_w_up", "rwkv_a0", "rwkv_a_up", "rwkv_g_up", "rwkv_k_k",
           "rwkv_k_a", "rwkv_r_k", "rwkv_gn_g", "rwkv_gn_b", "conv_w", "conv_b", "conv_ln_g", "conv_ln_b",
           "proj_rwkv", "proj_conv", "w_out", "ln1_g", "ln1_b", "ln_mem_g", "ln_mem_b", "xattn_wq", "xattn_wk",
           "xattn_wv", "xattn_wo", "ln2_g", "ln2_b", "mlp_w1", "mlp_w2", "ln3_g", "ln3_b"]
SHARD_AXIS = {"w_in": 1, "rwkv_w_up": 1, "rwkv_a_up": 1, "rwkv_g_up": 1, "conv_w": 1, "proj_rwkv": 1, "proj_conv": 1,
              "w_out": 0, "xattn_wq": 0, "xattn_wk": 0, "xattn_wv": 0, "xattn_wo": 0, "mlp_w1": 1, "mlp_w2": 0}


def _unpack(flat, shapes):
    out, off = [], 0
    for shp in shapes:
        n = 1
        for dim in shp:
            n *= dim
        out.append(flat[..., off:off + n].reshape(flat.shape[:-1] + tuple(shp)))
        off += n
    return out


def _half_tile(rh, w):
    return _tile(rh, max(16, (2 << 20) // (w * 4) // 16 * 16), unit=16)


def kernel(*args):
    n_w = len(WEIGHTS)
    x, mem = args[0][0], args[1][0]
    tgt = args[2 + n_w][0]
    w_loc = {n: a for n, a in zip(WEIGHTS, args[2:2 + n_w])}
    m_loc = {n: a for n, a in zip(WEIGHTS, args[3 + n_w:3 + 2 * n_w])}
    v_loc = {n: a for n, a in zip(WEIGHTS, args[3 + 2 * n_w:3 + 3 * n_w])}
    big = [n for n in WEIGHTS if n in SHARD_AXIS]
    small = [n for n in WEIGHTS if n not in SHARD_AXIS]

    def as2d(n, a):
        if n in SHARD_AXIS:
            return a.reshape(a.shape[1], a.shape[-1])
        return a.reshape(1, -1)

    loc2d = {n: as2d(n, w_loc[n]) for n in WEIGHTS}
    conv_rows = loc2d["conv_w"].shape[0]
    slots = []
    for n in big:
        if n == "conv_w":
            padded = jnp.pad(loc2d[n], ((0, CONV_HALO - conv_rows), (0, 0)))
            slots.append(into_slot(padded, F32, name=f"slot_{n}"))
        else:
            slots.append(into_slot(loc2d[n], BF16, name=f"slot_{n}"))
    slots = dict(zip(big, slots))
    early = [n for n in big if n not in LATE]
    wg = dict(zip(early, allgather_multi([slots[n] for n in early], name="allgather_weights")))
    sp = {n: loc2d[n] for n in small}

    loss_part, grad_x, gw, gs, (pss_late, sent_late) = layer_step(x, mem, tgt, wg, sp, {n: slots[n] for n in LATE})

    pss = reduce_over_core_pair(early, [gw[n] for n in early])
    sent = list(scatter_multi(pss, name="rs_scatter"))
    order = early + list(LATE)
    reds = [chip_sum(ps, got, tr=_half_tile(ps.shape[1], ps.shape[2]), name=f"rs_chip_sum_{n}")
            for n, ps, got in zip(order, pss + pss_late, sent + sent_late)]
    reds = sibling_join_multi(reds, name="rs_sibling_join")
    g_big = {n: (r[0:conv_rows] if n == "conv_w" else r) for n, r in zip(order, reds)}

    small_parts = [gs[n] for n in small] + [loss_part[0:1, 0:1]]
    flat = jnp.concatenate([p.reshape(-1).astype(F32) for p in small_parts])
    flat = jnp.pad(flat, (0, -flat.shape[0] % (8 * LANES))).reshape(-1, LANES)
    red = allreduce_small(flat, name="allreduce_small").reshape(-1)
    g_small = dict(zip(small, _unpack(red, [loc2d[n].shape for n in small])))
    n_small = sum(loc2d[n].shape[1] for n in small)
    loss = red[n_small]

    grads, deltas, new_m, new_v = {}, {}, {}, {}
    for n in big:
        g2 = g_big[n]
        d2, m2, v2 = adamw(loc2d[n], g2, as2d(n, m_loc[n]), as2d(n, v_loc[n]), name=f"adamw_{n}")
        shp = w_loc[n].shape
        grads[n], deltas[n], new_m[n], new_v[n] = (t.reshape(shp) for t in (g2, d2, m2, v2))

    def small_pack(d):
        f = jnp.concatenate([as2d(n, d[n]).reshape(-1) for n in small])
        return jnp.pad(f, (0, -f.shape[0] % (8 * LANES))).reshape(-1, LANES)

    g_pack = small_pack({n: g_small[n] for n in small})
    outs = adamw(small_pack(w_loc), g_pack, small_pack(m_loc), small_pack(v_loc), name="adamw_small")
    for dst, packed in zip((deltas, new_m, new_v), outs):
        for n, t in zip(small, _unpack(packed.reshape(-1), [loc2d[n].shape for n in small])):
            dst[n] = t.reshape(w_loc[n].shape)
    for n in small:
        grads[n] = g_small[n].reshape(w_loc[n].shape)

    return (loss, grad_x[None], *[grads[n] for n in WEIGHTS], *[deltas[n] for n in WEIGHTS],
            *[new_m[n] for n in WEIGHTS], *[new_v[n] for n in WEIGHTS])
```

```python
import functools

import jax
import jax.numpy as jnp
from jax import lax
from jax.experimental import pallas as pl
from jax.experimental.pallas import tpu as pltpu

F32, BF16 = jnp.float32, jnp.bfloat16
ALPHA = 2.0 ** 0.25
LN_EPS = 1e-5
GN_EPS = 64e-5
HEAD = 64
LANES = 128
PAIR = 2 * HEAD
XATTN_HEADS = 4
CONV_WIDTH = 31
CONV_HALO = 32
N_DECAY, N_ICLR, N_GATE = 96, 96, 256
N_LORA_PAD = 512
VMEM_LIMIT = 56 * 1024 * 1024
ADAM_LR, ADAM_B1, ADAM_B2, ADAM_EPS, ADAM_WD, ADAM_STEP = 0.001, 0.9, 0.999, 1e-8, 0.01, 10
MESH = pl.DeviceIdType.MESH
HI = lax.Precision.HIGHEST


def _params(**kw):
    return pltpu.CompilerParams(vmem_limit_bytes=VMEM_LIMIT, **kw)


def _tile(n, pref, unit=LANES):
    if n <= pref:
        return n
    t = pref
    while n % t:
        t -= unit
    return t


def mm(a, b, *, name, ta=False, tb=False, out_dtype=F32, acc=None, b_sh=False, out_sh=False):
    m, k = (a.shape[1], a.shape[0]) if ta else a.shape
    if b_sh:
        n = b.shape[1] if tb else 4 * b.shape[2]
    else:
        n = b.shape[0] if tb else b.shape[1]
    tm, tn, tk = _tile(m, 1024), _tile(n // 4 if (b_sh and not tb) or out_sh else n, 1024), _tile(k // 4 if b_sh and tb else k, 1024)
    nk = k // tk
    a_spec = pl.BlockSpec((tk, tm), lambda i, j, kk: (kk, i)) if ta else pl.BlockSpec((tm, tk), lambda i, j, kk: (i, kk))
    if b_sh and tb:
        per_k = k // 4 // tk
        b_spec = pl.BlockSpec((None, tn, tk), lambda i, j, kk: (kk // per_k, j, kk % per_k))
    elif b_sh:
        per_n = n // 4 // tn
        b_spec = pl.BlockSpec((None, tk, tn), lambda i, j, kk: (j // per_n, kk, j % per_n))
    else:
        b_spec = pl.BlockSpec((tn, tk), lambda i, j, kk: (j, kk)) if tb else pl.BlockSpec((tk, tn), lambda i, j, kk: (kk, j))
    if out_sh:
        per_o = n // 4 // tn
        o_spec = pl.BlockSpec((None, tm, tn), lambda i, j, kk: (j // per_o, i, j % per_o))
    else:
        o_spec = pl.BlockSpec((tm, tn), lambda i, j, kk: (i, j))
    dims = (((0 if ta else 1,), (1 if tb else 0,)), ((), ()))

    def body(*refs):
        if acc is None:
            a_ref, b_ref, o_ref, acc_ref = refs
        else:
            a_ref, b_ref, c_ref, o_ref, acc_ref = refs
        kk = pl.program_id(2)

        @pl.when(kk == 0)
        def _():
            acc_ref[...] = jnp.zeros_like(acc_ref) if acc is None else c_ref[...].astype(F32)

        acc_ref[...] += lax.dot_general(a_ref[...].astype(BF16), b_ref[...].astype(BF16), dims,
                                        preferred_element_type=F32)

        @pl.when(kk == nk - 1)
        def _():
            o_ref[...] = acc_ref[...].astype(o_ref.dtype)

    ins = [a, b] + ([] if acc is None else [acc])
    in_specs = [a_spec, b_spec] + ([] if acc is None else [o_spec])
    return pl.pallas_call(
        body, name=name, grid=(m // tm, n // tn, nk),
        in_specs=in_specs, out_specs=o_spec,
        out_shape=jax.ShapeDtypeStruct((4, m, n // 4) if out_sh else (m, n), out_dtype),
        scratch_shapes=[pltpu.VMEM((tm, tn), F32)],
        input_output_aliases={} if acc is None else {2: 0},
        compiler_params=_params(dimension_semantics=("arbitrary", "arbitrary", "arbitrary")),
    )(*ins)


def rowcall(fn, rows, consts, out_rows, out_accs, *, tm, name, scratch=()):
    rows = [r if isinstance(r, tuple) else (r, None) for r in rows]
    s = rows[0][0].shape[0]
    n = s // tm
    in_specs, ins = [], []
    for arr, halo in rows:
        w = arr.shape[1]
        in_specs.append(pl.BlockSpec((tm, w), lambda i: (i, 0)))
        ins.append(arr)
        if halo is not None:
            kind, h = halo
            per = tm // h
            if kind == "prev":
                in_specs.append(pl.BlockSpec((h, w), lambda i, per=per: (jnp.maximum(i * per - 1, 0), 0)))
            else:
                in_specs.append(pl.BlockSpec((h, w), lambda i, per=per, last=s // h - 1: (jnp.minimum((i + 1) * per, last), 0)))
            ins.append(arr)
    for cst in consts:
        in_specs.append(pl.BlockSpec(cst.shape, lambda i, nd=cst.ndim: (0,) * nd))
        ins.append(cst)
    out_specs = [pl.BlockSpec((tm, w), lambda i: (i, 0)) for w, _ in out_rows]
    out_specs += [pl.BlockSpec(shp, lambda i, nd=len(shp): (0,) * nd) for shp, _ in out_accs]
    out_shape = [jax.ShapeDtypeStruct((s, w), dt) for w, dt in out_rows]
    out_shape += [jax.ShapeDtypeStruct(shp, dt) for shp, dt in out_accs]
    n_in, n_or, n_oa = len(ins), len(out_rows), len(out_accs)

    def body(*refs):
        i = pl.program_id(0)
        it = iter(refs[:n_in])
        row_vals = []
        for _, halo in rows:
            cur = next(it)[...]
            row_vals.append(cur if halo is None else (cur, next(it)[...]))
        const_vals = [r[...] for r in it]
        o_refs = refs[n_in:n_in + n_or]
        a_refs = refs[n_in + n_or:n_in + n_or + n_oa]
        outs, parts = fn(i, n, row_vals, const_vals, refs[n_in + n_or + n_oa:])
        for o_ref, val in zip(o_refs, outs, strict=True):
            o_ref[...] = val.astype(o_ref.dtype)
        for a_ref, part in zip(a_refs, parts, strict=True):
            part = jnp.broadcast_to(part, a_ref.shape).astype(a_ref.dtype)

            @pl.when(i == 0)
            def _(a_ref=a_ref, part=part):
                a_ref[...] = part

            @pl.when(i > 0)
            def _(a_ref=a_ref, part=part):
                a_ref[...] += part

    res = pl.pallas_call(
        body, name=name, grid=(n,), in_specs=in_specs, out_specs=out_specs, out_shape=out_shape,
        scratch_shapes=list(scratch),
        compiler_params=_params(dimension_semantics=("arbitrary",)),
    )(*ins)
    return res[:n_or], res[n_or:]


def stage_fwd(f, rows, params, consts, outs, *, tm, name):
    n_p = len(params)

    def fn(i, n, rv, cv, sc):
        return f([r.astype(F32) for r in rv], cv[:n_p], cv[n_p:]), []

    return rowcall(fn, rows, list(params) + list(consts), outs, [], tm=tm, name=name)[0]


def stage_bwd(f, rows, params, consts, cts, row_grads, *, tm, name):
    n_r, n_p = len(rows), len(params)

    def fn(i, n, rv, cv, sc):
        r = [x.astype(F32) for x in rv[:n_r]]
        ct = [x.astype(F32) for x in rv[n_r:]]
        _, vjp = jax.vjp(lambda r_, p_: f(r_, p_, cv[n_p:]), r, list(cv[:n_p]))
        d_r, d_p = vjp(ct)
        return [d_r[k] for k, _ in row_grads], d_p

    return rowcall(fn, list(rows) + list(cts), list(params) + list(consts),
                   [(rows[k].shape[1], dt) for k, dt in row_grads],
                   [(p.shape, F32) for p in params], tm=tm, name=name)


def _ln(x, g, b, eps=LN_EPS):
    xc = x - jnp.mean(x, -1, keepdims=True)
    var = jnp.mean(xc * xc, -1, keepdims=True)
    return xc * lax.rsqrt(var + eps) * g + b


def _sigmoid(x):
    return 1.0 / (1.0 + jnp.exp(-x))


def _softplus(x):
    return jnp.maximum(x, 0.0) + jnp.log(1.0 + jnp.exp(-jnp.abs(x)))


def _bdot(a, b):
    return jnp.dot(a.astype(BF16), b.astype(BF16), preferred_element_type=F32)


def _head_sum(x, e, et):
    return jnp.dot(jnp.dot(x, e, precision=HI, preferred_element_type=F32), et, precision=HI,
                   preferred_element_type=F32)


def f_rwkv_pre(rows, params, consts):
    zk, zl = rows
    w0, w_up, a0, a_up, g_up, k_k, k_a = params
    e, et = consts
    w = -_softplus(-(w0 + _bdot(jnp.tanh(zl[:, 0:LANES]), w_up))) - 0.5
    log_decay = -jnp.exp(w)
    a = _sigmoid(a0 + _bdot(zl[:, 0:2 * LANES], a_up))
    g = _bdot(_sigmoid(zl[:, LANES:N_LORA_PAD]), g_up)
    kk = zk * k_k
    kk = kk / jnp.maximum(jnp.sqrt(_head_sum(kk * kk, e, et)), 1e-12)
    return [zk * (1.0 + (a - 1.0) * k_a), log_decay, -kk, kk * a, g]


def f_rwkv_post(rows, params, consts):
    o, r, k, v, g = rows
    r_k, gn_g, gn_b = params
    e, et = consts
    oc = o - _head_sum(o, e, et) * (1.0 / HEAD)
    var = _head_sum(oc * oc, e, et) * (1.0 / HEAD)
    on = oc * lax.rsqrt(var + GN_EPS) * gn_g + gn_b
    return [(on + _head_sum(r * k * r_k, e, et) * v) * g]


def f_glu(rows, params, consts):
    return [rows[0] * _sigmoid(rows[1])]


def f_convln(rows, params, consts):
    x = _ln(rows[0], params[0], params[1])
    return [x * _sigmoid(x)]


def f_merge(rows, params, consts):
    gr, gc, pr, pc = rows
    return [_sigmoid(gr) * pr + _sigmoid(gc) * pc]


def f_resln(rows, params, consts):
    return [_ln(ALPHA * rows[0] + rows[1], params[0], params[1])]


def f_resln_twice(rows, params, consts):
    y = _ln(ALPHA * rows[0] + rows[1], params[0], params[1])
    return [y, y]


def f_ln(rows, params, consts):
    return [_ln(rows[0], params[0], params[1])]


def f_attn(rows, params, consts):
    q, (k, v) = rows[0], params
    dh = q.shape[1] // XATTN_HEADS
    outs = []
    for h in range(XATTN_HEADS):
        sl = slice(h * dh, (h + 1) * dh)
        s = lax.dot_general(q[:, sl].astype(BF16), k[:, sl].astype(BF16), (((1,), (1,)), ((), ())),
                            preferred_element_type=F32) * dh ** -0.5
        p = jnp.exp(s - jnp.max(s, -1, keepdims=True))
        p = p / jnp.sum(p, -1, keepdims=True)
        outs.append(_bdot(p, v[:, sl]))
    return [jnp.concatenate(outs, axis=-1)]


def f_relu2(rows, params, consts):
    return [jnp.square(jnp.maximum(rows[0], 0.0))]


def loss_bwd(h, y, tgt, g, b, *, tm, name):
    def fn(i, n, rv, cv, sc):
        def loss(h_, y_, g_, b_):
            err = _ln(ALPHA * h_ + y_, g_, b_) - rv[2]
            return 0.5 * jnp.sum(jnp.mean(err * err, -1, keepdims=True))
        val, vjp = jax.vjp(loss, rv[0], rv[1], cv[0], cv[1])
        dh, dy, dg, db = vjp(jnp.ones((), F32))
        return [dh, dy], [val.reshape(1, 1), dg, db]

    w = h.shape[1]
    (dh, dy), (val, dg, db) = rowcall(fn, [h, y, tgt], [g, b], [(w, F32), (w, BF16)],
                                      [((8, LANES), F32), (g.shape, F32), (b.shape, F32)], tm=tm, name=name)
    return val, dh, dy, dg, db


def _shift_down(cur, halo, first):
    rolled = pltpu.roll(cur, 1, 0)
    row0 = jnp.where(first, 0.0, halo[halo.shape[0] - 1:, :])
    return jnp.where(lax.broadcasted_iota(jnp.int32, cur.shape, 0) == 0, row0, rolled)


def _shift_up(cur, halo, last):
    rolled = pltpu.roll(cur, cur.shape[0] - 1, 0)
    rown = jnp.where(last, 0.0, halo[0:1, :])
    return jnp.where(lax.broadcasted_iota(jnp.int32, cur.shape, 0) == cur.shape[0] - 1, rown, rolled)


def tokenshift_fwd(zs, mus, *, tm, name):
    def fn(i, n, rv, cv, sc):
        return [z + (_shift_down(z, halo, i == 0) - z) * mu for (z, halo), mu in zip(rv, cv)], []

    return rowcall(fn, [(z, ("prev", 8)) for z in zs], mus, [(z.shape[1], F32) for z in zs], [], tm=tm, name=name)[0]


def tokenshift_bwd(dzs, zs, mus, *, tm, name):
    nz = len(zs)

    def fn(i, n, rv, cv, sc):
        outs, parts = [], []
        for (dz, dnext), (z, zprev), mu in zip(rv[:nz], rv[nz:], cv):
            g = dz * mu
            outs.append(dz - g + _shift_up(g, dnext * mu, i == n - 1))
            parts.append(jnp.sum(dz * (_shift_down(z, zprev, i == 0) - z), 0, keepdims=True))
        return outs, parts

    return rowcall(fn, [(d, ("next", 8)) for d in dzs] + [(z, ("prev", 8)) for z in zs], mus,
                   [(z.shape[1], BF16) for z in zs], [(mu.shape, F32) for mu in mus], tm=tm, name=name)


def conv_fwd(u, w, b, *, tm, name):
    c = u.shape[1]

    def fn(i, n, rv, cv, sc):
        (cur, halo), (ext,) = rv[0], sc
        ext[0:CONV_HALO, :] = jnp.where(i == 0, 0.0, halo)
        ext[CONV_HALO:, :] = cur
        wv = cv[0]
        acc = jnp.broadcast_to(cv[1], cur.shape)
        for j in range(CONV_WIDTH):
            acc = acc + wv[j:j + 1, :] * ext[pl.ds(CONV_HALO - CONV_WIDTH + 1 + j, tm), :]
        return [acc], []

    return rowcall(fn, [(u, ("prev", CONV_HALO))], [w, b], [(c, F32)], [], tm=tm, name=name,
                   scratch=[pltpu.VMEM((tm + CONV_HALO, c), F32)])[0][0]


def conv_bwd(dc, u, w, *, tm, name):
    c = u.shape[1]

    def fn(i, n, rv, cv, sc):
        (dcur, dnext), (ucur, uprev) = rv
        dext, uext, dw_ref = sc
        dext[0:tm, :] = dcur
        dext[tm:, :] = jnp.where(i == n - 1, 0.0, dnext)
        uext[0:CONV_HALO, :] = jnp.where(i == 0, 0.0, uprev)
        uext[CONV_HALO:, :] = ucur
        wv = cv[0]
        du = jnp.zeros_like(dcur)
        dw_ref[...] = jnp.zeros_like(dw_ref)
        for j in range(CONV_WIDTH):
            du = du + wv[j:j + 1, :] * dext[pl.ds(CONV_WIDTH - 1 - j, tm), :]
            dw_ref[j:j + 1, :] = jnp.sum(dcur * uext[pl.ds(CONV_HALO - CONV_WIDTH + 1 + j, tm), :], 0, keepdims=True)
        return [du], [dw_ref[...], jnp.sum(dcur, 0, keepdims=True)]

    (du,), (dw, db) = rowcall(
        fn, [(dc, ("next", CONV_HALO)), (u, ("prev", CONV_HALO))], [w], [(c, F32)],
        [((CONV_HALO, c), F32), ((1, c), F32)], tm=tm, name=name,
        scratch=[pltpu.VMEM((tm + CONV_HALO, c), F32), pltpu.VMEM((tm + CONV_HALO, c), F32),
                 pltpu.VMEM((CONV_HALO, c), F32)])
    return du, dw, db


SCAN_TB = 64


def _parts2(x):
    hi = x.astype(BF16)
    return hi, (x - hi.astype(F32)).astype(BF16)


def _split3(x):
    x1 = x.astype(BF16)
    d1 = x - x1.astype(F32)
    x2 = d1.astype(BF16)
    return x1, x2, (d1 - x2.astype(F32)).astype(BF16)


def _rows3(x):
    hi, lo = _parts2(x)
    return jnp.concatenate([hi, hi, lo], axis=1)


def _bd_parts(t, left):
    def expand(u):
        zero = jnp.zeros_like(u)
        return jnp.concatenate([jnp.where(left, u, zero), jnp.where(left, zero, u)], axis=0)

    hi, lo = _parts2(t)
    return expand(hi), expand(lo)


def _w_nn(parts):
    return jnp.concatenate([parts[0], parts[1], parts[0]], axis=0)


def _w_nt(parts):
    return jnp.concatenate([parts[0], parts[1], parts[0]], axis=1)


def _nn(lhs, w):
    return jnp.dot(lhs, w, preferred_element_type=F32)


def _nt(lhs, w):
    return lax.dot_general(lhs, w, (((1,), (1,)), ((), ())), preferred_element_type=F32)


def _col_const():
    i = lax.broadcasted_iota(jnp.int32, (48, 8 * PAIR), 0) % 16
    n = lax.broadcasted_iota(jnp.int32, (48, 8 * PAIR), 1)
    return ((i % 8 == n // PAIR) & (i // 8 == n % PAIR // HEAD)).astype(BF16)


def _col_tiles(x8, col3):
    xs = jnp.concatenate([x8[:, 0:HEAD], pltpu.roll(x8, HEAD, 1)[:, 0:HEAD]], axis=0)
    return lax.dot_general(jnp.concatenate(_split3(xs), axis=0), col3, (((0,), (0,)), ((), ())),
                           preferred_element_type=F32)


def _scan_steps(pg, left, col, s_ref, refs, rows, on_state, on_out):
    r_ref, lw_ref, k_ref, v_ref, a_ref, b_ref = refs
    r8, k8, a8, b8 = r_ref[rows, :], k_ref[rows, :], a_ref[rows, :], b_ref[rows, :]
    v8, w8 = v_ref[rows, :], jnp.exp(lw_ref[rows, :])
    sub8 = lax.broadcasted_iota(jnp.int32, (8, PAIR), 0)
    ls = [slice(p * PAIR, (p + 1) * PAIR) for p in range(pg)]
    lhs = [_rows3(jnp.concatenate([r8[:, l], a8[:, l]], axis=0)) for l in ls]
    wc, bc, kc = ([_col_tiles(x8[:, l], col) for l in ls] for x8 in (w8, b8, k8))
    sts = [s_ref[p * HEAD:(p + 1) * HEAD, :] for p in range(pg)]
    sas = [_nn(lhs[p], _w_nn(_bd_parts(sts[p], left)))[8:9, :] for p in range(pg)]
    outs = [jnp.zeros((8, PAIR), F32) for _ in range(pg)]
    for j in range(8):
        tile = slice(j * PAIR, (j + 1) * PAIR)
        for p in range(pg):
            sts[p] = sts[p] * wc[p][:, tile] + bc[p][:, tile] * sas[p] + kc[p][:, tile] * v8[j:j + 1, ls[p]]
            on_state(p, j, sts[p], sas[p])
            res = _nn(lhs[p], _w_nn(_bd_parts(sts[p], left)))
            outs[p] = jnp.where(sub8 == j, res[j:j + 1, :], outs[p])
            if j < 7:
                sas[p] = res[9 + j:10 + j, :]
    for p in range(pg):
        s_ref[p * HEAD:(p + 1) * HEAD, :] = sts[p]
        on_out(p, outs[p])


def _place():
    x, y, c = lax.axis_index("x"), lax.axis_index("y"), lax.axis_index("c")
    return x, y, c, [(1 - x, y), (x, 1 - y), (1 - x, 1 - y)]


def _allgather_ops(ins, shapes, send_sems, recv_sems):
    x, y, c, chips = _place()
    sib = (x, y, 1 - c)
    nb = len(ins)

    def copy(i, kk, jj, cc, to):
        rh = shapes[i][1] // 2
        rows = ins[i].at[jj, pl.ds(cc * rh, rh), :]
        return pltpu.make_async_remote_copy(src_ref=rows, dst_ref=rows, send_sem=send_sems.at[6 * i + kk],
                                            recv_sem=recv_sems.at[6 * i + kk], device_id=to, device_id_type=MESH)

    def start():
        for i in range(nb):
            for kk, (cx, cy) in enumerate(chips):
                copy(i, kk, 2 * x + y, c, (cx, cy, c)).start()

    def finish():
        for i in range(nb):
            for kk, (cx, cy) in enumerate(chips):
                copy(i, kk, 2 * cx + cy, c, sib).wait_recv()
                copy(i, 3 + kk, 2 * cx + cy, c, sib).start()
        for i in range(nb):
            for kk, (cx, cy) in enumerate(chips):
                copy(i, 3 + kk, 2 * cx + cy, 1 - c, sib).wait_recv()
        for i in range(nb):
            for kk, (cx, cy) in enumerate(chips):
                copy(i, kk, 2 * x + y, c, (cx, cy, c)).wait_send()
                copy(i, 3 + kk, 2 * cx + cy, c, sib).wait_send()

    return start, finish


def _scatter_ops(ins, outs, send_sems, recv_sems):
    x, y, c, chips = _place()
    cps = [pltpu.make_async_remote_copy(src_ref=ins[i].at[2 * cx + cy], dst_ref=outs[i].at[kk],
                                        send_sem=send_sems.at[3 * i + kk], recv_sem=recv_sems.at[3 * i + kk],
                                        device_id=(cx, cy, c), device_id_type=MESH)
           for i in range(len(ins)) for kk, (cx, cy) in enumerate(chips)]

    def start():
        for cp in cps:
            cp.start()

    def finish():
        for cp in cps:
            cp.wait()

    return start, finish


def scan_fwd(r, lw, k, v, a, b, *, pg, name, gather=()):
    s, c = r.shape
    tb, lw_ = SCAN_TB, PAIR * pg
    ng, nt = c // lw_, s // tb
    blk = pl.BlockSpec((tb, lw_), lambda g, t: (t, g))
    nb = len(gather)

    def body(*refs):
        r_ref, lw_ref, k_ref, v_ref, a_ref, b_ref = refs[:6]
        o_ref, ck_ref, tpost_ref, sa_ref = refs[6 + nb:10 + nb]
        s_ref = refs[10 + 2 * nb]
        first = (pl.program_id(0) == 0) & (pl.program_id(1) == 0)
        last = (pl.program_id(0) == ng - 1) & (pl.program_id(1) == nt - 1)
        if nb:
            start, finish = _allgather_ops(refs[6:6 + nb], [g.shape for g in gather], *refs[11 + 2 * nb:])
            pl.when(first)(start)

        @pl.when(pl.program_id(1) == 0)
        def _():
            s_ref[...] = jnp.zeros_like(s_ref)

        ck_ref[0] = s_ref[...]
        left = lax.broadcasted_iota(jnp.int32, (HEAD, PAIR), 1) < HEAD
        col = _col_const()

        sub8 = lax.broadcasted_iota(jnp.int32, (8, PAIR), 0)

        def step8(t8, carry):
            base = pl.multiple_of(t8 * 8, 8)
            rows = pl.ds(base, 8)
            sa_acc = [jnp.zeros((8, PAIR), F32) for _ in range(pg)]

            def on_state(p, j, st, sa):
                tpost_ref[base + j, p * HEAD:(p + 1) * HEAD, :] = st
                sa_acc[p] = jnp.where(sub8 == j, sa, sa_acc[p])

            def on_out(p, o8):
                o_ref[rows, p * PAIR:(p + 1) * PAIR] = o8
                sa_ref[rows, p * PAIR:(p + 1) * PAIR] = sa_acc[p]

            _scan_steps(pg, left, col, s_ref, (r_ref, lw_ref, k_ref, v_ref, a_ref, b_ref), rows, on_state, on_out)
            return carry

        lax.fori_loop(0, tb // 8, step8, 0)
        if nb:
            pl.when(last)(finish)

    any_spec = pl.BlockSpec(memory_space=pl.ANY)
    res = pl.pallas_call(
        body, name=name, grid=(ng, nt), in_specs=[blk] * 6 + [any_spec] * nb,
        out_specs=[blk, pl.BlockSpec((1, pg * HEAD, PAIR), lambda g, t: (t, g, 0)),
                   pl.BlockSpec((tb, pg * HEAD, PAIR), lambda g, t: (t, g, 0)), blk] + [any_spec] * nb,
        out_shape=[jax.ShapeDtypeStruct((s, c), F32), jax.ShapeDtypeStruct((nt, c // 2, PAIR), F32),
                   jax.ShapeDtypeStruct((s, c // 2, PAIR), F32), jax.ShapeDtypeStruct((s, c), F32)]
        + [jax.ShapeDtypeStruct(g.shape, g.dtype) for g in gather],
        input_output_aliases={6 + i: 4 + i for i in range(nb)},
        scratch_shapes=[pltpu.VMEM((pg * HEAD, PAIR), F32)]
        + ([pltpu.SemaphoreType.DMA((6 * nb,)), pltpu.SemaphoreType.DMA((6 * nb,))] if nb else []),
        compiler_params=_params(dimension_semantics=("arbitrary", "arbitrary")),
    )(r, lw, k, v, a, b, *gather)
    return res[:4], list(res[4:])


def scan_bwd(r, lw, k, v, a, b, ck, tpost, sa, do, dr_in, dk_in, dv_in, *, pg, name, scatter=()):
    s, c = r.shape
    tb, lw_ = SCAN_TB, PAIR * pg
    ng, nt = c // lw_, s // tb
    blk = pl.BlockSpec((tb, lw_), lambda g, t: (nt - 1 - t, g))
    ck_spec = pl.BlockSpec((1, pg * HEAD, PAIR), lambda g, t: (nt - 1 - t, g, 0))
    nb = len(scatter)

    def body(*refs):
        (r_ref, lw_ref, k_ref, v_ref, a_ref, b_ref, ck_ref, tpost_ref, sa_ref, do_ref, dri_ref, dki_ref,
         dvi_ref) = refs[:13]
        dr_ref, dlw_ref, dk_ref, dv_ref, da_ref, db_ref = refs[13 + nb:19 + nb]
        ds_ref, rows_ref = refs[19 + 2 * nb:21 + 2 * nb]
        if nb:
            start, finish = _scatter_ops(refs[13:13 + nb], refs[19 + nb:19 + 2 * nb], *refs[21 + 2 * nb:])
            pl.when((pl.program_id(0) == 0) & (pl.program_id(1) == 0))(start)

        @pl.when(pl.program_id(1) == 0)
        def _():
            ds_ref[...] = jnp.zeros_like(ds_ref)

        left = lax.broadcasted_iota(jnp.int32, (HEAD, PAIR), 1) < HEAD
        col = _col_const()
        ls = [slice(p * PAIR, (p + 1) * PAIR) for p in range(pg)]
        hs = [slice(p * HEAD, (p + 1) * HEAD) for p in range(pg)]
        ones_lhs = jnp.ones((8, PAIR), BF16)

        def state_before(t, h):
            return jnp.where(t == 0, ck_ref[0, h, :], tpost_ref[jnp.maximum(t - 1, 0), h, :])

        def bwd8(i8, carry):
            base = pl.multiple_of((tb // 8 - 1 - i8) * 8, 8)
            rows = pl.ds(base, 8)
            r8, k8, a8, b8 = r_ref[rows, :], k_ref[rows, :], a_ref[rows, :], b_ref[rows, :]
            v8, do8, w8, sa8 = v_ref[rows, :], do_ref[rows, :], jnp.exp(lw_ref[rows, :]), sa_ref[rows, :]
            lhs_kb = [_rows3(jnp.concatenate([k8[:, l], b8[:, l]], axis=0)) for l in ls]
            lhs_vs = [jnp.concatenate([v8[:, l], sa8[:, l]], axis=0).astype(BF16) for l in ls]
            lhs_do = [do8[:, l].astype(BF16) for l in ls]
            rc, wc, ac = ([_col_tiles(x8[:, l], col) for l in ls] for x8 in (r8, w8, a8))
            dss = [ds_ref[h, :] for h in hs]
            t_post = [_bd_parts(tpost_ref[base + 7, h, :], left)[0] for h in hs]
            for j in reversed(range(8)):
                tile = slice(j * PAIR, (j + 1) * PAIR)
                res_nn, res_nt, res_r, res_w, t_prev = [], [], [], [], []
                for p in range(pg):
                    t_prev_f = state_before(base + j, hs[p])
                    t_prev.append(_bd_parts(t_prev_f, left)[0])
                    dss[p] = dss[p] + rc[p][:, tile] * do8[j:j + 1, ls[p]]
                    dd = _bd_parts(dss[p], left)
                    res_nn.append(_nn(lhs_kb[p], _w_nn(dd)))
                    res_nt.append(_nt(lhs_vs[p], dd[0]))
                    res_r.append(_nt(lhs_do[p], t_post[p]))
                    res_w.append(_nt(ones_lhs, _bd_parts(dss[p] * t_prev_f, left)[0]))
                for p in range(pg):
                    dsa = res_nn[p][8 + j:9 + j, :]
                    new = [res_r[p][j:j + 1, :],
                           res_w[p][0:1, :],
                           res_nt[p][j:j + 1, :],
                           res_nn[p][j:j + 1, :],
                           _nt(jnp.broadcast_to(dsa, (8, PAIR)).astype(BF16), t_prev[p])[0:1, :],
                           res_nt[p][8 + j:9 + j, :]]
                    for q, row in enumerate(new):
                        rows_ref[q, j:j + 1, ls[p]] = row
                    dss[p] = dss[p] * wc[p][:, tile] + ac[p][:, tile] * dsa
                    t_post[p] = t_prev[p]
            for p in range(pg):
                ds_ref[hs[p], :] = dss[p]
            dr_ref[rows, :] = rows_ref[0] + dri_ref[rows, :]
            dlw_ref[rows, :] = rows_ref[1] * w8
            dk_ref[rows, :] = rows_ref[2] + dki_ref[rows, :]
            dv_ref[rows, :] = rows_ref[3] + dvi_ref[rows, :]
            da_ref[rows, :] = rows_ref[4]
            db_ref[rows, :] = rows_ref[5]
            return carry

        lax.fori_loop(0, tb // 8, bwd8, 0)
        if nb:
            pl.when((pl.program_id(0) == ng - 1) & (pl.program_id(1) == nt - 1))(finish)

    any_spec = pl.BlockSpec(memory_space=pl.ANY)
    tpost_spec = pl.BlockSpec((tb, pg * HEAD, PAIR), lambda g, t: (nt - 1 - t, g, 0))
    res = pl.pallas_call(
        body, name=name, grid=(ng, nt),
        in_specs=[blk] * 6 + [ck_spec, tpost_spec] + [blk] * 5 + [any_spec] * nb,
        out_specs=[blk] * 6 + [any_spec] * nb,
        out_shape=[jax.ShapeDtypeStruct((s, c), F32)] * 6
        + [jax.ShapeDtypeStruct((3,) + p.shape[1:], p.dtype) for p in scatter],
        scratch_shapes=[pltpu.VMEM((pg * HEAD, PAIR), F32), pltpu.VMEM((6, 8, pg * PAIR), F32)]
        + ([pltpu.SemaphoreType.DMA((3 * nb,)), pltpu.SemaphoreType.DMA((3 * nb,))] if nb else []),
        compiler_params=_params(dimension_semantics=("arbitrary", "arbitrary")),
    )(r, lw, k, v, a, b, ck, tpost, sa, do, dr_in, dk_in, dv_in, *scatter)
    return res[:6], list(res[6:])


_ANY = pl.BlockSpec(memory_space=pl.ANY)


def into_slot(w, dtype, *, name):
    r, c = w.shape
    tr = _tile(r, max(8, (1 << 20) // (c * 4) // 16 * 16), unit=16) if r % 16 == 0 else r
    j_arr = (2 * lax.axis_index("x") + lax.axis_index("y")).astype(jnp.int32).reshape(1)

    def body(j_ref, w_ref, o_ref):
        o_ref[...] = w_ref[...].astype(o_ref.dtype)

    return pl.pallas_call(
        body, name=name,
        grid_spec=pltpu.PrefetchScalarGridSpec(
            num_scalar_prefetch=1, grid=(r // tr,),
            in_specs=[pl.BlockSpec((tr, c), lambda i, j_ref: (i, 0))],
            out_specs=pl.BlockSpec((None, tr, c), lambda i, j_ref: (j_ref[0], i, 0))),
        out_shape=jax.ShapeDtypeStruct((4, r, c), dtype),
        compiler_params=_params(dimension_semantics=("arbitrary",)),
    )(j_arr, w)


def allgather_multi(bufs, *, name):
    nb = len(bufs)

    def body(*refs):
        start, finish = _allgather_ops(refs[:nb], [b.shape for b in bufs], *refs[2 * nb:])
        start()
        finish()

    return pl.pallas_call(
        body, name=name, in_specs=[_ANY] * nb, out_specs=[_ANY] * nb,
        out_shape=[jax.ShapeDtypeStruct(b.shape, b.dtype) for b in bufs],
        input_output_aliases={i: i for i in range(nb)},
        scratch_shapes=[pltpu.SemaphoreType.DMA((6 * nb,)), pltpu.SemaphoreType.DMA((6 * nb,))],
    )(*bufs)


def sibling_swap_multi(gs, *, name):
    nb = len(gs)

    def body(*refs):
        ins, outs, (send_sems, recv_sems) = refs[:nb], refs[nb:2 * nb], refs[2 * nb:]
        x, y, c, _ = _place()
        cps = []
        for i in range(nb):
            rh = gs[i].shape[1] // 2
            cps.append(pltpu.make_async_remote_copy(
                src_ref=ins[i].at[:, pl.ds((1 - c) * rh, rh), :], dst_ref=outs[i], send_sem=send_sems.at[i],
                recv_sem=recv_sems.at[i], device_id=(x, y, 1 - c), device_id_type=MESH))
        for cp in cps:
            cp.start()
        for cp in cps:
            cp.wait()

    return pl.pallas_call(
        body, name=name, in_specs=[_ANY] * nb, out_specs=[_ANY] * nb,
        out_shape=[jax.ShapeDtypeStruct((4, g.shape[1] // 2, g.shape[2]), g.dtype) for g in gs],
        scratch_shapes=[pltpu.SemaphoreType.DMA((nb,)), pltpu.SemaphoreType.DMA((nb,))],
    )(*gs)


def pair_sum(g, got, *, tr, name):
    _, rh, w = got.shape
    nb = rh // tr
    c_arr = lax.axis_index("c").astype(jnp.int32).reshape(1)

    def body(c_ref, g_ref, got_ref, o_ref):
        o_ref[...] = (g_ref[...].astype(F32) + got_ref[...].astype(F32)).astype(o_ref.dtype)

    return pl.pallas_call(
        body, name=name,
        grid_spec=pltpu.PrefetchScalarGridSpec(
            num_scalar_prefetch=1, grid=(4, nb),
            in_specs=[pl.BlockSpec((1, tr, w), lambda j, i, c_ref: (j, c_ref[0] * nb + i, 0)),
                      pl.BlockSpec((1, tr, w), lambda j, i, c_ref: (j, i, 0))],
            out_specs=pl.BlockSpec((1, tr, w), lambda j, i, c_ref: (j, i, 0))),
        out_shape=jax.ShapeDtypeStruct((4, rh, w), got.dtype),
        compiler_params=_params(dimension_semantics=("arbitrary", "arbitrary")),
    )(c_arr, g, got)


def scatter_multi(pss, *, name):
    nb = len(pss)

    def body(*refs):
        start, finish = _scatter_ops(refs[:nb], refs[nb:2 * nb], *refs[2 * nb:])
        start()
        finish()

    return pl.pallas_call(
        body, name=name, in_specs=[_ANY] * nb, out_specs=[_ANY] * nb,
        out_shape=[jax.ShapeDtypeStruct((3,) + p.shape[1:], p.dtype) for p in pss],
        scratch_shapes=[pltpu.SemaphoreType.DMA((3 * nb,)), pltpu.SemaphoreType.DMA((3 * nb,))],
    )(*pss)


def chip_sum(ps, got, *, tr, name):
    _, rh, w = ps.shape
    nb = rh // tr
    jc_arr = jnp.stack([2 * lax.axis_index("x") + lax.axis_index("y"), lax.axis_index("c")]).astype(jnp.int32)

    def body(jc_ref, ps_ref, got_ref, o_ref):
        acc = ps_ref[0].astype(F32)
        for kk in range(3):
            acc = acc + got_ref[kk].astype(F32)
        o_ref[...] = acc

    return pl.pallas_call(
        body, name=name,
        grid_spec=pltpu.PrefetchScalarGridSpec(
            num_scalar_prefetch=1, grid=(nb,),
            in_specs=[pl.BlockSpec((1, tr, w), lambda i, jc: (jc[0], i, 0)),
                      pl.BlockSpec((3, tr, w), lambda i, jc: (0, i, 0))],
            out_specs=pl.BlockSpec((tr, w), lambda i, jc: (jc[1] * nb + i, 0))),
        out_shape=jax.ShapeDtypeStruct((2 * rh, w), F32),
        compiler_params=_params(dimension_semantics=("arbitrary",)),
    )(jc_arr, ps, got)


def sibling_join_multi(reds, *, name):
    nb = len(reds)

    def body(*refs):
        ins, (send_sems, recv_sems) = refs[:nb], refs[2 * nb:]
        x, y, c, _ = _place()

        def copy(i, cc):
            rh = reds[i].shape[0] // 2
            rows = ins[i].at[pl.ds(cc * rh, rh), :]
            return pltpu.make_async_remote_copy(src_ref=rows, dst_ref=rows, send_sem=send_sems.at[i],
                                                recv_sem=recv_sems.at[i], device_id=(x, y, 1 - c), device_id_type=MESH)

        cps = [copy(i, c) for i in range(nb)]
        for cp in cps:
            cp.start()
        for i, cp in enumerate(cps):
            cp.wait_send()
            copy(i, 1 - c).wait_recv()

    return pl.pallas_call(
        body, name=name, in_specs=[_ANY] * nb, out_specs=[_ANY] * nb,
        out_shape=[jax.ShapeDtypeStruct(r.shape, r.dtype) for r in reds],
        input_output_aliases={i: i for i in range(nb)},
        scratch_shapes=[pltpu.SemaphoreType.DMA((nb,)), pltpu.SemaphoreType.DMA((nb,))],
    )(*reds)


def allreduce_small(part, *, name):
    m_per, n = part.shape

    def body(x_ref, sum_ref, all_ref, send_sems, recv_sems, local_sem):
        x, y, c, chips = _place()
        me, sib = (x, y, c), (x, y, 1 - c)

        def rows(px, py, pc):
            return all_ref.at[pl.ds((4 * px + 2 * py + pc) * m_per, m_per), :]

        def copy(kk, block, to, src=None):
            return pltpu.make_async_remote_copy(src_ref=rows(*block) if src is None else src, dst_ref=rows(*block),
                                                send_sem=send_sems.at[kk], recv_sem=recv_sems.at[kk],
                                                device_id=to, device_id_type=MESH)

        mine = pltpu.make_async_copy(x_ref, rows(*me), local_sem)
        mine.start()
        first = [copy(0, me, sib, src=x_ref)]
        first += [copy(1 + kk, me, (*chip, c), src=x_ref) for kk, chip in enumerate(chips)]
        for cp in first:
            cp.start()
        passed = [copy(4 + kk, (*chip, c), sib) for kk, chip in enumerate(chips)]
        for kk, chip in enumerate(chips):
            copy(1 + kk, (*chip, c), me).wait_recv()
            passed[kk].start()
        copy(0, sib, me).wait_recv()
        for kk, chip in enumerate(chips):
            copy(4 + kk, (*chip, 1 - c), me).wait_recv()
        for cp in first + passed:
            cp.wait_send()
        mine.wait()
        acc = all_ref[0:m_per, :]
        for d in range(1, 8):
            acc = acc + all_ref[d * m_per:(d + 1) * m_per, :]
        sum_ref[...] = acc

    vmem = pl.BlockSpec(memory_space=pltpu.VMEM)
    return pl.pallas_call(
        body, name=name, in_specs=[vmem], out_specs=vmem,
        out_shape=jax.ShapeDtypeStruct((m_per, n), part.dtype),
        scratch_shapes=[pltpu.VMEM((8 * m_per, n), part.dtype), pltpu.SemaphoreType.DMA((7,)),
                        pltpu.SemaphoreType.DMA((7,)), pltpu.SemaphoreType.DMA],
    )(part)


def adamw(w, g, m, v, *, name):
    r, c = w.shape
    tm = r if r * c * 4 <= (1 << 20) else _tile(r, max(8, ((1 << 20) // (c * 4)) // 8 * 8), unit=8)
    bc1, bc2 = 1.0 - ADAM_B1 ** ADAM_STEP, 1.0 - ADAM_B2 ** ADAM_STEP

    def fn(i, n, rv, cv, sc):
        w_, g_, m_, v_ = rv
        m_ = ADAM_B1 * m_ + (1.0 - ADAM_B1) * g_
        v_ = ADAM_B2 * v_ + (1.0 - ADAM_B2) * (g_ * g_)
        delta = -ADAM_LR * ((m_ / bc1) / (jnp.sqrt(v_ / bc2) + ADAM_EPS) + ADAM_WD * w_)
        return [delta, m_, v_], []

    return rowcall(fn, [w, g, m, v], [], [(c, F32)] * 3, [], tm=tm, name=name)[0]


def _head_one_hot(c):
    e = (lax.broadcasted_iota(jnp.int32, (c, LANES), 0) // HEAD
         == lax.broadcasted_iota(jnp.int32, (c, LANES), 1)).astype(F32)
    return e, e.T


def _join_cols(g):
    return jnp.concatenate([g[j] for j in range(4)], axis=1)


def _split_cols(a):
    return jnp.stack(jnp.split(a, 4, axis=1))


LATE = ("proj_rwkv", "proj_conv", "w_out", "xattn_wq", "xattn_wk", "xattn_wv", "xattn_wo", "mlp_w1", "mlp_w2")


def reduce_over_core_pair(names, g_list):
    gots = sibling_swap_multi(g_list, name=f"rs_sibling_swap_{names[0]}")
    return [pair_sum(g, got, tr=_half_tile(g.shape[1] // 2, g.shape[2]), name=f"rs_pair_sum_{n}")
            for n, g, got in zip(names, g_list, gots)]


def layer_step(x, mem, tgt, wg, sp, late_slots=None):
    s, d = x.shape
    dr, dc = sp["rwkv_w0"].shape[1], sp["conv_b"].shape[1]
    n_lora = N_DECAY + N_ICLR + N_GATE
    n_rwkv = 3 * dr + n_lora
    pad_l = N_LORA_PAD - n_lora
    w_in = _join_cols(wg["w_in"])
    cuts = [0, dr, 2 * dr, 3 * dr, n_rwkv, n_rwkv + dc, n_rwkv + 2 * dc, n_rwkv + 2 * dc + d, n_rwkv + 2 * dc + 2 * d]
    w_r, w_k, w_v, w_l, w_ca, w_cb, w_gr, w_gc = (w_in[:, lo:hi] for lo, hi in zip(cuts[:-1], cuts[1:]))
    w_l = jnp.pad(w_l, ((0, 0), (0, pad_l)))
    sm = sp["rwkv_shift_mix"]
    mus = [sm[:, 0:dr], sm[:, dr:2 * dr], sm[:, 2 * dr:3 * dr], jnp.pad(sm[:, 3 * dr:], ((0, 0), (0, pad_l)))]
    w_up = jnp.pad(_join_cols(wg["rwkv_w_up"]).astype(F32), ((0, LANES - N_DECAY), (0, 0)))
    a_up = jnp.pad(_join_cols(wg["rwkv_a_up"]).astype(F32), ((N_DECAY, 2 * LANES - N_DECAY - N_ICLR), (0, 0)))
    g_lo = N_DECAY + N_ICLR - LANES
    g_up = jnp.pad(_join_cols(wg["rwkv_g_up"]).astype(F32), ((g_lo, pad_l), (0, 0)))
    conv_w = _join_cols(wg["conv_w"])
    row_sharded = ("w_out", "xattn_wq", "xattn_wk", "xattn_wv", "xattn_wo", "mlp_w2")
    e, et = _head_one_hot(dr)
    pre_p = [sp["rwkv_w0"], w_up, sp["rwkv_a0"], a_up, g_up, sp["rwkv_k_k"], sp["rwkv_k_a"]]
    post_p = [sp["rwkv_r_k"], sp["rwkv_gn_g"], sp["rwkv_gn_b"]]
    pairs = dr // PAIR
    tm = min(s, 128)
    tmm = mem.shape[0]

    x_bf = x.astype(BF16)
    z_r, z_k, z_v, z_l = (mm(x_bf, w, name=f"z_{n}") for n, w in zip("rkvl", (w_r, w_k, w_v, w_l)))
    z_ca, z_cb = mm(x_bf, w_ca, name="z_ca"), mm(x_bf, w_cb, name="z_cb")
    z_gr, z_gc = mm(x_bf, w_gr, name="z_gr"), mm(x_bf, w_gc, name="z_gc")
    zs_r, zs_k, zs_v, zs_l = tokenshift_fwd([z_r, z_k, z_v, z_l], mus, tm=tm, name="shift_fwd")
    pre_o = [(dr, F32)] * 5
    k_m, lw, a_s, b_s, g = stage_fwd(f_rwkv_pre, [zs_k, zs_l], pre_p, [e, et], pre_o, tm=min(s, 64), name="pre_fwd")
    (o, ck, tpost, sa_rows), gathered = scan_fwd(zs_r, lw, k_m, zs_v, a_s, b_s, pg=min(8, pairs), name="scan_fwd",
                                                 gather=[late_slots[n] for n in LATE] if late_slots else ())
    wg = dict(wg, **dict(zip(LATE, gathered)))
    wt = {n: wg[n].reshape(-1, wg[n].shape[2]) for n in row_sharded}
    post_r = [o, zs_r, k_m, zs_v, g]
    (o_r,) = stage_fwd(f_rwkv_post, post_r, post_p, [e, et], [(dr, BF16)], tm=min(s, 64), name="post_fwd")
    (u,) = stage_fwd(f_glu, [z_ca, z_cb], [], [], [(dc, F32)], tm=tm, name="glu_fwd")
    cv = conv_fwd(u, conv_w, sp["conv_b"], tm=tm, name="conv_fwd")
    cln_p = [sp["conv_ln_g"], sp["conv_ln_b"]]
    (o_c,) = stage_fwd(f_convln, [cv], cln_p, [], [(dc, BF16)], tm=tm, name="convln_fwd")
    p_r = mm(o_r, wg["proj_rwkv"], b_sh=True, name="proj_r")
    p_c = mm(o_c, wg["proj_conv"], b_sh=True, name="proj_c")
    (merged,) = stage_fwd(f_merge, [z_gr, z_gc, p_r, p_c], [], [], [(d, BF16)], tm=tm, name="merge_fwd")
    y1 = mm(merged, wt["w_out"], name="y1")
    ln1_p, ln2_p, lnm_p = ([sp[f"{n}_g"], sp[f"{n}_b"]] for n in ("ln1", "ln2", "ln_mem"))
    h1, h1_bf = stage_fwd(f_resln_twice, [x, y1], ln1_p, [], [(d, F32), (d, BF16)], tm=tm, name="ln1_fwd")
    (mem_n,) = stage_fwd(f_ln, [mem], lnm_p, [], [(d, F32)], tm=tmm, name="lnmem_fwd")
    k_mem, v_mem = mm(mem_n, wt["xattn_wk"], name="k_mem"), mm(mem_n, wt["xattn_wv"], name="v_mem")
    q = mm(h1_bf, wt["xattn_wq"], name="q")
    (ao,) = stage_fwd(f_attn, [q], [k_mem, v_mem], [], [(d, BF16)], tm=tm, name="attn_fwd")
    ca = mm(ao, wt["xattn_wo"], name="ca")
    h2, h2_bf = stage_fwd(f_resln_twice, [h1, ca], ln2_p, [], [(d, F32), (d, BF16)], tm=tm, name="ln2_fwd")
    u1 = mm(h2_bf, wg["mlp_w1"], b_sh=True, name="u1")
    f_dim = u1.shape[1]
    tmf = min(s, 64)
    (act,) = stage_fwd(f_relu2, [u1], [], [], [(f_dim, BF16)], tm=tmf, name="relu2_fwd")
    ff = mm(act, wt["mlp_w2"], name="ff")

    gw, gs = {}, {}
    loss, dh2, dff, gs["ln3_g"], gs["ln3_b"] = loss_bwd(h2, ff, tgt, sp["ln3_g"], sp["ln3_b"], tm=tm, name="loss_bwd")
    gw["mlp_w2"] = mm(act, dff, ta=True, out_dtype=BF16, name="g_mlp_w2")
    dact = mm(dff, wt["mlp_w2"], tb=True, name="d_act")
    (du1,), _ = stage_bwd(f_relu2, [u1], [], [], [dact], [(0, BF16)], tm=tmf, name="relu2_bwd")
    gw["mlp_w1"] = mm(h2_bf, du1, ta=True, out_dtype=BF16, out_sh=True, name="g_mlp_w1")
    dh2 = mm(du1, wg["mlp_w1"], tb=True, b_sh=True, acc=dh2, name="d_h2")
    (dh1, dca), (gs["ln2_g"], gs["ln2_b"]) = stage_bwd(f_resln, [h1, ca], ln2_p, [], [dh2], [(0, F32), (1, BF16)],
                                                       tm=tm, name="ln2_bwd")
    gw["xattn_wo"] = mm(ao, dca, ta=True, out_dtype=BF16, name="g_wo")
    dao = mm(dca, wt["xattn_wo"], tb=True, name="d_ao")
    (dq,), (dk_mem, dv_mem) = stage_bwd(f_attn, [q], [k_mem, v_mem], [], [dao], [(0, BF16)], tm=tm, name="attn_bwd")
    gw["xattn_wq"] = mm(h1_bf, dq, ta=True, out_dtype=BF16, name="g_wq")
    dh1 = mm(dq, wt["xattn_wq"], tb=True, acc=dh1, name="d_h1")
    gw["xattn_wk"] = mm(mem_n, dk_mem, ta=True, out_dtype=BF16, name="g_wk")
    gw["xattn_wv"] = mm(mem_n, dv_mem, ta=True, out_dtype=BF16, name="g_wv")
    dmem_n = mm(dk_mem, wt["xattn_wk"], tb=True, name="d_memn_k")
    dmem_n = mm(dv_mem, wt["xattn_wv"], tb=True, acc=dmem_n, name="d_memn_v")
    _, (gs["ln_mem_g"], gs["ln_mem_b"]) = stage_bwd(f_ln, [mem], lnm_p, [], [dmem_n], [], tm=tmm, name="lnmem_bwd")
    (dx, dy1), (gs["ln1_g"], gs["ln1_b"]) = stage_bwd(f_resln, [x, y1], ln1_p, [], [dh1], [(0, F32), (1, BF16)],
                                                      tm=tm, name="ln1_bwd")
    gw["w_out"] = mm(merged, dy1, ta=True, out_dtype=BF16, name="g_w_out")
    dmerged = mm(dy1, wt["w_out"], tb=True, name="d_merged")
    (dz_gr, dz_gc, dp_r, dp_c), _ = stage_bwd(f_merge, [z_gr, z_gc, p_r, p_c], [], [], [dmerged],
                                              [(0, BF16), (1, BF16), (2, BF16), (3, BF16)], tm=tm, name="merge_bwd")
    gw["proj_rwkv"] = mm(o_r, dp_r, ta=True, out_dtype=BF16, out_sh=True, name="g_proj_r")
    gw["proj_conv"] = mm(o_c, dp_c, ta=True, out_dtype=BF16, out_sh=True, name="g_proj_c")
    do_r = mm(dp_r, wg["proj_rwkv"], tb=True, b_sh=True, name="d_o_r")
    do_c = mm(dp_c, wg["proj_conv"], tb=True, b_sh=True, name="d_o_c")
    (dcv,), (gs["conv_ln_g"], gs["conv_ln_b"]) = stage_bwd(f_convln, [cv], cln_p, [], [do_c], [(0, F32)],
                                                           tm=tm, name="convln_bwd")
    du, g_conv_w, gs["conv_b"] = conv_bwd(dcv, u, conv_w, tm=tm, name="conv_bwd")
    gw["conv_w"] = _split_cols(g_conv_w.astype(BF16))
    (dz_ca, dz_cb), _ = stage_bwd(f_glu, [z_ca, z_cb], [], [], [du], [(0, BF16), (1, BF16)], tm=tm, name="glu_bwd")
    (d_o, dr_p, dk_p, dv_p, dg), (gs["rwkv_r_k"], gs["rwkv_gn_g"], gs["rwkv_gn_b"]) = stage_bwd(
        f_rwkv_post, post_r, post_p, [e, et], [do_r], [(k, F32) for k in range(5)], tm=min(s, 64), name="post_bwd")
    for n in row_sharded:
        gw[n] = gw[n].reshape(wg[n].shape)
    pss = reduce_over_core_pair(LATE, [gw[n] for n in LATE]) if late_slots else []
    (dzs_r, dlw, dk_m, dzs_v, da_s, db_s), sent = scan_bwd(zs_r, lw, k_m, zs_v, a_s, b_s, ck, tpost, sa_rows, d_o, dr_p,
                                                           dk_p, dv_p, pg=min(8, pairs), name="scan_bwd", scatter=pss)
    (dzs_k, dzs_l), pre_g = stage_bwd(f_rwkv_pre, [zs_k, zs_l], pre_p, [e, et], [dk_m, dlw, da_s, db_s, dg],
                                      [(0, F32), (1, F32)], tm=min(s, 64), name="pre_bwd")
    gs["rwkv_w0"], g_w_up, gs["rwkv_a0"], g_a_up, g_g_up, gs["rwkv_k_k"], gs["rwkv_k_a"] = pre_g
    gw["rwkv_w_up"] = _split_cols(g_w_up[0:N_DECAY].astype(BF16))
    gw["rwkv_a_up"] = _split_cols(g_a_up[N_DECAY:N_DECAY + N_ICLR].astype(BF16))
    gw["rwkv_g_up"] = _split_cols(g_g_up[g_lo:g_lo + N_GATE].astype(BF16))
    dzs, dmus = tokenshift_bwd([dzs_r, dzs_k, dzs_v, dzs_l], [z_r, z_k, z_v, z_l], mus, tm=tm, name="shift_bwd")
    gs["rwkv_shift_mix"] = jnp.concatenate(list(dmus[:3]) + [dmus[3][:, 0:n_lora]], axis=1)
    dzs = list(dzs) + [dz_ca, dz_cb, dz_gr, dz_gc]
    g_in = []
    for n, dz, w in zip(("r", "k", "v", "l", "ca", "cb", "gr", "gc"), dzs, (w_r, w_k, w_v, w_l, w_ca, w_cb, w_gr, w_gc)):
        g_in.append(mm(x_bf, dz, ta=True, out_dtype=BF16, name=f"g_w_{n}"))
        dx = mm(dz, w, tb=True, acc=dx, name=f"d_x_{n}")
    g_in[3] = g_in[3][:, 0:n_lora]
    gw["w_in"] = _split_cols(jnp.concatenate(g_in, axis=1))
    return loss, dx, gw, gs, ((pss, sent) if late_slots else None)


WEIGHTS = ["w_in", "rwkv_shift_mix", "rwkv_w0", "rwkv_w_up", "rwkv_a0", "rwkv_a_up", "rwkv_g_up", "rwkv_k_k",
           "rwkv_k_a", "rwkv_r_k", "rwkv_gn_g", "rwkv_gn_b", "conv_w", "conv_b", "conv_ln_g", "conv_ln_b",
           "proj_rwkv", "proj_conv", "w_out", "ln1_g", "ln1_b", "ln_mem_g", "ln_mem_b", "xattn_wq", "xattn_wk",
           "xattn_wv", "xattn_wo", "ln2_g", "ln2_b", "mlp_w1", "mlp_w2", "ln3_g", "ln3_b"]
SHARD_AXIS = {"w_in": 1, "rwkv_w_up": 1, "rwkv_a_up": 1, "rwkv_g_up": 1, "conv_w": 1, "proj_rwkv": 1, "proj_conv": 1,
              "w_out": 0, "xattn_wq": 0, "xattn_wk": 0, "xattn_wv": 0, "xattn_wo": 0, "mlp_w1": 1, "mlp_w2": 0}


def _unpack(flat, shapes):
    out, off = [], 0
    for shp in shapes:
        n = 1
        for dim in shp:
            n *= dim
        out.append(flat[..., off:off + n].reshape(flat.shape[:-1] + tuple(shp)))
        off += n
    return out


def _half_tile(rh, w):
    return _tile(rh, max(16, (2 << 20) // (w * 4) // 16 * 16), unit=16)


def kernel(*args):
    n_w = len(WEIGHTS)
    x, mem = args[0][0], args[1][0]
    tgt = args[2 + n_w][0]
    w_loc = {n: a for n, a in zip(WEIGHTS, args[2:2 + n_w])}
    m_loc = {n: a for n, a in zip(WEIGHTS, args[3 + n_w:3 + 2 * n_w])}
    v_loc = {n: a for n, a in zip(WEIGHTS, args[3 + 2 * n_w:3 + 3 * n_w])}
    big = [n for n in WEIGHTS if n in SHARD_AXIS]
    small = [n for n in WEIGHTS if n not in SHARD_AXIS]

    def as2d(n, a):
        if n in SHARD_AXIS:
            return a.reshape(a.shape[1], a.shape[-1])
        return a.reshape(1, -1)

    loc2d = {n: as2d(n, w_loc[n]) for n in WEIGHTS}
    conv_rows = loc2d["conv_w"].shape[0]
    slots = []
    for n in big:
        if n == "conv_w":
            padded = jnp.pad(loc2d[n], ((0, CONV_HALO - conv_rows), (0, 0)))
            slots.append(into_slot(padded, F32, name=f"slot_{n}"))
        else:
            slots.append(into_slot(loc2d[n], BF16, name=f"slot_{n}"))
    slots = dict(zip(big, slots))
    early = [n for n in big if n not in LATE]
    wg = dict(zip(early, allgather_multi([slots[n] for n in early], name="allgather_weights")))
    sp = {n: loc2d[n] for n in small}

    loss_part, grad_x, gw, gs, (pss_late, sent_late) = layer_step(x, mem, tgt, wg, sp, {n: slots[n] for n in LATE})

    pss = reduce_over_core_pair(early, [gw[n] for n in early])
    sent = list(scatter_multi(pss, name="rs_scatter"))
    order = early + list(LATE)
    reds = [chip_sum(ps, got, tr=_half_tile(ps.shape[1], ps.shape[2]), name=f"rs_chip_sum_{n}")
            for n, ps, got in zip(order, pss + pss_late, sent + sent_late)]
    reds = sibling_join_multi(reds, name="rs_sibling_join")
    g_big = {n: (r[0:conv_rows] if n == "conv_w" else r) for n, r in zip(order, reds)}

    small_parts = [gs[n] for n in small] + [loss_part[0:1, 0:1]]
    flat = jnp.concatenate([p.reshape(-1).astype(F32) for p in small_parts])
    flat = jnp.pad(flat, (0, -flat.shape[0] % (8 * LANES))).reshape(-1, LANES)
    red = allreduce_small(flat, name="allreduce_small").reshape(-1)
    g_small = dict(zip(small, _unpack(red, [loc2d[n].shape for n in small])))
    n_small = sum(loc2d[n].shape[1] for n in small)
    loss = red[n_small]

    grads, deltas, new_m, new_v = {}, {}, {}, {}
    for n in big:
        g2 = g_big[n]
        d2, m2, v2 = adamw(loc2d[n], g2, as2d(n, m_loc[n]), as2d(n, v_loc[n]), name=f"adamw_{n}")
        shp = w_loc[n].shape
        grads[n], deltas[n], new_m[n], new_v[n] = (t.reshape(shp) for t in (g2, d2, m2, v2))

    def small_pack(d):
        f = jnp.concatenate([as2d(n, d[n]).reshape(-1) for n in small])
        return jnp.pad(f, (0, -f.shape[0] % (8 * LANES))).reshape(-1, LANES)

    g_pack = small_pack({n: g_small[n] for n in small})
    outs = adamw(small_pack(w_loc), g_pack, small_pack(m_loc), small_pack(v_loc), name="adamw_small")
    for dst, packed in zip((deltas, new_m, new_v), outs):
        for n, t in zip(small, _unpack(packed.reshape(-1), [loc2d[n].shape for n in small])):
            dst[n] = t.reshape(w_loc[n].shape)
    for n in small:
        grads[n] = g_small[n].reshape(w_loc[n].shape)

    return (loss, grad_x[None], *[grads[n] for n in WEIGHTS], *[deltas[n] for n in WEIGHTS],
            *[new_m[n] for n in WEIGHTS], *[new_v[n] for n in WEIGHTS])
```

```python
import functools

import jax
import jax.numpy as jnp
from jax import lax
from jax.experimental import pallas as pl
from jax.experimental.pallas import tpu as pltpu

F32, BF16 = jnp.float32, jnp.bfloat16
ALPHA = 2.0 ** 0.25
LN_EPS = 1e-5
GN_EPS = 64e-5
HEAD = 64
LANES = 128
PAIR = 2 * HEAD
XATTN_HEADS = 4
CONV_WIDTH = 31
CONV_HALO = 32
N_DECAY, N_ICLR, N_GATE = 96, 96, 256
N_LORA_PAD = 512
VMEM_LIMIT = 56 * 1024 * 1024
ADAM_LR, ADAM_B1, ADAM_B2, ADAM_EPS, ADAM_WD, ADAM_STEP = 0.001, 0.9, 0.999, 1e-8, 0.01, 10
MESH = pl.DeviceIdType.MESH
HI = lax.Precision.HIGHEST


def _params(**kw):
    return pltpu.CompilerParams(vmem_limit_bytes=VMEM_LIMIT, **kw)


def _tile(n, pref, unit=LANES):
    if n <= pref:
        return n
    t = pref
    while n % t:
        t -= unit
    return t


def mm(a, b, *, name, ta=False, tb=False, out_dtype=F32, acc=None, b_sh=False, out_sh=False):
    m, k = (a.shape[1], a.shape[0]) if ta else a.shape
    if b_sh:
        n = b.shape[1] if tb else 4 * b.shape[2]
    else:
        n = b.shape[0] if tb else b.shape[1]
    tm, tn, tk = _tile(m, 1024), _tile(n // 4 if (b_sh and not tb) or out_sh else n, 1024), _tile(k // 4 if b_sh and tb else k, 1024)
    nk = k // tk
    a_spec = pl.BlockSpec((tk, tm), lambda i, j, kk: (kk, i)) if ta else pl.BlockSpec((tm, tk), lambda i, j, kk: (i, kk))
    if b_sh and tb:
        per_k = k // 4 // tk
        b_spec = pl.BlockSpec((None, tn, tk), lambda i, j, kk: (kk // per_k, j, kk % per_k))
    elif b_sh:
        per_n = n // 4 // tn
        b_spec = pl.BlockSpec((None, tk, tn), lambda i, j, kk: (j // per_n, kk, j % per_n))
    else:
        b_spec = pl.BlockSpec((tn, tk), lambda i, j, kk: (j, kk)) if tb else pl.BlockSpec((tk, tn), lambda i, j, kk: (kk, j))
    if out_sh:
        per_o = n // 4 // tn
        o_spec = pl.BlockSpec((None, tm, tn), lambda i, j, kk: (j // per_o, i, j % per_o))
    else:
        o_spec = pl.BlockSpec((tm, tn), lambda i, j, kk: (i, j))
    dims = (((0 if ta else 1,), (1 if tb else 0,)), ((), ()))

    def body(*refs):
        if acc is None:
            a_ref, b_ref, o_ref, acc_ref = refs
        else:
            a_ref, b_ref, c_ref, o_ref, acc_ref = refs
        kk = pl.program_id(2)

        @pl.when(kk == 0)
        def _():
            acc_ref[...] = jnp.zeros_like(acc_ref) if acc is None else c_ref[...].astype(F32)

        acc_ref[...] += lax.dot_general(a_ref[...].astype(BF16), b_ref[...].astype(BF16), dims,
                                        preferred_element_type=F32)

        @pl.when(kk == nk - 1)
        def _():
            o_ref[...] = acc_ref[...].astype(o_ref.dtype)

    ins = [a, b] + ([] if acc is None else [acc])
    in_specs = [a_spec, b_spec] + ([] if acc is None else [o_spec])
    return pl.pallas_call(
        body, name=name, grid=(m // tm, n // tn, nk),
        in_specs=in_specs, out_specs=o_spec,
        out_shape=jax.ShapeDtypeStruct((4, m, n // 4) if out_sh else (m, n), out_dtype),
        scratch_shapes=[pltpu.VMEM((tm, tn), F32)],
        input_output_aliases={} if acc is None else {2: 0},
        compiler_params=_params(dimension_semantics=("arbitrary", "arbitrary", "arbitrary")),
    )(*ins)


def rowcall(fn, rows, consts, out_rows, out_accs, *, tm, name, scratch=()):
    rows = [r if isinstance(r, tuple) else (r, None) for r in rows]
    s = rows[0][0].shape[0]
    n = s // tm
    in_specs, ins = [], []
    for arr, halo in rows:
        w = arr.shape[1]
        in_specs.append(pl.BlockSpec((tm, w), lambda i: (i, 0)))
        ins.append(arr)
        if halo is not None:
            kind, h = halo
            per = tm // h
            if kind == "prev":
                in_specs.append(pl.BlockSpec((h, w), lambda i, per=per: (jnp.maximum(i * per - 1, 0), 0)))
            else:
                in_specs.append(pl.BlockSpec((h, w), lambda i, per=per, last=s // h - 1: (jnp.minimum((i + 1) * per, last), 0)))
            ins.append(arr)
    for cst in consts:
        in_specs.append(pl.BlockSpec(cst.shape, lambda i, nd=cst.ndim: (0,) * nd))
        ins.append(cst)
    out_specs = [pl.BlockSpec((tm, w), lambda i: (i, 0)) for w, _ in out_rows]
    out_specs += [pl.BlockSpec(shp, lambda i, nd=len(shp): (0,) * nd) for shp, _ in out_accs]
    out_shape = [jax.ShapeDtypeStruct((s, w), dt) for w, dt in out_rows]
    out_shape += [jax.ShapeDtypeStruct(shp, dt) for shp, dt in out_accs]
    n_in, n_or, n_oa = len(ins), len(out_rows), len(out_accs)

    def body(*refs):
        i = pl.program_id(0)
        it = iter(refs[:n_in])
        row_vals = []
        for _, halo in rows:
            cur = next(it)[...]
            row_vals.append(cur if halo is None else (cur, next(it)[...]))
        const_vals = [r[...] for r in it]
        o_refs = refs[n_in:n_in + n_or]
        a_refs = refs[n_in + n_or:n_in + n_or + n_oa]
        outs, parts = fn(i, n, row_vals, const_vals, refs[n_in + n_or + n_oa:])
        for o_ref, val in zip(o_refs, outs, strict=True):
            o_ref[...] = val.astype(o_ref.dtype)
        for a_ref, part in zip(a_refs, parts, strict=True):
            part = jnp.broadcast_to(part, a_ref.shape).astype(a_ref.dtype)

            @pl.when(i == 0)
            def _(a_ref=a_ref, part=part):
                a_ref[...] = part

            @pl.when(i > 0)
            def _(a_ref=a_ref, part=part):
                a_ref[...] += part

    res = pl.pallas_call(
        body, name=name, grid=(n,), in_specs=in_specs, out_specs=out_specs, out_shape=out_shape,
        scratch_shapes=list(scratch),
        compiler_params=_params(dimension_semantics=("arbitrary",)),
    )(*ins)
    return res[:n_or], res[n_or:]


def stage_fwd(f, rows, params, consts, outs, *, tm, name):
    n_p = len(params)

    def fn(i, n, rv, cv, sc):
        return f([r.astype(F32) for r in rv], cv[:n_p], cv[n_p:]), []

    return rowcall(fn, rows, list(params) + list(consts), outs, [], tm=tm, name=name)[0]


def stage_bwd(f, rows, params, consts, cts, row_grads, *, tm, name):
    n_r, n_p = len(rows), len(params)

    def fn(i, n, rv, cv, sc):
        r = [x.astype(F32) for x in rv[:n_r]]
        ct = [x.astype(F32) for x in rv[n_r:]]
        _, vjp = jax.vjp(lambda r_, p_: f(r_, p_, cv[n_p:]), r, list(cv[:n_p]))
        d_r, d_p = vjp(ct)
        return [d_r[k] for k, _ in row_grads], d_p

    return rowcall(fn, list(rows) + list(cts), list(params) + list(consts),
                   [(rows[k].shape[1], dt) for k, dt in row_grads],
                   [(p.shape, F32) for p in params], tm=tm, name=name)


def _ln(x, g, b, eps=LN_EPS):
    xc = x - jnp.mean(x, -1, keepdims=True)
    var = jnp.mean(xc * xc, -1, keepdims=True)
    return xc * lax.rsqrt(var + eps) * g + b


def _sigmoid(x):
    return 1.0 / (1.0 + jnp.exp(-x))


def _softplus(x):
    return jnp.maximum(x, 0.0) + jnp.log(1.0 + jnp.exp(-jnp.abs(x)))


def _bdot(a, b):
    return jnp.dot(a.astype(BF16), b.astype(BF16), preferred_element_type=F32)


def _head_sum(x, e, et):
    return jnp.dot(jnp.dot(x, e, precision=HI, preferred_element_type=F32), et, precision=HI,
                   preferred_element_type=F32)


def f_rwkv_pre(rows, params, consts):
    zk, zl = rows
    w0, w_up, a0, a_up, g_up, k_k, k_a = params
    e, et = consts
    w = -_softplus(-(w0 + _bdot(jnp.tanh(zl[:, 0:LANES]), w_up))) - 0.5
    log_decay = -jnp.exp(w)
    a = _sigmoid(a0 + _bdot(zl[:, 0:2 * LANES], a_up))
    g = _bdot(_sigmoid(zl[:, LANES:N_LORA_PAD]), g_up)
    kk = zk * k_k
    kk = kk / jnp.maximum(jnp.sqrt(_head_sum(kk * kk, e, et)), 1e-12)
    return [zk * (1.0 + (a - 1.0) * k_a), log_decay, -kk, kk * a, g]


def f_rwkv_post(rows, params, consts):
    o, r, k, v, g = rows
    r_k, gn_g, gn_b = params
    e, et = consts
    oc = o - _head_sum(o, e, et) * (1.0 / HEAD)
    var = _head_sum(oc * oc, e, et) * (1.0 / HEAD)
    on = oc * lax.rsqrt(var + GN_EPS) * gn_g + gn_b
    return [(on + _head_sum(r * k * r_k, e, et) * v) * g]


def f_glu(rows, params, consts):
    return [rows[0] * _sigmoid(rows[1])]


def f_convln(rows, params, consts):
    x = _ln(rows[0], params[0], params[1])
    return [x * _sigmoid(x)]


def f_merge(rows, params, consts):
    gr, gc, pr, pc = rows
    return [_sigmoid(gr) * pr + _sigmoid(gc) * pc]


def f_resln(rows, params, consts):
    return [_ln(ALPHA * rows[0] + rows[1], params[0], params[1])]


def f_resln_twice(rows, params, consts):
    y = _ln(ALPHA * rows[0] + rows[1], params[0], params[1])
    return [y, y]


def f_ln(rows, params, consts):
    return [_ln(rows[0], params[0], params[1])]


def f_attn(rows, params, consts):
    q, (k, v) = rows[0], params
    dh = q.shape[1] // XATTN_HEADS
    outs = []
    for h in range(XATTN_HEADS):
        sl = slice(h * dh, (h + 1) * dh)
        s = lax.dot_general(q[:, sl].astype(BF16), k[:, sl].astype(BF16), (((1,), (1,)), ((), ())),
                            preferred_element_type=F32) * dh ** -0.5
        p = jnp.exp(s - jnp.max(s, -1, keepdims=True))
        p = p / jnp.sum(p, -1, keepdims=True)
        outs.append(_bdot(p, v[:, sl]))
    return [jnp.concatenate(outs, axis=-1)]


def f_relu2(rows, params, consts):
    return [jnp.square(jnp.maximum(rows[0], 0.0))]


def loss_bwd(h, y, tgt, g, b, *, tm, name):
    def fn(i, n, rv, cv, sc):
        def loss(h_, y_, g_, b_):
            err = _ln(ALPHA * h_ + y_, g_, b_) - rv[2]
            return 0.5 * jnp.sum(jnp.mean(err * err, -1, keepdims=True))
        val, vjp = jax.vjp(loss, rv[0], rv[1], cv[0], cv[1])
        dh, dy, dg, db = vjp(jnp.ones((), F32))
        return [dh, dy], [val.reshape(1, 1), dg, db]

    w = h.shape[1]
    (dh, dy), (val, dg, db) = rowcall(fn, [h, y, tgt], [g, b], [(w, F32), (w, BF16)],
                                      [((8, LANES), F32), (g.shape, F32), (b.shape, F32)], tm=tm, name=name)
    return val, dh, dy, dg, db


def _shift_down(cur, halo, first):
    rolled = pltpu.roll(cur, 1, 0)
    row0 = jnp.where(first, 0.0, halo[halo.shape[0] - 1:, :])
    return jnp.where(lax.broadcasted_iota(jnp.int32, cur.shape, 0) == 0, row0, rolled)


def _shift_up(cur, halo, last):
    rolled = pltpu.roll(cur, cur.shape[0] - 1, 0)
    rown = jnp.where(last, 0.0, halo[0:1, :])
    return jnp.where(lax.broadcasted_iota(jnp.int32, cur.shape, 0) == cur.shape[0] - 1, rown, rolled)


def tokenshift_fwd(zs, mus, *, tm, name):
    def fn(i, n, rv, cv, sc):
        return [z + (_shift_down(z, halo, i == 0) - z) * mu for (z, halo), mu in zip(rv, cv)], []

    return rowcall(fn, [(z, ("prev", 8)) for z in zs], mus, [(z.shape[1], F32) for z in zs], [], tm=tm, name=name)[0]


def tokenshift_bwd(dzs, zs, mus, *, tm, name):
    nz = len(zs)

    def fn(i, n, rv, cv, sc):
        outs, parts = [], []
        for (dz, dnext), (z, zprev), mu in zip(rv[:nz], rv[nz:], cv):
            g = dz * mu
            outs.append(dz - g + _shift_up(g, dnext * mu, i == n - 1))
            parts.append(jnp.sum(dz * (_shift_down(z, zprev, i == 0) - z), 0, keepdims=True))
        return outs, parts

    return rowcall(fn, [(d, ("next", 8)) for d in dzs] + [(z, ("prev", 8)) for z in zs], mus,
                   [(z.shape[1], BF16) for z in zs], [(mu.shape, F32) for mu in mus], tm=tm, name=name)


def conv_fwd(u, w, b, *, tm, name):
    c = u.shape[1]

    def fn(i, n, rv, cv, sc):
        (cur, halo), (ext,) = rv[0], sc
        ext[0:CONV_HALO, :] = jnp.where(i == 0, 0.0, halo)
        ext[CONV_HALO:, :] = cur
        wv = cv[0]
        acc = jnp.broadcast_to(cv[1], cur.shape)
        for j in range(CONV_WIDTH):
            acc = acc + wv[j:j + 1, :] * ext[pl.ds(CONV_HALO - CONV_WIDTH + 1 + j, tm), :]
        return [acc], []

    return rowcall(fn, [(u, ("prev", CONV_HALO))], [w, b], [(c, F32)], [], tm=tm, name=name,
                   scratch=[pltpu.VMEM((tm + CONV_HALO, c), F32)])[0][0]


def conv_bwd(dc, u, w, *, tm, name):
    c = u.shape[1]

    def fn(i, n, rv, cv, sc):
        (dcur, dnext), (ucur, uprev) = rv
        dext, uext, dw_ref = sc
        dext[0:tm, :] = dcur
        dext[tm:, :] = jnp.where(i == n - 1, 0.0, dnext)
        uext[0:CONV_HALO, :] = jnp.where(i == 0, 0.0, uprev)
        uext[CONV_HALO:, :] = ucur
        wv = cv[0]
        du = jnp.zeros_like(dcur)
        dw_ref[...] = jnp.zeros_like(dw_ref)
        for j in range(CONV_WIDTH):
            du = du + wv[j:j + 1, :] * dext[pl.ds(CONV_WIDTH - 1 - j, tm), :]
            dw_ref[j:j + 1, :] = jnp.sum(dcur * uext[pl.ds(CONV_HALO - CONV_WIDTH + 1 + j, tm), :], 0, keepdims=True)
        return [du], [dw_ref[...], jnp.sum(dcur, 0, keepdims=True)]

    (du,), (dw, db) = rowcall(
        fn, [(dc, ("next", CONV_HALO)), (u, ("prev", CONV_HALO))], [w], [(c, F32)],
        [((CONV_HALO, c), F32), ((1, c), F32)], tm=tm, name=name,
        scratch=[pltpu.VMEM((tm + CONV_HALO, c), F32), pltpu.VMEM((tm + CONV_HALO, c), F32),
                 pltpu.VMEM((CONV_HALO, c), F32)])
    return du, dw, db


SCAN_TB = 64


def _parts2(x):
    hi = x.astype(BF16)
    return hi, (x - hi.astype(F32)).astype(BF16)


def _split3(x):
    x1 = x.astype(BF16)
    d1 = x - x1.astype(F32)
    x2 = d1.astype(BF16)
    return x1, x2, (d1 - x2.astype(F32)).astype(BF16)


def _rows3(x):
    hi, lo = _parts2(x)
    return jnp.concatenate([hi, hi, lo], axis=1)


def _bd_parts(t, left):
    def expand(u):
        zero = jnp.zeros_like(u)
        return jnp.concatenate([jnp.where(left, u, zero), jnp.where(left, zero, u)], axis=0)

    hi, lo = _parts2(t)
    return expand(hi), expand(lo)


def _w_nn(parts):
    return jnp.concatenate([parts[0], parts[1], parts[0]], axis=0)


def _w_nt(parts):
    return jnp.concatenate([parts[0], parts[1], parts[0]], axis=1)


def _nn(lhs, w):
    return jnp.dot(lhs, w, preferred_element_type=F32)


def _nt(lhs, w):
    return lax.dot_general(lhs, w, (((1,), (1,)), ((), ())), preferred_element_type=F32)


def _col_const():
    i = lax.broadcasted_iota(jnp.int32, (48, 8 * PAIR), 0) % 16
    n = lax.broadcasted_iota(jnp.int32, (48, 8 * PAIR), 1)
    return ((i % 8 == n // PAIR) & (i // 8 == n % PAIR // HEAD)).astype(BF16)


def _col_tiles(x8, col3):
    xs = jnp.concatenate([x8[:, 0:HEAD], pltpu.roll(x8, HEAD, 1)[:, 0:HEAD]], axis=0)
    return lax.dot_general(jnp.concatenate(_split3(xs), axis=0), col3, (((0,), (0,)), ((), ())),
                           preferred_element_type=F32)


def _scan_steps(pg, left, col, s_ref, refs, rows, on_state, on_out):
    r_ref, lw_ref, k_ref, v_ref, a_ref, b_ref = refs
    r8, k8, a8, b8 = r_ref[rows, :], k_ref[rows, :], a_ref[rows, :], b_ref[rows, :]
    v8, w8 = v_ref[rows, :], jnp.exp(lw_ref[rows, :])
    sub8 = lax.broadcasted_iota(jnp.int32, (8, PAIR), 0)
    ls = [slice(p * PAIR, (p + 1) * PAIR) for p in range(pg)]
    lhs = [_rows3(jnp.concatenate([r8[:, l], a8[:, l]], axis=0)) for l in ls]
    wc, bc, kc = ([_col_tiles(x8[:, l], col) for l in ls] for x8 in (w8, b8, k8))
    sts = [s_ref[p * HEAD:(p + 1) * HEAD, :] for p in range(pg)]
    sas = [_nn(lhs[p], _w_nn(_bd_parts(sts[p], left)))[8:9, :] for p in range(pg)]
    outs = [jnp.zeros((8, PAIR), F32) for _ in range(pg)]
    for j in range(8):
        tile = slice(j * PAIR, (j + 1) * PAIR)
        for p in range(pg):
            sts[p] = sts[p] * wc[p][:, tile] + bc[p][:, tile] * sas[p] + kc[p][:, tile] * v8[j:j + 1, ls[p]]
            on_state(p, j, sts[p], sas[p])
            res = _nn(lhs[p], _w_nn(_bd_parts(sts[p], left)))
            outs[p] = jnp.where(sub8 == j, res[j:j + 1, :], outs[p])
            if j < 7:
                sas[p] = res[9 + j:10 + j, :]
    for p in range(pg):
        s_ref[p * HEAD:(p + 1) * HEAD, :] = sts[p]
        on_out(p, outs[p])


def _place():
    x, y, c = lax.axis_index("x"), lax.axis_index("y"), lax.axis_index("c")
    return x, y, c, [(1 - x, y), (x, 1 - y), (1 - x, 1 - y)]


def _allgather_ops(ins, shapes, send_sems, recv_sems):
    x, y, c, chips = _place()
    sib = (x, y, 1 - c)
    nb = len(ins)

    def copy(i, kk, jj, cc, to):
        rh = shapes[i][1] // 2
        rows = ins[i].at[jj, pl.ds(cc * rh, rh), :]
        return pltpu.make_async_remote_copy(src_ref=rows, dst_ref=rows, send_sem=send_sems.at[6 * i + kk],
                                            recv_sem=recv_sems.at[6 * i + kk], device_id=to, device_id_type=MESH)

    def start():
        for i in range(nb):
            for kk, (cx, cy) in enumerate(chips):
                copy(i, kk, 2 * x + y, c, (cx, cy, c)).start()

    def finish():
        for i in range(nb):
            for kk, (cx, cy) in enumerate(chips):
                copy(i, kk, 2 * cx + cy, c, sib).wait_recv()
                copy(i, 3 + kk, 2 * cx + cy, c, sib).start()
        for i in range(nb):
            for kk, (cx, cy) in enumerate(chips):
                copy(i, 3 + kk, 2 * cx + cy, 1 - c, sib).wait_recv()
        for i in range(nb):
            for kk, (cx, cy) in enumerate(chips):
                copy(i, kk, 2 * x + y, c, (cx, cy, c)).wait_send()
                copy(i, 3 + kk, 2 * cx + cy, c, sib).wait_send()

    return start, finish


def _scatter_ops(ins, outs, send_sems, recv_sems):
    x, y, c, chips = _place()
    cps = [pltpu.make_async_remote_copy(src_ref=ins[i].at[2 * cx + cy], dst_ref=outs[i].at[kk],
                                        send_sem=send_sems.at[3 * i + kk], recv_sem=recv_sems.at[3 * i + kk],
                                        device_id=(cx, cy, c), device_id_type=MESH)
           for i in range(len(ins)) for kk, (cx, cy) in enumerate(chips)]

    def start():
        for cp in cps:
            cp.start()

    def finish():
        for cp in cps:
            cp.wait()

    return start, finish


def scan_fwd(r, lw, k, v, a, b, *, pg, name, gather=()):
    s, c = r.shape
    tb, lw_ = SCAN_TB, PAIR * pg
    ng, nt = c // lw_, s // tb
    blk = pl.BlockSpec((tb, lw_), lambda g, t: (t, g))
    nb = len(gather)

    def body(*refs):
        r_ref, lw_ref, k_ref, v_ref, a_ref, b_ref = refs[:6]
        o_ref, ck_ref, tpost_ref, sa_ref = refs[6 + nb:10 + nb]
        s_ref = refs[10 + 2 * nb]
        first = (pl.program_id(0) == 0) & (pl.program_id(1) == 0)
        last = (pl.program_id(0) == ng - 1) & (pl.program_id(1) == nt - 1)
        if nb:
            start, finish = _allgather_ops(refs[6:6 + nb], [g.shape for g in gather], *refs[11 + 2 * nb:])
            pl.when(first)(start)

        @pl.when(pl.program_id(1) == 0)
        def _():
            s_ref[...] = jnp.zeros_like(s_ref)

        ck_ref[0] = s_ref[...]
        left = lax.broadcasted_iota(jnp.int32, (HEAD, PAIR), 1) < HEAD
        col = _col_const()

        sub8 = lax.broadcasted_iota(jnp.int32, (8, PAIR), 0)

        def step8(t8, carry):
            base = pl.multiple_of(t8 * 8, 8)
            rows = pl.ds(base, 8)
            sa_acc = [jnp.zeros((8, PAIR), F32) for _ in range(pg)]

            def on_state(p, j, st, sa):
                tpost_ref[base + j, p * HEAD:(p + 1) * HEAD, :] = st
                sa_acc[p] = jnp.where(sub8 == j, sa, sa_acc[p])

            def on_out(p, o8):
                o_ref[rows, p * PAIR:(p + 1) * PAIR] = o8
                sa_ref[rows, p * PAIR:(p + 1) * PAIR] = sa_acc[p]

            _scan_steps(pg, left, col, s_ref, (r_ref, lw_ref, k_ref, v_ref, a_ref, b_ref), rows, on_state, on_out)
            return carry

        lax.fori_loop(0, tb // 8, step8, 0)
        if nb:
            pl.when(last)(finish)

    any_spec = pl.BlockSpec(memory_space=pl.ANY)
    res = pl.pallas_call(
        body, name=name, grid=(ng, nt), in_specs=[blk] * 6 + [any_spec] * nb,
        out_specs=[blk, pl.BlockSpec((1, pg * HEAD, PAIR), lambda g, t: (t, g, 0)),
                   pl.BlockSpec((tb, pg * HEAD, PAIR), lambda g, t: (t, g, 0)), blk] + [any_spec] * nb,
        out_shape=[jax.ShapeDtypeStruct((s, c), F32), jax.ShapeDtypeStruct((nt, c // 2, PAIR), F32),
                   jax.ShapeDtypeStruct((s, c // 2, PAIR), F32), jax.ShapeDtypeStruct((s, c), F32)]
        + [jax.ShapeDtypeStruct(g.shape, g.dtype) for g in gather],
        input_output_aliases={6 + i: 4 + i for i in range(nb)},
        scratch_shapes=[pltpu.VMEM((pg * HEAD, PAIR), F32)]
        + ([pltpu.SemaphoreType.DMA((6 * nb,)), pltpu.SemaphoreType.DMA((6 * nb,))] if nb else []),
        compiler_params=_params(dimension_semantics=("arbitrary", "arbitrary")),
    )(r, lw, k, v, a, b, *gather)
    return res[:4], list(res[4:])


def scan_bwd(r, lw, k, v, a, b, ck, tpost, sa, do, dr_in, dk_in, dv_in, *, pg, name, scatter=()):
    s, c = r.shape
    tb, lw_ = SCAN_TB, PAIR * pg
    ng, nt = c // lw_, s // tb
    blk = pl.BlockSpec((tb, lw_), lambda g, t: (nt - 1 - t, g))
    ck_spec = pl.BlockSpec((1, pg * HEAD, PAIR), lambda g, t: (nt - 1 - t, g, 0))
    nb = len(scatter)

    def body(*refs):
        (r_ref, lw_ref, k_ref, v_ref, a_ref, b_ref, ck_ref, tpost_ref, sa_ref, do_ref, dri_ref, dki_ref,
         dvi_ref) = refs[:13]
        dr_ref, dlw_ref, dk_ref, dv_ref, da_ref, db_ref = refs[13 + nb:19 + nb]
        ds_ref, rows_ref = refs[19 + 2 * nb:21 + 2 * nb]
        if nb:
            start, finish = _scatter_ops(refs[13:13 + nb], refs[19 + nb:19 + 2 * nb], *refs[21 + 2 * nb:])
            pl.when((pl.program_id(0) == 0) & (pl.program_id(1) == 0))(start)

        @pl.when(pl.program_id(1) == 0)
        def _():
            ds_ref[...] = jnp.zeros_like(ds_ref)

        left = lax.broadcasted_iota(jnp.int32, (HEAD, PAIR), 1) < HEAD
        col = _col_const()
        ls = [slice(p * PAIR, (p + 1) * PAIR) for p in range(pg)]
        hs = [slice(p * HEAD, (p + 1) * HEAD) for p in range(pg)]
        ones_lhs = jnp.ones((8, PAIR), BF16)

        def state_before(t, h):
            return jnp.where(t == 0, ck_ref[0, h, :], tpost_ref[jnp.maximum(t - 1, 0), h, :])

        def bwd8(i8, carry):
            base = pl.multiple_of((tb // 8 - 1 - i8) * 8, 8)
            rows = pl.ds(base, 8)
            r8, k8, a8, b8 = r_ref[rows, :], k_ref[rows, :], a_ref[rows, :], b_ref[rows, :]
            v8, do8, w8, sa8 = v_ref[rows, :], do_ref[rows, :], jnp.exp(lw_ref[rows, :]), sa_ref[rows, :]
            lhs_kb = [_rows3(jnp.concatenate([k8[:, l], b8[:, l]], axis=0)) for l in ls]
            lhs_vs = [jnp.concatenate([v8[:, l], sa8[:, l]], axis=0).astype(BF16) for l in ls]
            rc, wc, ac = ([_col_tiles(x8[:, l], col) for l in ls] for x8 in (r8, w8, a8))
            dss = [ds_ref[h, :] for h in hs]
            t_post = [_bd_parts(tpost_ref[base + 7, h, :], left)[0] for h in hs]
            dsa_next = [jnp.zeros((8, PAIR), F32) for _ in range(pg)]
            for j in reversed(range(8)):
                tile = slice(j * PAIR, (j + 1) * PAIR)
                res_nn, res_nt, res_r, res_w, t_prev = [], [], [], [], []
                for p in range(pg):
                    t_prev_f = state_before(base + j, hs[p])
                    t_prev.append(_bd_parts(t_prev_f, left)[0])
                    dss[p] = dss[p] + rc[p][:, tile] * do8[j:j + 1, ls[p]]
                    dd = _bd_parts(dss[p], left)
                    res_nn.append(_nn(lhs_kb[p], _w_nn(dd)))
                    res_nt.append(_nt(lhs_vs[p], dd[0]))
                    res_r.append(_nt(jnp.concatenate([do8[:, ls[p]], dsa_next[p]], axis=0).astype(BF16), t_post[p]))
                    res_w.append(_nt(ones_lhs, _bd_parts(dss[p] * t_prev_f, left)[0]))
                for p in range(pg):
                    dsa = res_nn[p][8 + j:9 + j, :]
                    new = {0: res_r[p][j:j + 1, :],
                           1: res_w[p][0:1, :],
                           2: res_nt[p][j:j + 1, :],
                           3: res_nn[p][j:j + 1, :],
                           5: res_nt[p][8 + j:9 + j, :]}
                    for q, row in new.items():
                        rows_ref[q, j:j + 1, ls[p]] = row
                    if j < 7:
                        rows_ref[4, j + 1:j + 2, ls[p]] = res_r[p][8:9, :]
                    dss[p] = dss[p] * wc[p][:, tile] + ac[p][:, tile] * dsa
                    dsa_next[p] = jnp.broadcast_to(dsa, (8, PAIR))
                    t_post[p] = t_prev[p]
            for p in range(pg):
                rows_ref[4, 0:1, ls[p]] = _nt(dsa_next[p].astype(BF16), t_post[p])[0:1, :]
            for p in range(pg):
                ds_ref[hs[p], :] = dss[p]
            dr_ref[rows, :] = rows_ref[0] + dri_ref[rows, :]
            dlw_ref[rows, :] = rows_ref[1] * w8
            dk_ref[rows, :] = rows_ref[2] + dki_ref[rows, :]
            dv_ref[rows, :] = rows_ref[3] + dvi_ref[rows, :]
            da_ref[rows, :] = rows_ref[4]
            db_ref[rows, :] = rows_ref[5]
            return carry

        lax.fori_loop(0, tb // 8, bwd8, 0)
        if nb:
            pl.when((pl.program_id(0) == ng - 1) & (pl.program_id(1) == nt - 1))(finish)

    any_spec = pl.BlockSpec(memory_space=pl.ANY)
    tpost_spec = pl.BlockSpec((tb, pg * HEAD, PAIR), lambda g, t: (nt - 1 - t, g, 0))
    res = pl.pallas_call(
        body, name=name, grid=(ng, nt),
        in_specs=[blk] * 6 + [ck_spec, tpost_spec] + [blk] * 5 + [any_spec] * nb,
        out_specs=[blk] * 6 + [any_spec] * nb,
        out_shape=[jax.ShapeDtypeStruct((s, c), F32)] * 6
        + [jax.ShapeDtypeStruct((3,) + p.shape[1:], p.dtype) for p in scatter],
        scratch_shapes=[pltpu.VMEM((pg * HEAD, PAIR), F32), pltpu.VMEM((6, 8, pg * PAIR), F32)]
        + ([pltpu.SemaphoreType.DMA((3 * nb,)), pltpu.SemaphoreType.DMA((3 * nb,))] if nb else []),
        compiler_params=_params(dimension_semantics=("arbitrary", "arbitrary")),
    )(r, lw, k, v, a, b, ck, tpost, sa, do, dr_in, dk_in, dv_in, *scatter)
    return res[:6], list(res[6:])


_ANY = pl.BlockSpec(memory_space=pl.ANY)


def into_slot(w, dtype, *, name):
    r, c = w.shape
    tr = _tile(r, max(8, (1 << 20) // (c * 4) // 16 * 16), unit=16) if r % 16 == 0 else r
    j_arr = (2 * lax.axis_index("x") + lax.axis_index("y")).astype(jnp.int32).reshape(1)

    def body(j_ref, w_ref, o_ref):
        o_ref[...] = w_ref[...].astype(o_ref.dtype)

    return pl.pallas_call(
        body, name=name,
        grid_spec=pltpu.PrefetchScalarGridSpec(
            num_scalar_prefetch=1, grid=(r // tr,),
            in_specs=[pl.BlockSpec((tr, c), lambda i, j_ref: (i, 0))],
            out_specs=pl.BlockSpec((None, tr, c), lambda i, j_ref: (j_ref[0], i, 0))),
        out_shape=jax.ShapeDtypeStruct((4, r, c), dtype),
        compiler_params=_params(dimension_semantics=("arbitrary",)),
    )(j_arr, w)


def allgather_multi(bufs, *, name):
    nb = len(bufs)

    def body(*refs):
        start, finish = _allgather_ops(refs[:nb], [b.shape for b in bufs], *refs[2 * nb:])
        start()
        finish()

    return pl.pallas_call(
        body, name=name, in_specs=[_ANY] * nb, out_specs=[_ANY] * nb,
        out_shape=[jax.ShapeDtypeStruct(b.shape, b.dtype) for b in bufs],
        input_output_aliases={i: i for i in range(nb)},
        scratch_shapes=[pltpu.SemaphoreType.DMA((6 * nb,)), pltpu.SemaphoreType.DMA((6 * nb,))],
    )(*bufs)


def sibling_swap_multi(gs, *, name):
    nb = len(gs)

    def body(*refs):
        ins, outs, (send_sems, recv_sems) = refs[:nb], refs[nb:2 * nb], refs[2 * nb:]
        x, y, c, _ = _place()
        cps = []
        for i in range(nb):
            rh = gs[i].shape[1] // 2
            cps.append(pltpu.make_async_remote_copy(
                src_ref=ins[i].at[:, pl.ds((1 - c) * rh, rh), :], dst_ref=outs[i], send_sem=send_sems.at[i],
                recv_sem=recv_sems.at[i], device_id=(x, y, 1 - c), device_id_type=MESH))
        for cp in cps:
            cp.start()
        for cp in cps:
            cp.wait()

    return pl.pallas_call(
        body, name=name, in_specs=[_ANY] * nb, out_specs=[_ANY] * nb,
        out_shape=[jax.ShapeDtypeStruct((4, g.shape[1] // 2, g.shape[2]), g.dtype) for g in gs],
        scratch_shapes=[pltpu.SemaphoreType.DMA((nb,)), pltpu.SemaphoreType.DMA((nb,))],
    )(*gs)


def pair_sum(g, got, *, tr, name):
    _, rh, w = got.shape
    nb = rh // tr
    c_arr = lax.axis_index("c").astype(jnp.int32).reshape(1)

    def body(c_ref, g_ref, got_ref, o_ref):
        o_ref[...] = (g_ref[...].astype(F32) + got_ref[...].astype(F32)).astype(o_ref.dtype)

    return pl.pallas_call(
        body, name=name,
        grid_spec=pltpu.PrefetchScalarGridSpec(
            num_scalar_prefetch=1, grid=(4, nb),
            in_specs=[pl.BlockSpec((1, tr, w), lambda j, i, c_ref: (j, c_ref[0] * nb + i, 0)),
                      pl.BlockSpec((1, tr, w), lambda j, i, c_ref: (j, i, 0))],
            out_specs=pl.BlockSpec((1, tr, w), lambda j, i, c_ref: (j, i, 0))),
        out_shape=jax.ShapeDtypeStruct((4, rh, w), got.dtype),
        compiler_params=_params(dimension_semantics=("arbitrary", "arbitrary")),
    )(c_arr, g, got)


def scatter_multi(pss, *, name):
    nb = len(pss)

    def body(*refs):
        start, finish = _scatter_ops(refs[:nb], refs[nb:2 * nb], *refs[2 * nb:])
        start()
        finish()

    return pl.pallas_call(
        body, name=name, in_specs=[_ANY] * nb, out_specs=[_ANY] * nb,
        out_shape=[jax.ShapeDtypeStruct((3,) + p.shape[1:], p.dtype) for p in pss],
        scratch_shapes=[pltpu.SemaphoreType.DMA((3 * nb,)), pltpu.SemaphoreType.DMA((3 * nb,))],
    )(*pss)


def chip_sum(ps, got, *, tr, name):
    _, rh, w = ps.shape
    nb = rh // tr
    jc_arr = jnp.stack([2 * lax.axis_index("x") + lax.axis_index("y"), lax.axis_index("c")]).astype(jnp.int32)

    def body(jc_ref, ps_ref, got_ref, o_ref):
        acc = ps_ref[0].astype(F32)
        for kk in range(3):
            acc = acc + got_ref[kk].astype(F32)
        o_ref[...] = acc

    return pl.pallas_call(
        body, name=name,
        grid_spec=pltpu.PrefetchScalarGridSpec(
            num_scalar_prefetch=1, grid=(nb,),
            in_specs=[pl.BlockSpec((1, tr, w), lambda i, jc: (jc[0], i, 0)),
                      pl.BlockSpec((3, tr, w), lambda i, jc: (0, i, 0))],
            out_specs=pl.BlockSpec((tr, w), lambda i, jc: (jc[1] * nb + i, 0))),
        out_shape=jax.ShapeDtypeStruct((2 * rh, w), F32),
        compiler_params=_params(dimension_semantics=("arbitrary",)),
    )(jc_arr, ps, got)


def sibling_join_multi(reds, *, name):
    nb = len(reds)

    def body(*refs):
        ins, (send_sems, recv_sems) = refs[:nb], refs[2 * nb:]
        x, y, c, _ = _place()

        def copy(i, cc):
            rh = reds[i].shape[0] // 2
            rows = ins[i].at[pl.ds(cc * rh, rh), :]
            return pltpu.make_async_remote_copy(src_ref=rows, dst_ref=rows, send_sem=send_sems.at[i],
                                                recv_sem=recv_sems.at[i], device_id=(x, y, 1 - c), device_id_type=MESH)

        cps = [copy(i, c) for i in range(nb)]
        for cp in cps:
            cp.start()
        for i, cp in enumerate(cps):
            cp.wait_send()
            copy(i, 1 - c).wait_recv()

    return pl.pallas_call(
        body, name=name, in_specs=[_ANY] * nb, out_specs=[_ANY] * nb,
        out_shape=[jax.ShapeDtypeStruct(r.shape, r.dtype) for r in reds],
        input_output_aliases={i: i for i in range(nb)},
        scratch_shapes=[pltpu.SemaphoreType.DMA((nb,)), pltpu.SemaphoreType.DMA((nb,))],
    )(*reds)


def allreduce_small(part, *, name):
    m_per, n = part.shape

    def body(x_ref, sum_ref, all_ref, send_sems, recv_sems, local_sem):
        x, y, c, chips = _place()
        me, sib = (x, y, c), (x, y, 1 - c)

        def rows(px, py, pc):
            return all_ref.at[pl.ds((4 * px + 2 * py + pc) * m_per, m_per), :]

        def copy(kk, block, to, src=None):
            return pltpu.make_async_remote_copy(src_ref=rows(*block) if src is None else src, dst_ref=rows(*block),
                                                send_sem=send_sems.at[kk], recv_sem=recv_sems.at[kk],
                                                device_id=to, device_id_type=MESH)

        mine = pltpu.make_async_copy(x_ref, rows(*me), local_sem)
        mine.start()
        first = [copy(0, me, sib, src=x_ref)]
        first += [copy(1 + kk, me, (*chip, c), src=x_ref) for kk, chip in enumerate(chips)]
        for cp in first:
            cp.start()
        passed = [copy(4 + kk, (*chip, c), sib) for kk, chip in enumerate(chips)]
        for kk, chip in enumerate(chips):
            copy(1 + kk, (*chip, c), me).wait_recv()
            passed[kk].start()
        copy(0, sib, me).wait_recv()
        for kk, chip in enumerate(chips):
            copy(4 + kk, (*chip, 1 - c), me).wait_recv()
        for cp in first + passed:
            cp.wait_send()
        mine.wait()
        acc = all_ref[0:m_per, :]
        for d in range(1, 8):
            acc = acc + all_ref[d * m_per:(d + 1) * m_per, :]
        sum_ref[...] = acc

    vmem = pl.BlockSpec(memory_space=pltpu.VMEM)
    return pl.pallas_call(
        body, name=name, in_specs=[vmem], out_specs=vmem,
        out_shape=jax.ShapeDtypeStruct((m_per, n), part.dtype),
        scratch_shapes=[pltpu.VMEM((8 * m_per, n), part.dtype), pltpu.SemaphoreType.DMA((7,)),
                        pltpu.SemaphoreType.DMA((7,)), pltpu.SemaphoreType.DMA],
    )(part)


def adamw(w, g, m, v, *, name):
    r, c = w.shape
    tm = r if r * c * 4 <= (1 << 20) else _tile(r, max(8, ((1 << 20) // (c * 4)) // 8 * 8), unit=8)
    bc1, bc2 = 1.0 - ADAM_B1 ** ADAM_STEP, 1.0 - ADAM_B2 ** ADAM_STEP

    def fn(i, n, rv, cv, sc):
        w_, g_, m_, v_ = rv
        m_ = ADAM_B1 * m_ + (1.0 - ADAM_B1) * g_
        v_ = ADAM_B2 * v_ + (1.0 - ADAM_B2) * (g_ * g_)
        delta = -ADAM_LR * ((m_ / bc1) / (jnp.sqrt(v_ / bc2) + ADAM_EPS) + ADAM_WD * w_)
        return [delta, m_, v_], []

    return rowcall(fn, [w, g, m, v], [], [(c, F32)] * 3, [], tm=tm, name=name)[0]


def _head_one_hot(c):
    e = (lax.broadcasted_iota(jnp.int32, (c, LANES), 0) // HEAD
         == lax.broadcasted_iota(jnp.int32, (c, LANES), 1)).astype(F32)
    return e, e.T


def _join_cols(g):
    return jnp.concatenate([g[j] for j in range(4)], axis=1)


def _split_cols(a):
    return jnp.stack(jnp.split(a, 4, axis=1))


LATE = ("proj_rwkv", "proj_conv", "w_out", "xattn_wq", "xattn_wk", "xattn_wv", "xattn_wo", "mlp_w1", "mlp_w2")


def reduce_over_core_pair(names, g_list):
    gots = sibling_swap_multi(g_list, name=f"rs_sibling_swap_{names[0]}")
    return [pair_sum(g, got, tr=_half_tile(g.shape[1] // 2, g.shape[2]), name=f"rs_pair_sum_{n}")
            for n, g, got in zip(names, g_list, gots)]


def layer_step(x, mem, tgt, wg, sp, late_slots=None):
    s, d = x.shape
    dr, dc = sp["rwkv_w0"].shape[1], sp["conv_b"].shape[1]
    n_lora = N_DECAY + N_ICLR + N_GATE
    n_rwkv = 3 * dr + n_lora
    pad_l = N_LORA_PAD - n_lora
    w_in = _join_cols(wg["w_in"])
    cuts = [0, dr, 2 * dr, 3 * dr, n_rwkv, n_rwkv + dc, n_rwkv + 2 * dc, n_rwkv + 2 * dc + d, n_rwkv + 2 * dc + 2 * d]
    w_r, w_k, w_v, w_l, w_ca, w_cb, w_gr, w_gc = (w_in[:, lo:hi] for lo, hi in zip(cuts[:-1], cuts[1:]))
    w_l = jnp.pad(w_l, ((0, 0), (0, pad_l)))
    sm = sp["rwkv_shift_mix"]
    mus = [sm[:, 0:dr], sm[:, dr:2 * dr], sm[:, 2 * dr:3 * dr], jnp.pad(sm[:, 3 * dr:], ((0, 0), (0, pad_l)))]
    w_up = jnp.pad(_join_cols(wg["rwkv_w_up"]).astype(F32), ((0, LANES - N_DECAY), (0, 0)))
    a_up = jnp.pad(_join_cols(wg["rwkv_a_up"]).astype(F32), ((N_DECAY, 2 * LANES - N_DECAY - N_ICLR), (0, 0)))
    g_lo = N_DECAY + N_ICLR - LANES
    g_up = jnp.pad(_join_cols(wg["rwkv_g_up"]).astype(F32), ((g_lo, pad_l), (0, 0)))
    conv_w = _join_cols(wg["conv_w"])
    row_sharded = ("w_out", "xattn_wq", "xattn_wk", "xattn_wv", "xattn_wo", "mlp_w2")
    e, et = _head_one_hot(dr)
    pre_p = [sp["rwkv_w0"], w_up, sp["rwkv_a0"], a_up, g_up, sp["rwkv_k_k"], sp["rwkv_k_a"]]
    post_p = [sp["rwkv_r_k"], sp["rwkv_gn_g"], sp["rwkv_gn_b"]]
    pairs = dr // PAIR
    tm = min(s, 128)
    tmm = mem.shape[0]

    x_bf = x.astype(BF16)
    z_r, z_k, z_v, z_l = (mm(x_bf, w, name=f"z_{n}") for n, w in zip("rkvl", (w_r, w_k, w_v, w_l)))
    z_ca, z_cb = mm(x_bf, w_ca, name="z_ca"), mm(x_bf, w_cb, name="z_cb")
    z_gr, z_gc = mm(x_bf, w_gr, name="z_gr"), mm(x_bf, w_gc, name="z_gc")
    zs_r, zs_k, zs_v, zs_l = tokenshift_fwd([z_r, z_k, z_v, z_l], mus, tm=tm, name="shift_fwd")
    pre_o = [(dr, F32)] * 5
    k_m, lw, a_s, b_s, g = stage_fwd(f_rwkv_pre, [zs_k, zs_l], pre_p, [e, et], pre_o, tm=min(s, 64), name="pre_fwd")
    (o, ck, tpost, sa_rows), gathered = scan_fwd(zs_r, lw, k_m, zs_v, a_s, b_s, pg=min(8, pairs), name="scan_fwd",
                                                 gather=[late_slots[n] for n in LATE] if late_slots else ())
    wg = dict(wg, **dict(zip(LATE, gathered)))
    wt = {n: wg[n].reshape(-1, wg[n].shape[2]) for n in row_sharded}
    post_r = [o, zs_r, k_m, zs_v, g]
    (o_r,) = stage_fwd(f_rwkv_post, post_r, post_p, [e, et], [(dr, BF16)], tm=min(s, 64), name="post_fwd")
    (u,) = stage_fwd(f_glu, [z_ca, z_cb], [], [], [(dc, F32)], tm=tm, name="glu_fwd")
    cv = conv_fwd(u, conv_w, sp["conv_b"], tm=tm, name="conv_fwd")
    cln_p = [sp["conv_ln_g"], sp["conv_ln_b"]]
    (o_c,) = stage_fwd(f_convln, [cv], cln_p, [], [(dc, BF16)], tm=tm, name="convln_fwd")
    p_r = mm(o_r, wg["proj_rwkv"], b_sh=True, name="proj_r")
    p_c = mm(o_c, wg["proj_conv"], b_sh=True, name="proj_c")
    (merged,) = stage_fwd(f_merge, [z_gr, z_gc, p_r, p_c], [], [], [(d, BF16)], tm=tm, name="merge_fwd")
    y1 = mm(merged, wt["w_out"], name="y1")
    ln1_p, ln2_p, lnm_p = ([sp[f"{n}_g"], sp[f"{n}_b"]] for n in ("ln1", "ln2", "ln_mem"))
    h1, h1_bf = stage_fwd(f_resln_twice, [x, y1], ln1_p, [], [(d, F32), (d, BF16)], tm=tm, name="ln1_fwd")
    (mem_n,) = stage_fwd(f_ln, [mem], lnm_p, [], [(d, F32)], tm=tmm, name="lnmem_fwd")
    k_mem, v_mem = mm(mem_n, wt["xattn_wk"], name="k_mem"), mm(mem_n, wt["xattn_wv"], name="v_mem")
    q = mm(h1_bf, wt["xattn_wq"], name="q")
    (ao,) = stage_fwd(f_attn, [q], [k_mem, v_mem], [], [(d, BF16)], tm=tm, name="attn_fwd")
    ca = mm(ao, wt["xattn_wo"], name="ca")
    h2, h2_bf = stage_fwd(f_resln_twice, [h1, ca], ln2_p, [], [(d, F32), (d, BF16)], tm=tm, name="ln2_fwd")
    u1 = mm(h2_bf, wg["mlp_w1"], b_sh=True, name="u1")
    f_dim = u1.shape[1]
    tmf = min(s, 64)
    (act,) = stage_fwd(f_relu2, [u1], [], [], [(f_dim, BF16)], tm=tmf, name="relu2_fwd")
    ff = mm(act, wt["mlp_w2"], name="ff")

    gw, gs = {}, {}
    loss, dh2, dff, gs["ln3_g"], gs["ln3_b"] = loss_bwd(h2, ff, tgt, sp["ln3_g"], sp["ln3_b"], tm=tm, name="loss_bwd")
    gw["mlp_w2"] = mm(act, dff, ta=True, out_dtype=BF16, name="g_mlp_w2")
    dact = mm(dff, wt["mlp_w2"], tb=True, name="d_act")
    (du1,), _ = stage_bwd(f_relu2, [u1], [], [], [dact], [(0, BF16)], tm=tmf, name="relu2_bwd")
    gw["mlp_w1"] = mm(h2_bf, du1, ta=True, out_dtype=BF16, out_sh=True, name="g_mlp_w1")
    dh2 = mm(du1, wg["mlp_w1"], tb=True, b_sh=True, acc=dh2, name="d_h2")
    (dh1, dca), (gs["ln2_g"], gs["ln2_b"]) = stage_bwd(f_resln, [h1, ca], ln2_p, [], [dh2], [(0, F32), (1, BF16)],
                                                       tm=tm, name="ln2_bwd")
    gw["xattn_wo"] = mm(ao, dca, ta=True, out_dtype=BF16, name="g_wo")
    dao = mm(dca, wt["xattn_wo"], tb=True, name="d_ao")
    (dq,), (dk_mem, dv_mem) = stage_bwd(f_attn, [q], [k_mem, v_mem], [], [dao], [(0, BF16)], tm=tm, name="attn_bwd")
    gw["xattn_wq"] = mm(h1_bf, dq, ta=True, out_dtype=BF16, name="g_wq")
    dh1 = mm(dq, wt["xattn_wq"], tb=True, acc=dh1, name="d_h1")
    gw["xattn_wk"] = mm(mem_n, dk_mem, ta=True, out_dtype=BF16, name="g_wk")
    gw["xattn_wv"] = mm(mem_n, dv_mem, ta=True, out_dtype=BF16, name="g_wv")
    dmem_n = mm(dk_mem, wt["xattn_wk"], tb=True, name="d_memn_k")
    dmem_n = mm(dv_mem, wt["xattn_wv"], tb=True, acc=dmem_n, name="d_memn_v")
    _, (gs["ln_mem_g"], gs["ln_mem_b"]) = stage_bwd(f_ln, [mem], lnm_p, [], [dmem_n], [], tm=tmm, name="lnmem_bwd")
    (dx, dy1), (gs["ln1_g"], gs["ln1_b"]) = stage_bwd(f_resln, [x, y1], ln1_p, [], [dh1], [(0, F32), (1, BF16)],
                                                      tm=tm, name="ln1_bwd")
    gw["w_out"] = mm(merged, dy1, ta=True, out_dtype=BF16, name="g_w_out")
    dmerged = mm(dy1, wt["w_out"], tb=True, name="d_merged")
    (dz_gr, dz_gc, dp_r, dp_c), _ = stage_bwd(f_merge, [z_gr, z_gc, p_r, p_c], [], [], [dmerged],
                                              [(0, BF16), (1, BF16), (2, BF16), (3, BF16)], tm=tm, name="merge_bwd")
    gw["proj_rwkv"] = mm(o_r, dp_r, ta=True, out_dtype=BF16, out_sh=True, name="g_proj_r")
    gw["proj_conv"] = mm(o_c, dp_c, ta=True, out_dtype=BF16, out_sh=True, name="g_proj_c")
    do_r = mm(dp_r, wg["proj_rwkv"], tb=True, b_sh=True, name="d_o_r")
    do_c = mm(dp_c, wg["proj_conv"], tb=True, b_sh=True, name="d_o_c")
    (dcv,), (gs["conv_ln_g"], gs["conv_ln_b"]) = stage_bwd(f_convln, [cv], cln_p, [], [do_c], [(0, F32)],
                                                           tm=tm, name="convln_bwd")
    du, g_conv_w, gs["conv_b"] = conv_bwd(dcv, u, conv_w, tm=tm, name="conv_bwd")
    gw["conv_w"] = _split_cols(g_conv_w.astype(BF16))
    (dz_ca, dz_cb), _ = stage_bwd(f_glu, [z_ca, z_cb], [], [], [du], [(0, BF16), (1, BF16)], tm=tm, name="glu_bwd")
    (d_o, dr_p, dk_p, dv_p, dg), (gs["rwkv_r_k"], gs["rwkv_gn_g"], gs["rwkv_gn_b"]) = stage_bwd(
        f_rwkv_post, post_r, post_p, [e, et], [do_r], [(k, F32) for k in range(5)], tm=min(s, 64), name="post_bwd")
    for n in row_sharded:
        gw[n] = gw[n].reshape(wg[n].shape)
    pss = reduce_over_core_pair(LATE, [gw[n] for n in LATE]) if late_slots else []
    (dzs_r, dlw, dk_m, dzs_v, da_s, db_s), sent = scan_bwd(zs_r, lw, k_m, zs_v, a_s, b_s, ck, tpost, sa_rows, d_o, dr_p,
                                                           dk_p, dv_p, pg=min(8, pairs), name="scan_bwd", scatter=pss)
    (dzs_k, dzs_l), pre_g = stage_bwd(f_rwkv_pre, [zs_k, zs_l], pre_p, [e, et], [dk_m, dlw, da_s, db_s, dg],
                                      [(0, F32), (1, F32)], tm=min(s, 64), name="pre_bwd")
    gs["rwkv_w0"], g_w_up, gs["rwkv_a0"], g_a_up, g_g_up, gs["rwkv_k_k"], gs["rwkv_k_a"] = pre_g
    gw["rwkv_w_up"] = _split_cols(g_w_up[0:N_DECAY].astype(BF16))
    gw["rwkv_a_up"] = _split_cols(g_a_up[N_DECAY:N_DECAY + N_ICLR].astype(BF16))
    gw["rwkv_g_up"] = _split_cols(g_g_up[g_lo:g_lo + N_GATE].astype(BF16))
    dzs, dmus = tokenshift_bwd([dzs_r, dzs_k, dzs_v, dzs_l], [z_r, z_k, z_v, z_l], mus, tm=tm, name="shift_bwd")
    gs["rwkv_shift_mix"] = jnp.concatenate(list(dmus[:3]) + [dmus[3][:, 0:n_lora]], axis=1)
    dzs = list(dzs) + [dz_ca, dz_cb, dz_gr, dz_gc]
    g_in = []
    for n, dz, w in zip(("r", "k", "v", "l", "ca", "cb", "gr", "gc"), dzs, (w_r, w_k, w_v, w_l, w_ca, w_cb, w_gr, w_gc)):
        g_in.append(mm(x_bf, dz, ta=True, out_dtype=BF16, name=f"g_w_{n}"))
        dx = mm(dz, w, tb=True, acc=dx, name=f"d_x_{n}")
    g_in[3] = g_in[3][:, 0:n_lora]
    gw["w_in"] = _split_cols(jnp.concatenate(g_in, axis=1))
    return loss, dx, gw, gs, ((pss, sent) if late_slots else None)


WEIGHTS = ["w_in", "rwkv_shift_mix", "rwkv_w0", "rwkv_w_up", "rwkv_a0", "rwkv_a_up", "rwkv_g_up", "rwkv_k_k",
           "rwkv_k_a", "rwkv_r_k", "rwkv_gn_g", "rwkv_gn_b", "conv_w", "conv_b", "conv_ln_g", "conv_ln_b",
           "proj_rwkv", "proj_conv", "w_out", "ln1_g", "ln1_b", "ln_mem_g", "ln_mem_b", "xattn_wq", "xattn_wk",
           "xattn_wv", "xattn_wo", "ln2_g", "ln2_b", "mlp_w1", "mlp_w2", "ln3_g", "ln3_b"]
SHARD_AXIS = {"w_in": 1, "rwkv_w_up": 1, "rwkv_a_up": 1, "rwkv_g_up": 1, "conv_w": 1, "proj_rwkv": 1, "proj_conv": 1,
              "w_out": 0, "xattn_wq": 0, "xattn_wk": 0, "xattn_wv": 0, "xattn_wo": 0, "mlp_w1": 1, "mlp_w2": 0}


def _unpack(flat, shapes):
    out, off = [], 0
    for shp in shapes:
        n = 1
        for dim in shp:
            n *= dim
        out.append(flat[..., off:off + n].reshape(flat.shape[:-1] + tuple(shp)))
        off += n
    return out


def _half_tile(rh, w):
    return _tile(rh, max(16, (2 << 20) // (w * 4) // 16 * 16), unit=16)


def kernel(*args):
    n_w = len(WEIGHTS)
    x, mem = args[0][0], args[1][0]
    tgt = args[2 + n_w][0]
    w_loc = {n: a for n, a in zip(WEIGHTS, args[2:2 + n_w])}
    m_loc = {n: a for n, a in zip(WEIGHTS, args[3 + n_w:3 + 2 * n_w])}
    v_loc = {n: a for n, a in zip(WEIGHTS, args[3 + 2 * n_w:3 + 3 * n_w])}
    big = [n for n in WEIGHTS if n in SHARD_AXIS]
    small = [n for n in WEIGHTS if n not in SHARD_AXIS]

    def as2d(n, a):
        if n in SHARD_AXIS:
            return a.reshape(a.shape[1], a.shape[-1])
        return a.reshape(1, -1)

    loc2d = {n: as2d(n, w_loc[n]) for n in WEIGHTS}
    conv_rows = loc2d["conv_w"].shape[0]
    slots = []
    for n in big:
        if n == "conv_w":
            padded = jnp.pad(loc2d[n], ((0, CONV_HALO - conv_rows), (0, 0)))
            slots.append(into_slot(padded, F32, name=f"slot_{n}"))
        else:
            slots.append(into_slot(loc2d[n], BF16, name=f"slot_{n}"))
    slots = dict(zip(big, slots))
    early = [n for n in big if n not in LATE]
    wg = dict(zip(early, allgather_multi([slots[n] for n in early], name="allgather_weights")))
    sp = {n: loc2d[n] for n in small}

    loss_part, grad_x, gw, gs, (pss_late, sent_late) = layer_step(x, mem, tgt, wg, sp, {n: slots[n] for n in LATE})

    pss = reduce_over_core_pair(early, [gw[n] for n in early])
    sent = list(scatter_multi(pss, name="rs_scatter"))
    order = early + list(LATE)
    reds = [chip_sum(ps, got, tr=_half_tile(ps.shape[1], ps.shape[2]), name=f"rs_chip_sum_{n}")
            for n, ps, got in zip(order, pss + pss_late, sent + sent_late)]
    reds = sibling_join_multi(reds, name="rs_sibling_join")
    g_big = {n: (r[0:conv_rows] if n == "conv_w" else r) for n, r in zip(order, reds)}

    small_parts = [gs[n] for n in small] + [loss_part[0:1, 0:1]]
    flat = jnp.concatenate([p.reshape(-1).astype(F32) for p in small_parts])
    flat = jnp.pad(flat, (0, -flat.shape[0] % (8 * LANES))).reshape(-1, LANES)
    red = allreduce_small(flat, name="allreduce_small").reshape(-1)
    g_small = dict(zip(small, _unpack(red, [loc2d[n].shape for n in small])))
    n_small = sum(loc2d[n].shape[1] for n in small)
    loss = red[n_small]

    grads, deltas, new_m, new_v = {}, {}, {}, {}
    for n in big:
        g2 = g_big[n]
        d2, m2, v2 = adamw(loc2d[n], g2, as2d(n, m_loc[n]), as2d(n, v_loc[n]), name=f"adamw_{n}")
        shp = w_loc[n].shape
        grads[n], deltas[n], new_m[n], new_v[n] = (t.reshape(shp) for t in (g2, d2, m2, v2))

    def small_pack(d):
        f = jnp.concatenate([as2d(n, d[n]).reshape(-1) for n in small])
        return jnp.pad(f, (0, -f.shape[0] % (8 * LANES))).reshape(-1, LANES)

    g_pack = small_pack({n: g_small[n] for n in small})
    outs = adamw(small_pack(w_loc), g_pack, small_pack(m_loc), small_pack(v_loc), name="adamw_small")
    for dst, packed in zip((deltas, new_m, new_v), outs):
        for n, t in zip(small, _unpack(packed.reshape(-1), [loc2d[n].shape for n in small])):
            dst[n] = t.reshape(w_loc[n].shape)
    for n in small:
        grads[n] = g_small[n].reshape(w_loc[n].shape)

    return (loss, grad_x[None], *[grads[n] for n in WEIGHTS], *[deltas[n] for n in WEIGHTS],
            *[new_m[n] for n in WEIGHTS], *[new_v[n] for n in WEIGHTS])
```
